```python
import jax
import jax.numpy as jnp
from jax import lax
import numpy as np

D_MODEL = 1024
BATCH = 2
SEQ = 8192
DEPTH = 1

MEM_LEN = 256
EPS = 1e-6

GLA_HEADS = 4
GLA_DK = D_MODEL // 8
GLA_DV = D_MODEL // 4
GLA_GATE_RANK = 16
GLA_GATE_TEMP = 16.0
GLA_CHUNK = 64

DSA_HEADS = 8
DSA_KV_HEADS = 2
DSA_HEAD_DIM = 128
IDX_HEADS = 8
IDX_DIM = 64
IDX_TOPK_MAX = 256
Q_BLOCK = 128

MEM_HEADS = 4
MEM_HEAD_DIM = D_MODEL // 4

N_GROUPS = 4
EXPERTS_PER_GROUP = 4
N_EXPERTS = N_GROUPS * EXPERTS_PER_GROUP
EXPERT_TOPK = 2
D_EXPERT = D_MODEL // 4

N_BRANCH = 3
IN_SPLITS = (
    GLA_HEADS * GLA_DK,
    GLA_HEADS * GLA_DK,
    GLA_HEADS * GLA_DV,
    GLA_HEADS * GLA_DV,
    GLA_GATE_RANK,
    DSA_HEADS * DSA_HEAD_DIM,
    DSA_KV_HEADS * DSA_HEAD_DIM,
    DSA_KV_HEADS * DSA_HEAD_DIM,
    IDX_HEADS * IDX_DIM,
    IDX_DIM,
    IDX_HEADS,
    MEM_HEADS * MEM_HEAD_DIM,
    N_BRANCH * D_MODEL,
)
D_IN = sum(IN_SPLITS)

kernel_name = 'hybrid_gla_dsa_memxattn_hmoe'


def _rmsnorm(x, g):
    xf = x.astype(jnp.float32)
    y = xf * lax.rsqrt(jnp.mean(xf * xf, axis=-1, keepdims=True) + EPS)
    return y.astype(x.dtype) * g


def _split_in(z):
    offs = []
    acc = 0
    for w in IN_SPLITS[:-1]:
        acc += w
        offs.append(acc)
    return jnp.split(z, offs, axis=-1)


def _gla(q, k, v, g_out, a_low, w_a2, b_a2, norm_g):
    B, T = q.shape[:2]
    C = GLA_CHUNK
    N = T // C
    shp_k = (B, N, C, GLA_HEADS, GLA_DK)
    qf = q.astype(jnp.float32).reshape(shp_k) * GLA_DK ** -0.5
    kf = k.astype(jnp.float32).reshape(shp_k)
    vf = v.astype(jnp.float32).reshape(B, N, C, GLA_HEADS, GLA_DV)
    log_a = jax.nn.log_sigmoid((a_low @ w_a2 + b_a2).astype(jnp.float32)) / GLA_GATE_TEMP
    cum = jnp.cumsum(log_a.reshape(shp_k), axis=2)
    last = cum[:, :, -1:]
    q_dec = qf * jnp.exp(cum)
    k_inv = kf * jnp.exp(-cum)
    k_end = kf * jnp.exp(last - cum)
    causal = jnp.tril(jnp.ones((C, C), dtype=bool))
    att = jnp.where(causal, jnp.einsum('bnihd,bnjhd->bnhij', q_dec, k_inv), 0.0)
    o_intra = jnp.einsum('bnhij,bnjhv->bnihv', att, vf)
    upd = jnp.einsum('bnjhd,bnjhv->nbhdv', k_end, vf)
    decay = jnp.exp(last[:, :, 0]).transpose(1, 0, 2, 3)

    def step(state, inp):
        dec, u = inp
        return dec[..., None] * state + u, state

    s0 = jnp.zeros((B, GLA_HEADS, GLA_DK, GLA_DV), jnp.float32)
    _, s_prev = lax.scan(step, s0, (decay, upd))
    o_inter = jnp.einsum('bnihd,nbhdv->bnihv', q_dec, s_prev)
    o = (o_intra + o_inter).reshape(B, T, GLA_HEADS, GLA_DV)
    o = _rmsnorm(o, norm_g.astype(jnp.float32)).reshape(B, T, GLA_HEADS * GLA_DV)
    return (o * jax.nn.silu(g_out.astype(jnp.float32))).astype(q.dtype)


def _dsa(q, k, v, q_idx, k_idx, w_idx, q_gain, k_gain, kidx_gain):
    B, T = q.shape[:2]
    topk = min(IDX_TOPK_MAX, T // 4)
    grp = DSA_HEADS // DSA_KV_HEADS
    q = _rmsnorm(q.reshape(B, T, DSA_HEADS, DSA_HEAD_DIM), q_gain) * DSA_HEAD_DIM ** -0.5
    k = _rmsnorm(k.reshape(B, T, DSA_KV_HEADS, DSA_HEAD_DIM), k_gain)
    v = v.reshape(B, T, DSA_KV_HEADS, DSA_HEAD_DIM)
    q_idx = q_idx.reshape(B, T, IDX_HEADS, IDX_DIM)
    k_idx = _rmsnorm(k_idx, kidx_gain)
    w_idx = w_idx * (IDX_HEADS ** -0.5 * IDX_DIM ** -0.5)
    key_pos = jnp.arange(T)

    def block(i):
        start = i * Q_BLOCK
        qb = lax.dynamic_slice_in_dim(q, start, Q_BLOCK, axis=1)
        qib = lax.dynamic_slice_in_dim(q_idx, start, Q_BLOCK, axis=1)
        wb = lax.dynamic_slice_in_dim(w_idx, start, Q_BLOCK, axis=1)
        qpos = start + jnp.arange(Q_BLOCK)
        rel = jax.nn.relu(jnp.einsum('bqhd,bsd->bqhs', qib, k_idx).astype(jnp.float32))
        score = jnp.einsum('bqh,bqhs->bqs', wb.astype(jnp.float32), rel)
        allowed = key_pos[None, :] <= qpos[:, None]
        score = jnp.where(allowed, score, -jnp.inf)
        _, idx = lax.top_k(score, topk)
        valid = idx <= qpos[None, :, None]
        kg = jax.vmap(lambda kk, ii: kk[ii])(k, idx)
        vg = jax.vmap(lambda vv, ii: vv[ii])(v, idx)
        qg = qb.reshape(B, Q_BLOCK, DSA_KV_HEADS, grp, DSA_HEAD_DIM)
        logits = jnp.einsum('bqgrd,bqkgd->bqgrk', qg, kg).astype(jnp.float32)
        logits = jnp.where(valid[:, :, None, None, :], logits, -jnp.inf)
        p = jax.nn.softmax(logits, axis=-1).astype(v.dtype)
        o = jnp.einsum('bqgrk,bqkgd->bqgrd', p, vg)
        return o.reshape(B, Q_BLOCK, DSA_HEADS * DSA_HEAD_DIM)

    out = lax.map(block, jnp.arange(T // Q_BLOCK))
    return out.transpose(1, 0, 2, 3).reshape(B, T, DSA_HEADS * DSA_HEAD_DIM)


def _memory_attn(q, mem_h, w_kv, q_gain, k_gain):
    B, T = q.shape[:2]
    M = mem_h.shape[1]
    q = _rmsnorm(q.reshape(B, T, MEM_HEADS, MEM_HEAD_DIM), q_gain) * MEM_HEAD_DIM ** -0.5
    kv = (mem_h @ w_kv).reshape(B, M, 2, MEM_HEADS, MEM_HEAD_DIM)
    k = _rmsnorm(kv[:, :, 0], k_gain)
    v = kv[:, :, 1]
    logits = jnp.einsum('bthd,bmhd->bhtm', q, k).astype(jnp.float32)
    p = jax.nn.softmax(logits, axis=-1).astype(v.dtype)
    o = jnp.einsum('bhtm,bmhd->bthd', p, v)
    return o.reshape(B, T, MEM_HEADS * MEM_HEAD_DIM)


def _hier_moe(h, w_r1, b_r1, w_r2, b_r2, w_gate, w_up, w_down):
    B, T, D = h.shape
    hf = h.reshape(B * T, D)
    g_logits = (hf @ w_r1 + b_r1).astype(jnp.float32)
    g_prob = jax.nn.softmax(g_logits, axis=-1)
    g_sel = jnp.argmax(g_logits, axis=-1)
    g_w = jnp.take_along_axis(g_prob, g_sel[:, None], axis=-1)
    e_logits = (hf @ w_r2 + b_r2).astype(jnp.float32).reshape(-1, N_GROUPS, EXPERTS_PER_GROUP)
    e_logits = jnp.take_along_axis(e_logits, g_sel[:, None, None], axis=1)[:, 0]
    e_prob = jax.nn.softmax(e_logits, axis=-1)
    top_p, top_i = lax.top_k(e_prob, EXPERT_TOPK)
    top_p = top_p / jnp.sum(top_p, axis=-1, keepdims=True)
    expert_id = g_sel[:, None] * EXPERTS_PER_GROUP + top_i
    combine = jnp.sum(jax.nn.one_hot(expert_id, N_EXPERTS, dtype=jnp.float32)
                      * (g_w * top_p)[..., None], axis=1)
    hid = jax.nn.silu(jnp.einsum('nd,edf->nef', hf, w_gate)) * jnp.einsum('nd,edf->nef', hf, w_up)
    hid = hid * combine[..., None].astype(hid.dtype)
    out = jnp.einsum('nef,efd->nd', hid, w_down)
    return out.reshape(B, T, D)


def setup_inputs(seed: int = 0) -> dict:
    key = jax.random.key(seed)
    ks = iter(jax.random.split(key, 40))
    L = DEPTH

    def w(shape, fan_in):
        return jax.random.normal(next(ks), shape, jnp.float32) * fan_in ** -0.5

    def gain(shape):
        return 1.0 + 0.02 * jax.random.normal(next(ks), shape, jnp.float32)

    def bias(shape, s=0.01):
        return s * jax.random.normal(next(ks), shape, jnp.float32)

    return {
        'x': jax.random.normal(next(ks), (BATCH, SEQ, D_MODEL), jnp.float32),
        'mem': jax.random.normal(next(ks), (BATCH, MEM_LEN, D_MODEL), jnp.float32),
        'g_mix': gain((L, D_MODEL)),
        'g_mem': gain((L, D_MODEL)),
        'w_in': w((L, D_MODEL, D_IN), D_MODEL),
        'w_gla_a2': w((L, GLA_GATE_RANK, GLA_HEADS * GLA_DK), GLA_GATE_RANK),
        'b_gla_a2': bias((L, GLA_HEADS * GLA_DK), 0.1),
        'gla_norm': gain((L, GLA_DV)),
        'w_mem_kv': w((L, D_MODEL, 2 * MEM_HEADS * MEM_HEAD_DIM), D_MODEL),
        'dsa_q_norm': gain((L, DSA_HEAD_DIM)),
        'dsa_k_norm': gain((L, DSA_HEAD_DIM)),
        'idx_k_norm': gain((L, IDX_DIM)),
        'mem_q_norm': gain((L, MEM_HEAD_DIM)),
        'mem_k_norm': gain((L, MEM_HEAD_DIM)),
        'b_gate': bias((L, N_BRANCH * D_MODEL), 0.1),
        'w_o_gla': w((L, GLA_HEADS * GLA_DV, D_MODEL), GLA_HEADS * GLA_DV),
        'w_o_dsa': w((L, DSA_HEADS * DSA_HEAD_DIM, D_MODEL), DSA_HEADS * DSA_HEAD_DIM),
        'w_o_mem': w((L, MEM_HEADS * MEM_HEAD_DIM, D_MODEL), MEM_HEADS * MEM_HEAD_DIM),
        'w_out': w((L, D_MODEL, D_MODEL), D_MODEL),
        'g_ffn': gain((L, D_MODEL)),
        'w_r1': w((L, D_MODEL, N_GROUPS), D_MODEL),
        'b_r1': bias((L, N_GROUPS)),
        'w_r2': w((L, D_MODEL, N_EXPERTS), D_MODEL),
        'b_r2': bias((L, N_EXPERTS)),
        'w_gate': w((L, N_EXPERTS, D_MODEL, D_EXPERT), D_MODEL),
        'w_up': w((L, N_EXPERTS, D_MODEL, D_EXPERT), D_MODEL),
        'w_down': w((L, N_EXPERTS, D_EXPERT, D_MODEL), D_EXPERT),
    }


def reference(x, mem, g_mix, g_mem, w_in, w_gla_a2, b_gla_a2, gla_norm, w_mem_kv,
              dsa_q_norm, dsa_k_norm, idx_k_norm, mem_q_norm, mem_k_norm, b_gate,
              w_o_gla, w_o_dsa, w_o_mem, w_out, g_ffn, w_r1, b_r1, w_r2, b_r2,
              w_gate, w_up, w_down):
    B, T, D = x.shape
    for l in range(DEPTH):
        h = _rmsnorm(x, g_mix[l])
        (gq, gk, gv, gg, ga, dq, dk, dv, iq, ik, iw, mq, gates) = _split_in(h @ w_in[l])
        o_gla = _gla(gq, gk, gv, gg, ga, w_gla_a2[l], b_gla_a2[l], gla_norm[l])
        o_dsa = _dsa(dq, dk, dv, iq, ik, iw, dsa_q_norm[l], dsa_k_norm[l], idx_k_norm[l])
        mem_h = _rmsnorm(mem, g_mem[l])
        o_mem = _memory_attn(mq, mem_h, w_mem_kv[l], mem_q_norm[l], mem_k_norm[l])
        gate = jax.nn.sigmoid((gates + b_gate[l]).astype(jnp.float32)).astype(x.dtype)
        gate = gate.reshape(B, T, N_BRANCH, D)
        merged = (gate[:, :, 0] * (o_gla @ w_o_gla[l])
                  + gate[:, :, 1] * (o_dsa @ w_o_dsa[l])
                  + gate[:, :, 2] * (o_mem @ w_o_mem[l]))
        x = x + merged @ w_out[l]
        x = x + _hier_moe(_rmsnorm(x, g_ffn[l]), w_r1[l], b_r1[l], w_r2[l], b_r2[l],
                          w_gate[l], w_up[l], w_down[l])
    return x
```

```python
import functools

import jax
import jax.numpy as jnp
from jax import lax
from jax.experimental import pallas as pl
from jax.experimental.pallas import tpu as pltpu

F32 = jnp.float32
BF16 = jnp.bfloat16
I32 = jnp.int32

D_MODEL = 1024
EPS = 1e-6

GLA_HEADS = 4
GLA_DK = 128
GLA_DV = 256
GLA_GATE_RANK = 16
GLA_GATE_TEMP = 16.0
GLA_CHUNK = 64

DSA_HEADS = 8
DSA_KV_HEADS = 2
DSA_HEAD_DIM = 128
IDX_HEADS = 8
IDX_DIM = 64
IDX_TOPK_MAX = 256

MEM_HEADS = 4
MEM_HEAD_DIM = 256

N_GROUPS = 4
EXPERTS_PER_GROUP = 4
N_EXPERTS = 16
D_EXPERT = 256

LANE = 128
INT_MIN = -(2 ** 31)
INT_MAX = 2 ** 31 - 1
MASKED_LOGIT = -1e30

_OFF = {}
_acc = 0
for _name, _w in (("gq", 512), ("gk", 512), ("gv", 1024), ("gg", 1024), ("ga", 16), ("dq", 1024),
                  ("dk", 256), ("dv", 256), ("iq", 512), ("ik", 64), ("iw", 8), ("mq", 1024),
                  ("gates", 3072)):
    _OFF[_name] = (_acc, _acc + _w)
    _acc += _w


def _cols(w_in, name):
    lo, hi = _OFF[name]
    return w_in[:, lo:hi]


def _pad_cols(w, width):
    return jnp.pad(w, ((0, 0), (0, width - w.shape[1])))


def _rms_rows(x):
    return x * lax.rsqrt(jnp.mean(x * x, axis=-1, keepdims=True) + EPS)


def _dot(a, b):
    return jnp.dot(a, b, preferred_element_type=F32)


def _dot_nt(a, b):
    return lax.dot_general(a, b, (((1,), (1,)), ((), ())), preferred_element_type=F32)


def _dot_tn(a, b):
    return lax.dot_general(a, b, (((0,), (0,)), ((), ())), preferred_element_type=F32)


def _sigmoid(x):
    return 1.0 / (1.0 + jnp.exp(-x))


def _params(sem, vmem_mb):
    return pltpu.CompilerParams(dimension_semantics=sem, vmem_limit_bytes=vmem_mb * 1024 * 1024)


def _proj_body(epilogue, n_extra, x_ref, g_ref, w_ref, *rest):
    extra = rest[:n_extra]
    outs = rest[n_extra:-1]
    h_scr = rest[-1]

    @pl.when(pl.program_id(1) == 0)
    def _():
        h_scr[...] = (_rms_rows(x_ref[...]) * g_ref[...]).astype(BF16)

    epilogue(_dot(h_scr[...], w_ref[...]), extra, outs)


def _proj_call(name, x2d, gain, w, epilogue, extras, out_shapes, out_widths, tm, tn, vmem_mb):
    n, d = x2d.shape
    c = w.shape[1]
    in_specs = [pl.BlockSpec((tm, d), lambda i, j: (i, 0)),
                pl.BlockSpec((1, d), lambda i, j: (0, 0)),
                pl.BlockSpec((d, tn), lambda i, j: (0, j))]
    single = c == tn
    for e in extras:
        if e.shape[1] == c and not single:
            in_specs.append(pl.BlockSpec((1, tn), lambda i, j: (0, j)))
        else:
            in_specs.append(pl.BlockSpec(e.shape, lambda i, j: (0, 0)))
    out_specs = [pl.BlockSpec((tm, ow), (lambda i, j: (i, 0)) if single else (lambda i, j: (i, j)))
                 for ow in out_widths]
    return pl.pallas_call(
        functools.partial(_proj_body, epilogue, len(extras)),
        grid=(n // tm, c // tn),
        in_specs=in_specs,
        out_specs=out_specs,
        out_shape=out_shapes,
        scratch_shapes=[pltpu.VMEM((tm, d), BF16)],
        compiler_params=_params(("parallel", "arbitrary"), vmem_mb),
        name=name,
    )(x2d, gain.reshape(1, d), w, *extras)


def _epi_plain(acc, extra, outs):
    outs[0][...] = acc.astype(outs[0].dtype)


def _epi_headnorm(head_dim, scale, acc, extra, outs):
    gain = extra[0][...]
    for h in range(acc.shape[1] // head_dim):
        sl = slice(h * head_dim, (h + 1) * head_dim)
        outs[0][:, sl] = (_rms_rows(acc[:, sl]) * gain * scale).astype(outs[0].dtype)


def _epi_dsa_kv(acc, extra, outs):
    k_gain, idx_gain = extra
    kn_ref, v_ref, ki_ref, wi_ref = outs
    for h in range(DSA_KV_HEADS):
        sl = slice(h * DSA_HEAD_DIM, (h + 1) * DSA_HEAD_DIM)
        kn_ref[:, sl] = (_rms_rows(acc[:, sl]) * k_gain[...]).astype(BF16)
    v_ref[...] = acc[:, 256:512].astype(BF16)
    a = acc[:, 512:640]
    ms = jnp.sum(a * a, axis=-1, keepdims=True) * (1.0 / IDX_DIM)
    ki_ref[...] = (a * lax.rsqrt(ms + EPS) * idx_gain[...]).astype(BF16)
    wi_ref[...] = acc[:, 640:768] * (IDX_HEADS ** -0.5 * IDX_DIM ** -0.5)


def _epi_gate(acc, extra, outs):
    outs[0][...] = _sigmoid(acc + extra[0][...])


def _epi_mem_kv(acc, extra, outs):
    gain = extra[0][...]
    j = pl.program_id(1)

    @pl.when(j == 0)
    def _():
        for h in range(MEM_HEADS):
            sl = slice(h * MEM_HEAD_DIM, (h + 1) * MEM_HEAD_DIM)
            outs[0][:, sl] = (_rms_rows(acc[:, sl]) * gain).astype(BF16)

    @pl.when(j == 1)
    def _():
        outs[0][...] = acc.astype(BF16)


def _split3(x):
    hi = x.astype(BF16)
    r1 = x - hi.astype(F32)
    mid = r1.astype(BF16)
    lo = (r1 - mid.astype(F32)).astype(BF16)
    return hi, mid, lo


def _gla_body(q_ref, k_ref, v_ref, g_ref, a_ref, wa_ref, ba_ref, ng_ref, o_ref,
              st_scr, qd_scr, ki_scr, ke_scr, dec_scr):
    tb = q_ref.shape[0]
    c = GLA_CHUNK

    @pl.when(pl.program_id(1) == 0)
    def _():
        st_scr[...] = jnp.zeros_like(st_scr)

    pre = _dot(a_ref[...].astype(BF16), wa_ref[...]) + ba_ref[...]
    log_a = (jnp.minimum(pre, 0.0) - jnp.log1p(jnp.exp(-jnp.abs(pre)))) * (1.0 / GLA_GATE_TEMP)

    row = lax.broadcasted_iota(I32, (tb, tb), 0)
    col = lax.broadcasted_iota(I32, (tb, tb), 1)
    shift = c.bit_length() - 1
    same_chunk = (row >> shift) == (col >> shift)
    chunk_ones = jnp.where(same_chunk, 1.0, 0.0)
    tri = jnp.where(col <= row, chunk_ones, 0.0).astype(BF16)
    blk = chunk_ones.astype(BF16)
    parts = _split3(log_a)
    cum = _dot(tri, parts[0]) + _dot(tri, parts[1]) + _dot(tri, parts[2])
    tot = _dot(blk, parts[0]) + _dot(blk, parts[1]) + _dot(blk, parts[2])

    kf = k_ref[...]
    qd_scr[...] = (q_ref[...] * (GLA_DK ** -0.5) * jnp.exp(cum)).astype(BF16)
    ki_scr[...] = (kf * jnp.exp(-cum)).astype(BF16)
    ke_scr[...] = (kf * jnp.exp(tot - cum)).astype(BF16)
    dec_scr[...] = jnp.exp(tot)

    causal = lax.broadcasted_iota(I32, (c, c), 1) <= lax.broadcasted_iota(I32, (c, c), 0)

    def chunk(ci, carry):
        r0 = pl.multiple_of(ci * c, c)
        rows = pl.ds(r0, c)
        for h in range(GLA_HEADS):
            ks = slice(h * GLA_DK, (h + 1) * GLA_DK)
            vs = slice(h * GLA_DV, (h + 1) * GLA_DV)
            qd = qd_scr[rows, ks]
            vv = v_ref[rows, vs].astype(BF16)
            att = jnp.where(causal, _dot_nt(qd, ki_scr[rows, ks]), 0.0).astype(BF16)
            st = st_scr[h]
            o = _dot(att, vv) + _dot_nt(qd, st.astype(BF16))
            st_scr[h] = st * dec_scr[pl.ds(r0, 1), ks] + _dot_tn(vv, ke_scr[rows, ks])
            gg = g_ref[rows, vs]
            o_ref[rows, vs] = (_rms_rows(o) * ng_ref[...] * (gg * _sigmoid(gg))).astype(o_ref.dtype)
        return carry

    lax.fori_loop(0, tb // c, chunk, 0)


def _gla_call(zg, wa, ba, ng, batch, seq, tb):
    n = zg.shape[0]
    nb = seq // tb
    row = lambda b, i: b * nb + i
    in_specs = [
        pl.BlockSpec((tb, 512), lambda b, i: (row(b, i), 0)),
        pl.BlockSpec((tb, 512), lambda b, i: (row(b, i), 1)),
        pl.BlockSpec((tb, 1024), lambda b, i: (row(b, i), 1)),
        pl.BlockSpec((tb, 1024), lambda b, i: (row(b, i), 2)),
        pl.BlockSpec((tb, LANE), lambda b, i: (row(b, i), 3072 // LANE)),
        pl.BlockSpec(wa.shape, lambda b, i: (0, 0)),
        pl.BlockSpec(ba.shape, lambda b, i: (0, 0)),
        pl.BlockSpec(ng.shape, lambda b, i: (0, 0)),
    ]
    hk = GLA_HEADS * GLA_DK
    return pl.pallas_call(
        _gla_body,
        grid=(batch, nb),
        in_specs=in_specs,
        out_specs=pl.BlockSpec((tb, GLA_HEADS * GLA_DV), lambda b, i: (row(b, i), 0)),
        out_shape=jax.ShapeDtypeStruct((n, GLA_HEADS * GLA_DV), BF16),
        scratch_shapes=[pltpu.VMEM((GLA_HEADS, GLA_DV, GLA_DK), F32),
                        pltpu.VMEM((tb, hk), BF16), pltpu.VMEM((tb, hk), BF16),
                        pltpu.VMEM((tb, hk), BF16), pltpu.VMEM((tb, hk), F32)],
        compiler_params=_params(("parallel", "arbitrary"), 48),
        name="gla",
    )(zg, zg, zg, zg, zg, wa, ba, ng)


DSA_TQ = 128
DSA_TK = 512


def _dsa_body(topk, qn_ref, qi_ref, wi_ref, kn_ref, v_ref, ki_ref, o_ref,
              key_scr, q_scr, m_scr, l_scr, acc_scr):
    tq, tk = DSA_TQ, DSA_TK
    grp = DSA_HEADS // DSA_KV_HEADS
    q0 = pl.program_id(1) * tq
    nk = (q0 + tq + tk - 1) // tk
    qpos = q0 + lax.broadcasted_iota(I32, (tq, tk), 0)
    kofs = lax.broadcasted_iota(I32, (tq, tk), 1)

    def chunk_start(ci):
        return pl.multiple_of(ci * tk, tk)

    wi = wi_ref[...]

    def score_chunk(ci, carry):
        k0 = chunk_start(ci)
        kc = ki_ref[pl.ds(k0, tk), :]
        score = jnp.zeros((tq, tk), F32)
        for h in range(IDX_HEADS):
            qh = qi_ref[:, h * LANE:(h + 1) * LANE]
            score = score + wi[:, h:h + 1] * jnp.maximum(_dot_nt(qh, kc), 0.0)
        score = jnp.where(score == 0.0, 0.0, score)
        bits = lax.bitcast_convert_type(score, I32)
        key = bits ^ ((bits >> 31) & INT_MAX)
        key_scr[:, pl.ds(k0, tk)] = jnp.where(k0 + kofs <= qpos, key, INT_MIN)
        return carry

    lax.fori_loop(0, nk, score_chunk, 0)

    def count_ge(thr):
        thr_b = jnp.broadcast_to(thr, (tq, LANE))

        def body(ci, cnt):
            kk = key_scr[:, pl.ds(chunk_start(ci), tk)]
            for t in range(tk // LANE):
                cnt = cnt + jnp.where(kk[:, t * LANE:(t + 1) * LANE] >= thr_b, 1.0, 0.0)
            return cnt

        cnt = lax.fori_loop(0, nk, body, jnp.zeros((tq, LANE), F32))
        return jnp.sum(cnt, axis=1, keepdims=True)

    def bisect(_, carry):
        lo, hi = carry
        mid = (lo >> 1) + (hi >> 1) + (lo & hi & 1)
        ge = count_ge(mid) >= topk
        return jnp.where(ge, mid, lo), jnp.where(ge, hi, mid)

    lo0 = jnp.full((tq, 1), INT_MIN + 1, I32)
    hi0 = jnp.full((tq, 1), INT_MAX, I32)
    thr, _ = lax.fori_loop(0, 32, bisect, (lo0, hi0))

    c_ge = count_ge(thr)
    c_gt = count_ge(thr + 1)
    thr_w = jnp.broadcast_to(thr, (tq, tk))

    @pl.when(jnp.max(c_ge) > topk)
    def _():
        allowed = topk - c_gt
        upper = jnp.where(lax.broadcasted_iota(I32, (tk, tk), 0) <= lax.broadcasted_iota(I32, (tk, tk), 1),
                          1.0, 0.0).astype(BF16)

        def body(ci, seen):
            k0 = chunk_start(ci)
            kk = key_scr[:, pl.ds(k0, tk)]
            eq = jnp.where(kk == thr_w, 1.0, 0.0)
            rank = seen + _dot(eq.astype(BF16), upper)
            drop = (eq > 0.0) & (rank > allowed)
            key_scr[:, pl.ds(k0, tk)] = jnp.where(drop, INT_MIN, kk)
            return seen + jnp.sum(eq, axis=1, keepdims=True)

        lax.fori_loop(0, nk, body, jnp.zeros((tq, 1), F32))

    for g in range(DSA_KV_HEADS):
        for r in range(grp):
            hcol = (g * grp + r) * DSA_HEAD_DIM
            q_scr[g, r * tq:(r + 1) * tq, :] = qn_ref[:, hcol:hcol + DSA_HEAD_DIM]
    m_scr[...] = jnp.full(m_scr.shape, MASKED_LOGIT, F32)
    l_scr[...] = jnp.zeros_like(l_scr)
    acc_scr[...] = jnp.zeros_like(acc_scr)

    def attend(ci, carry):
        k0 = chunk_start(ci)
        sel = jnp.where(key_scr[:, pl.ds(k0, tk)] >= thr_w, 1.0, 0.0)
        sel4 = jnp.concatenate([sel] * grp, axis=0)
        for g in range(DSA_KV_HEADS):
            hs = slice(g * DSA_HEAD_DIM, (g + 1) * DSA_HEAD_DIM)
            s = _dot_nt(q_scr[g], kn_ref[pl.ds(k0, tk), hs])
            s = jnp.where(sel4 > 0.0, s, MASKED_LOGIT)
            m_prev = m_scr[g]
            m_new = jnp.maximum(m_prev, jnp.max(s, axis=1, keepdims=True))
            p = jnp.exp(s - m_new) * sel4
            alpha = jnp.exp(m_prev - m_new)
            l_scr[g] = alpha * l_scr[g] + jnp.sum(p, axis=1, keepdims=True)
            acc_scr[g] = alpha * acc_scr[g] + _dot(p.astype(BF16), v_ref[pl.ds(k0, tk), hs])
            m_scr[g] = m_new
        return carry

    lax.fori_loop(0, nk, attend, 0)

    for g in range(DSA_KV_HEADS):
        out = acc_scr[g] / l_scr[g]
        for r in range(grp):
            hcol = (g * grp + r) * DSA_HEAD_DIM
            o_ref[:, hcol:hcol + DSA_HEAD_DIM] = out[r * tq:(r + 1) * tq, :].astype(o_ref.dtype)


def _dsa_call(qn, qi, wi, kn, vv, ki, batch, seq):
    n = qn.shape[0]
    tq = DSA_TQ
    nb = seq // tq
    topk = min(IDX_TOPK_MAX, seq // 4)
    grp = DSA_HEADS // DSA_KV_HEADS
    row = lambda b, i: (b * nb + i, 0)
    per_batch = lambda b, i: (b, 0)
    return pl.pallas_call(
        functools.partial(_dsa_body, topk),
        grid=(batch, nb),
        in_specs=[pl.BlockSpec((tq, qn.shape[1]), row),
                  pl.BlockSpec((tq, qi.shape[1]), row),
                  pl.BlockSpec((tq, LANE), row),
                  pl.BlockSpec((seq, kn.shape[1]), per_batch),
                  pl.BlockSpec((seq, vv.shape[1]), per_batch),
                  pl.BlockSpec((seq, LANE), per_batch)],
        out_specs=pl.BlockSpec((tq, DSA_HEADS * DSA_HEAD_DIM), row),
        out_shape=jax.ShapeDtypeStruct((n, DSA_HEADS * DSA_HEAD_DIM), BF16),
        scratch_shapes=[pltpu.VMEM((tq, seq), I32),
                        pltpu.VMEM((DSA_KV_HEADS, grp * tq, DSA_HEAD_DIM), BF16),
                        pltpu.VMEM((DSA_KV_HEADS, grp * tq, 1), F32),
                        pltpu.VMEM((DSA_KV_HEADS, grp * tq, 1), F32),
                        pltpu.VMEM((DSA_KV_HEADS, grp * tq, DSA_HEAD_DIM), F32)],
        compiler_params=_params(("parallel", "arbitrary"), 48),
        name="dsa",
    )(qn, qi, wi, kn, vv, ki)


def _mem_body(q_ref, k_ref, v_ref, o_ref):
    for h in range(MEM_HEADS):
        sl = slice(h * MEM_HEAD_DIM, (h + 1) * MEM_HEAD_DIM)
        s = _dot_nt(q_ref[:, sl], k_ref[:, sl])
        p = jnp.exp(s - jnp.max(s, axis=1, keepdims=True))
        o = _dot(p.astype(BF16), v_ref[:, sl]) / jnp.sum(p, axis=1, keepdims=True)
        o_ref[:, sl] = o.astype(o_ref.dtype)


def _mem_call(qm, kv, batch, seq, mem_len, tm):
    n, d = qm.shape
    nb = seq // tm
    return pl.pallas_call(
        _mem_body,
        grid=(batch, nb),
        in_specs=[pl.BlockSpec((tm, d), lambda b, i: (b * nb + i, 0)),
                  pl.BlockSpec((mem_len, d), lambda b, i: (b, 0)),
                  pl.BlockSpec((mem_len, d), lambda b, i: (b, 1))],
        out_specs=pl.BlockSpec((tm, d), lambda b, i: (b * nb + i, 0)),
        out_shape=jax.ShapeDtypeStruct((n, d), BF16),
        compiler_params=_params(("parallel", "parallel"), 32),
        name="mem_attn",
    )(qm, kv, kv)


def _merge_body(x_ref, og_ref, od_ref, om_ref, gate_ref, wg_ref, wd_ref, wm_ref, wo_ref, o_ref):
    d = D_MODEL
    merged = (gate_ref[:, 0:d] * _dot(og_ref[...], wg_ref[...])
              + gate_ref[:, d:2 * d] * _dot(od_ref[...], wd_ref[...])
              + gate_ref[:, 2 * d:3 * d] * _dot(om_ref[...], wm_ref[...]))
    o_ref[...] = x_ref[...] + _dot(merged.astype(BF16), wo_ref[...])


def _merge_call(x2d, o_gla, o_dsa, o_mem, gate, wg, wd, wm, wo, tm):
    n, d = x2d.shape
    row = lambda i: (i, 0)
    fixed = lambda i: (0, 0)
    return pl.pallas_call(
        _merge_body,
        grid=(n // tm,),
        in_specs=[pl.BlockSpec((tm, d), row), pl.BlockSpec((tm, d), row), pl.BlockSpec((tm, d), row),
                  pl.BlockSpec((tm, d), row), pl.BlockSpec((tm, 3 * d), row),
                  pl.BlockSpec((d, d), fixed), pl.BlockSpec((d, d), fixed),
                  pl.BlockSpec((d, d), fixed), pl.BlockSpec((d, d), fixed)],
        out_specs=pl.BlockSpec((tm, d), row),
        out_shape=jax.ShapeDtypeStruct((n, d), F32),
        compiler_params=_params(("parallel",), 48),
        name="merge",
    )(x2d, o_gla, o_dsa, o_mem, gate, wg, wd, wm, wo)


def _route(z):
    lane = lax.broadcasted_iota(I32, z.shape, 1)
    lanef = lane.astype(F32)
    far = 1e9
    neg = -jnp.inf
    gmask = lane < N_GROUPS
    gmax = jnp.max(jnp.where(gmask, z, neg), axis=1, keepdims=True)
    gsum = jnp.sum(jnp.where(gmask, jnp.exp(z - gmax), 0.0), axis=1, keepdims=True)
    g_w = 1.0 / gsum
    g_sel = jnp.min(jnp.where(gmask & (z == gmax), lanef, far), axis=1, keepdims=True)
    lane_grp = ((lane - N_GROUPS) >> 2).astype(F32)
    emask = (lane >= N_GROUPS) & (lane < N_GROUPS + N_EXPERTS) & (lane_grp == g_sel)
    e1 = jnp.max(jnp.where(emask, z, neg), axis=1, keepdims=True)
    i1 = jnp.min(jnp.where(emask & (z == e1), lanef, far), axis=1, keepdims=True)
    emask2 = emask & (lanef != i1)
    e2 = jnp.max(jnp.where(emask2, z, neg), axis=1, keepdims=True)
    i2 = jnp.min(jnp.where(emask2 & (z == e2), lanef, far), axis=1, keepdims=True)
    t = jnp.exp(e2 - e1)
    p1 = 1.0 / (1.0 + t)
    p2 = t / (1.0 + t)
    return jnp.where(lanef == i1, g_w * p1, jnp.where(lanef == i2, g_w * p2, 0.0))


def _moe_body(x_ref, g_ref, wrh_ref, wrl_ref, br_ref, wg_ref, wu_ref, wd_ref, o_ref, h_scr, comb_scr):
    e = pl.program_id(1)

    @pl.when(e == 0)
    def _():
        x = x_ref[...]
        h = _rms_rows(x) * g_ref[...]
        hh = h.astype(BF16)
        hl = (h - hh.astype(F32)).astype(BF16)
        h_scr[...] = hh
        z = _dot(hh, wrh_ref[...]) + _dot(hl, wrh_ref[...]) + _dot(hh, wrl_ref[...]) + br_ref[...]
        comb_scr[...] = _route(z)
        o_ref[...] = x

    lane = lax.broadcasted_iota(I32, comb_scr.shape, 1)
    ce = jnp.sum(jnp.where(lane == e + N_GROUPS, comb_scr[...], 0.0), axis=1, keepdims=True)
    h = h_scr[...]
    gt = _dot(h, wg_ref[0])
    up = _dot(h, wu_ref[0])
    hid = gt * _sigmoid(gt) * up * ce
    o_ref[...] += _dot(hid.astype(BF16), wd_ref[0])


def _moe_call(x1, g_ffn, wr_hi, wr_lo, br, wg, wu, wd, tm):
    n, d = x1.shape
    row = lambda i, e: (i, 0)
    fixed = lambda i, e: (0, 0)
    per_e = lambda i, e: (e, 0, 0)
    return pl.pallas_call(
        _moe_body,
        grid=(n // tm, N_EXPERTS),
        in_specs=[pl.BlockSpec((tm, d), row), pl.BlockSpec((1, d), fixed),
                  pl.BlockSpec((d, LANE), fixed), pl.BlockSpec((d, LANE), fixed),
                  pl.BlockSpec((1, LANE), fixed),
                  pl.BlockSpec((1, d, D_EXPERT), per_e), pl.BlockSpec((1, d, D_EXPERT), per_e),
                  pl.BlockSpec((1, D_EXPERT, d), per_e)],
        out_specs=pl.BlockSpec((tm, d), row),
        out_shape=jax.ShapeDtypeStruct((n, d), F32),
        scratch_shapes=[pltpu.VMEM((tm, d), BF16), pltpu.VMEM((tm, LANE), F32)],
        compiler_params=_params(("parallel", "arbitrary"), 48),
        name="moe",
    )(x1, g_ffn.reshape(1, d), wr_hi, wr_lo, br, wg, wu, wd)


def _layer(x2d, mem2d, batch, seq, mem_len, g_mix, g_mem, w_in, w_gla_a2, b_gla_a2, gla_norm,
           w_mem_kv, dsa_q_norm, dsa_k_norm, idx_k_norm, mem_q_norm, mem_k_norm, b_gate,
           w_o_gla, w_o_dsa, w_o_mem, w_out, g_ffn, w_r1, b_r1, w_r2, b_r2, w_gate, w_up, w_down):
    n, d = x2d.shape
    sds = jax.ShapeDtypeStruct

    w_gla = jnp.concatenate([_cols(w_in, "gq"), _cols(w_in, "gk"), _cols(w_in, "gv"), _cols(w_in, "gg"),
                             _pad_cols(_cols(w_in, "ga"), LANE)], axis=1).astype(BF16)
    (zg,) = _proj_call("proj_gla", x2d, g_mix, w_gla, _epi_plain, [],
                       [sds((n, w_gla.shape[1]), F32)], [640], 1024, 640, 40)

    (qn,) = _proj_call("proj_dsa_q", x2d, g_mix, _cols(w_in, "dq").astype(BF16),
                       functools.partial(_epi_headnorm, DSA_HEAD_DIM, DSA_HEAD_DIM ** -0.5),
                       [dsa_q_norm.reshape(1, -1)], [sds((n, 1024), BF16)], [1024], 1024, 1024, 40)
    (qm,) = _proj_call("proj_mem_q", x2d, g_mix, _cols(w_in, "mq").astype(BF16),
                       functools.partial(_epi_headnorm, MEM_HEAD_DIM, MEM_HEAD_DIM ** -0.5),
                       [mem_q_norm.reshape(1, -1)], [sds((n, 1024), BF16)], [1024], 1024, 1024, 40)

    w_kv = jnp.concatenate([_cols(w_in, "dk"), _cols(w_in, "dv"), _pad_cols(_cols(w_in, "ik"), LANE),
                            _pad_cols(_cols(w_in, "iw"), LANE)], axis=1).astype(BF16)
    kn, vv, ki, wi = _proj_call(
        "proj_dsa_kv", x2d, g_mix, w_kv, _epi_dsa_kv,
        [dsa_k_norm.reshape(1, -1), _pad_cols(idx_k_norm.reshape(1, -1), LANE)],
        [sds((n, 256), BF16), sds((n, 256), BF16), sds((n, LANE), BF16), sds((n, LANE), F32)],
        [256, 256, LANE, LANE], 1024, 768, 40)

    w_iq = _cols(w_in, "iq").reshape(d, IDX_HEADS, IDX_DIM)
    w_iq = jnp.pad(w_iq, ((0, 0), (0, 0), (0, LANE - IDX_DIM))).reshape(d, IDX_HEADS * LANE).astype(BF16)
    (qi,) = _proj_call("proj_idx_q", x2d, g_mix, w_iq, _epi_plain, [],
                       [sds((n, IDX_HEADS * LANE), BF16)], [1024], 1024, 1024, 40)

    (gate,) = _proj_call("proj_gate", x2d, g_mix, _cols(w_in, "gates").astype(BF16), _epi_gate,
                         [b_gate.reshape(1, -1)], [sds((n, 3 * d), F32)], [1024], 1024, 1024, 40)
    wa = jnp.pad(w_gla_a2, ((0, LANE - GLA_GATE_RANK), (0, 0))).astype(BF16)
    o_gla = _gla_call(zg, wa, b_gla_a2.reshape(1, -1), gla_norm.reshape(1, -1), batch, seq, 512)

    o_dsa = _dsa_call(qn, qi, wi, kn, vv, ki, batch, seq)

    (kv,) = _proj_call("proj_mem_kv", mem2d, g_mem, w_mem_kv.astype(BF16), _epi_mem_kv,
                       [mem_k_norm.reshape(1, -1)], [sds((mem2d.shape[0], 2 * d), BF16)], [1024],
                       mem2d.shape[0], 1024, 40)
    o_mem = _mem_call(qm, kv, batch, seq, mem_len, 512)

    x1 = _merge_call(x2d, o_gla, o_dsa, o_mem, gate, w_o_gla.astype(BF16), w_o_dsa.astype(BF16),
                     w_o_mem.astype(BF16), w_out.astype(BF16), 256)

    w_r = _pad_cols(jnp.concatenate([w_r1, w_r2], axis=1), LANE)
    wr_hi = w_r.astype(BF16)
    wr_lo = (w_r - wr_hi.astype(F32)).astype(BF16)
    b_r = _pad_cols(jnp.concatenate([b_r1, b_r2]).reshape(1, -1), LANE)
    return _moe_call(x1, g_ffn, wr_hi, wr_lo, b_r, w_gate.astype(BF16), w_up.astype(BF16),
                     w_down.astype(BF16), 512)


def kernel(x, mem, g_mix, g_mem, w_in, w_gla_a2, b_gla_a2, gla_norm, w_mem_kv, dsa_q_norm, dsa_k_norm, idx_k_norm, mem_q_norm, mem_k_norm, b_gate, w_o_gla, w_o_dsa, w_o_mem, w_out, g_ffn, w_r1, b_r1, w_r2, b_r2, w_gate, w_up, w_down):
    batch, seq, d = x.shape
    mem_len = mem.shape[1]
    x2d = x.reshape(batch * seq, d)
    mem2d = mem.reshape(batch * mem_len, d)
    params = (g_mix, g_mem, w_in, w_gla_a2, b_gla_a2, gla_norm, w_mem_kv, dsa_q_norm, dsa_k_norm,
              idx_k_norm, mem_q_norm, mem_k_norm, b_gate, w_o_gla, w_o_dsa, w_o_mem, w_out, g_ffn,
              w_r1, b_r1, w_r2, b_r2, w_gate, w_up, w_down)
    for layer in range(g_mix.shape[0]):
        x2d = _layer(x2d, mem2d, batch, seq, mem_len, *(p[layer] for p in params))
    return x2d.reshape(batch, seq, d)
```

```python
import functools

import jax
import jax.numpy as jnp
from jax import lax
from jax.experimental import pallas as pl
from jax.experimental.pallas import tpu as pltpu

F32 = jnp.float32
BF16 = jnp.bfloat16
I32 = jnp.int32

D_MODEL = 1024
EPS = 1e-6

GLA_HEADS = 4
GLA_DK = 128
GLA_DV = 256
GLA_GATE_RANK = 16
GLA_GATE_TEMP = 16.0
GLA_CHUNK = 64

DSA_HEADS = 8
DSA_KV_HEADS = 2
DSA_HEAD_DIM = 128
IDX_HEADS = 8
IDX_DIM = 64
IDX_TOPK_MAX = 256

MEM_HEADS = 4
MEM_HEAD_DIM = 256

N_GROUPS = 4
EXPERTS_PER_GROUP = 4
N_EXPERTS = 16
D_EXPERT = 256

LANE = 128
INT_MIN = -(2 ** 31)
INT_MAX = 2 ** 31 - 1
MASKED_LOGIT = -1e30
LOG2_E = 1.4426950408889634

_OFF = {}
_acc = 0
for _name, _w in (("gq", 512), ("gk", 512), ("gv", 1024), ("gg", 1024), ("ga", 16), ("dq", 1024),
                  ("dk", 256), ("dv", 256), ("iq", 512), ("ik", 64), ("iw", 8), ("mq", 1024),
                  ("gates", 3072)):
    _OFF[_name] = (_acc, _acc + _w)
    _acc += _w


def _cols(w_in, name):
    lo, hi = _OFF[name]
    return w_in[:, lo:hi]


def _pad_cols(w, width):
    return jnp.pad(w, ((0, 0), (0, width - w.shape[1])))


def _rms_rows(x):
    return x * lax.rsqrt(jnp.mean(x * x, axis=-1, keepdims=True) + EPS)


def _dot(a, b):
    return jnp.dot(a, b, preferred_element_type=F32)


def _dot_nt(a, b):
    return lax.dot_general(a, b, (((1,), (1,)), ((), ())), preferred_element_type=F32)


def _dot_tn(a, b):
    return lax.dot_general(a, b, (((0,), (0,)), ((), ())), preferred_element_type=F32)


def _sigmoid(x):
    return 1.0 / (1.0 + jnp.exp(-x))


def _params(sem, vmem_mb):
    return pltpu.CompilerParams(dimension_semantics=sem, vmem_limit_bytes=vmem_mb * 1024 * 1024)


def _proj_body(epilogue, n_extra, x_ref, g_ref, w_ref, *rest):
    extra = rest[:n_extra]
    outs = rest[n_extra:-1]
    h_scr = rest[-1]

    @pl.when(pl.program_id(1) == 0)
    def _():
        h_scr[...] = (_rms_rows(x_ref[...]) * g_ref[...]).astype(BF16)

    epilogue(_dot(h_scr[...], w_ref[...]), extra, outs)


def _proj_call(name, x2d, gain, w, epilogue, extras, out_shapes, out_widths, tm, tn, vmem_mb):
    n, d = x2d.shape
    c = w.shape[1]
    in_specs = [pl.BlockSpec((tm, d), lambda i, j: (i, 0)),
                pl.BlockSpec((1, d), lambda i, j: (0, 0)),
                pl.BlockSpec((d, tn), lambda i, j: (0, j))]
    single = c == tn
    for e in extras:
        if e.shape[1] == c and not single:
            in_specs.append(pl.BlockSpec((1, tn), lambda i, j: (0, j)))
        else:
            in_specs.append(pl.BlockSpec(e.shape, lambda i, j: (0, 0)))
    out_specs = [pl.BlockSpec((tm, ow), (lambda i, j: (i, 0)) if single else (lambda i, j: (i, j)))
                 for ow in out_widths]
    return pl.pallas_call(
        functools.partial(_proj_body, epilogue, len(extras)),
        grid=(n // tm, c // tn),
        in_specs=in_specs,
        out_specs=out_specs,
        out_shape=out_shapes,
        scratch_shapes=[pltpu.VMEM((tm, d), BF16)],
        compiler_params=_params(("parallel", "arbitrary"), vmem_mb),
        name=name,
    )(x2d, gain.reshape(1, d), w, *extras)


def _epi_plain(acc, extra, outs):
    outs[0][...] = acc.astype(outs[0].dtype)


def _epi_headnorm(head_dim, scale, acc, extra, outs):
    gain = extra[0][...]
    for h in range(acc.shape[1] // head_dim):
        sl = slice(h * head_dim, (h + 1) * head_dim)
        outs[0][:, sl] = (_rms_rows(acc[:, sl]) * gain * scale).astype(outs[0].dtype)


def _epi_dsa_kv(acc, extra, outs):
    k_gain, idx_gain = extra
    kn_ref, v_ref, ki_ref, wi_ref = outs
    for h in range(DSA_KV_HEADS):
        sl = slice(h * DSA_HEAD_DIM, (h + 1) * DSA_HEAD_DIM)
        kn_ref[:, sl] = (_rms_rows(acc[:, sl]) * k_gain[...]).astype(BF16)
    v_ref[...] = acc[:, 256:512].astype(BF16)
    a = acc[:, 512:640]
    ms = jnp.sum(a * a, axis=-1, keepdims=True) * (1.0 / IDX_DIM)
    ki_ref[...] = (a * lax.rsqrt(ms + EPS) * idx_gain[...]).astype(BF16)
    wi_ref[...] = acc[:, 640:768] * (IDX_HEADS ** -0.5 * IDX_DIM ** -0.5)


def _epi_gate(acc, extra, outs):
    outs[0][...] = _sigmoid(acc + extra[0][...])


def _epi_mem_kv(acc, extra, outs):
    gain = extra[0][...]
    j = pl.program_id(1)

    @pl.when(j == 0)
    def _():
        for h in range(MEM_HEADS):
            sl = slice(h * MEM_HEAD_DIM, (h + 1) * MEM_HEAD_DIM)
            outs[0][:, sl] = (_rms_rows(acc[:, sl]) * gain).astype(BF16)

    @pl.when(j == 1)
    def _():
        outs[0][...] = acc.astype(BF16)


def _split3(x):
    hi = x.astype(BF16)
    r1 = x - hi.astype(F32)
    mid = r1.astype(BF16)
    lo = (r1 - mid.astype(F32)).astype(BF16)
    return hi, mid, lo


def _gla_body(q_ref, k_ref, v_ref, g_ref, a_ref, wa_ref, ba_ref, ng_ref, o_ref,
              st_scr, qd_scr, ki_scr, ke_scr, dec_scr):
    tb = q_ref.shape[0]
    c = GLA_CHUNK

    @pl.when(pl.program_id(1) == 0)
    def _():
        st_scr[...] = jnp.zeros_like(st_scr)

    pre = _dot(a_ref[...].astype(BF16), wa_ref[...]) + ba_ref[...]
    log_a = (jnp.minimum(pre, 0.0) - jnp.log1p(jnp.exp(-jnp.abs(pre)))) * (1.0 / GLA_GATE_TEMP)

    row = lax.broadcasted_iota(I32, (tb, tb), 0)
    col = lax.broadcasted_iota(I32, (tb, tb), 1)
    shift = c.bit_length() - 1
    same_chunk = (row >> shift) == (col >> shift)
    chunk_ones = jnp.where(same_chunk, 1.0, 0.0)
    tri = jnp.where(col <= row, chunk_ones, 0.0).astype(BF16)
    blk = chunk_ones.astype(BF16)
    parts = _split3(log_a)
    cum = _dot(tri, parts[0]) + _dot(tri, parts[1]) + _dot(tri, parts[2])
    tot = _dot(blk, parts[0]) + _dot(blk, parts[1]) + _dot(blk, parts[2])

    kf = k_ref[...]
    qd_scr[...] = (q_ref[...] * (GLA_DK ** -0.5) * jnp.exp(cum)).astype(BF16)
    ki_scr[...] = (kf * jnp.exp(-cum)).astype(BF16)
    ke_scr[...] = (kf * jnp.exp(tot - cum)).astype(BF16)
    dec_scr[...] = jnp.exp(tot)

    causal = lax.broadcasted_iota(I32, (c, c), 1) <= lax.broadcasted_iota(I32, (c, c), 0)

    def chunk(ci, carry):
        r0 = pl.multiple_of(ci * c, c)
        rows = pl.ds(r0, c)
        for h in range(GLA_HEADS):
            ks = slice(h * GLA_DK, (h + 1) * GLA_DK)
            vs = slice(h * GLA_DV, (h + 1) * GLA_DV)
            qd = qd_scr[rows, ks]
            vv = v_ref[rows, vs].astype(BF16)
            att = jnp.where(causal, _dot_nt(qd, ki_scr[rows, ks]), 0.0).astype(BF16)
            st = st_scr[h]
            o = _dot(att, vv) + _dot_nt(qd, st.astype(BF16))
            st_scr[h] = st * dec_scr[pl.ds(r0, 1), ks] + _dot_tn(vv, ke_scr[rows, ks])
            gg = g_ref[rows, vs]
            o_ref[rows, vs] = (_rms_rows(o) * ng_ref[...] * (gg * _sigmoid(gg))).astype(o_ref.dtype)
        return carry

    lax.fori_loop(0, tb // c, chunk, 0)


def _gla_call(zg, wa, ba, ng, batch, seq, tb):
    n = zg.shape[0]
    nb = seq // tb
    row = lambda b, i: b * nb + i
    in_specs = [
        pl.BlockSpec((tb, 512), lambda b, i: (row(b, i), 0)),
        pl.BlockSpec((tb, 512), lambda b, i: (row(b, i), 1)),
        pl.BlockSpec((tb, 1024), lambda b, i: (row(b, i), 1)),
        pl.BlockSpec((tb, 1024), lambda b, i: (row(b, i), 2)),
        pl.BlockSpec((tb, LANE), lambda b, i: (row(b, i), 3072 // LANE)),
        pl.BlockSpec(wa.shape, lambda b, i: (0, 0)),
        pl.BlockSpec(ba.shape, lambda b, i: (0, 0)),
        pl.BlockSpec(ng.shape, lambda b, i: (0, 0)),
    ]
    hk = GLA_HEADS * GLA_DK
    return pl.pallas_call(
        _gla_body,
        grid=(batch, nb),
        in_specs=in_specs,
        out_specs=pl.BlockSpec((tb, GLA_HEADS * GLA_DV), lambda b, i: (row(b, i), 0)),
        out_shape=jax.ShapeDtypeStruct((n, GLA_HEADS * GLA_DV), BF16),
        scratch_shapes=[pltpu.VMEM((GLA_HEADS, GLA_DV, GLA_DK), F32),
                        pltpu.VMEM((tb, hk), BF16), pltpu.VMEM((tb, hk), BF16),
                        pltpu.VMEM((tb, hk), BF16), pltpu.VMEM((tb, hk), F32)],
        compiler_params=_params(("parallel", "arbitrary"), 48),
        name="gla",
    )(zg, zg, zg, zg, zg, wa, ba, ng)


DSA_TQ = 128
DSA_TK = 512


def _dsa_body(topk, qn_ref, qi_ref, wi_ref, kn_ref, v_ref, ki_ref, o_ref,
              key_scr, q_scr, m_scr, l_scr, acc_scr):
    tq, tk = DSA_TQ, DSA_TK
    grp = DSA_HEADS // DSA_KV_HEADS
    q0 = pl.program_id(1) * tq
    nk = (q0 + tq + tk - 1) // tk
    qpos = q0 + lax.broadcasted_iota(I32, (tq, tk), 0)
    kofs = lax.broadcasted_iota(I32, (tq, tk), 1)

    def chunk_start(ci):
        return pl.multiple_of(ci * tk, tk)

    wi = wi_ref[...]

    def score_chunk(ci, carry):
        k0 = chunk_start(ci)
        kc = ki_ref[pl.ds(k0, tk), :]
        score = jnp.zeros((tq, tk), F32)
        for h in range(IDX_HEADS):
            qh = qi_ref[:, h * LANE:(h + 1) * LANE]
            score = score + wi[:, h:h + 1] * jnp.maximum(_dot_nt(qh, kc), 0.0)
        score = jnp.where(score == 0.0, 0.0, score)
        bits = lax.bitcast_convert_type(score, I32)
        key = bits ^ ((bits >> 31) & INT_MAX)
        key_scr[:, pl.ds(k0, tk)] = jnp.where(k0 + kofs <= qpos, key, INT_MIN)
        return carry

    lax.fori_loop(0, nk, score_chunk, 0)

    def count_ge(thr):
        thr_b = jnp.broadcast_to(thr, (tq, LANE))

        def body(ci, cnt):
            kk = key_scr[:, pl.ds(chunk_start(ci), tk)]
            for t in range(tk // LANE):
                cnt = cnt + jnp.where(kk[:, t * LANE:(t + 1) * LANE] >= thr_b, 1.0, 0.0)
            return cnt

        cnt = lax.fori_loop(0, nk, body, jnp.zeros((tq, LANE), F32))
        return jnp.sum(cnt, axis=1, keepdims=True)

    def unfinished(lo, hi):
        return jnp.where((lo == hi) | (lo + 1 == hi), 0.0, 1.0)

    def bisect_cond(carry):
        it, lo, hi, _ = carry
        return (it < 32) & (jnp.max(unfinished(lo, hi)) > 0.0)

    def bisect(carry):
        it, lo, hi, c_lo = carry
        mid = (lo >> 1) + (hi >> 1) + (lo & hi & 1)
        cnt = count_ge(mid)
        ge = cnt >= topk
        hi = jnp.where(ge, jnp.where(cnt == topk, mid, hi), mid)
        return it + 1, jnp.where(ge, mid, lo), hi, jnp.where(ge, cnt, c_lo)

    lo0 = jnp.full((tq, 1), INT_MIN + 1, I32)
    hi0 = jnp.full((tq, 1), INT_MAX, I32)
    n_causal = (q0 + 1 + lax.broadcasted_iota(I32, (tq, 1), 0)).astype(F32)
    _, thr, _, c_ge = lax.while_loop(bisect_cond, bisect, (jnp.int32(0), lo0, hi0, n_causal))
    thr_w = jnp.broadcast_to(thr, (tq, tk))

    @pl.when(jnp.max(c_ge) > topk)
    def _():
        allowed = topk - count_ge(thr + 1)
        upper = jnp.where(lax.broadcasted_iota(I32, (tk, tk), 0) <= lax.broadcasted_iota(I32, (tk, tk), 1),
                          1.0, 0.0).astype(BF16)

        def body(ci, seen):
            k0 = chunk_start(ci)
            kk = key_scr[:, pl.ds(k0, tk)]
            eq = jnp.where(kk == thr_w, 1.0, 0.0)
            rank = seen + _dot(eq.astype(BF16), upper)
            drop = (eq > 0.0) & (rank > allowed)
            key_scr[:, pl.ds(k0, tk)] = jnp.where(drop, INT_MIN, kk)
            return seen + jnp.sum(eq, axis=1, keepdims=True)

        lax.fori_loop(0, nk, body, jnp.zeros((tq, 1), F32))

    for g in range(DSA_KV_HEADS):
        for r in range(grp):
            hcol = (g * grp + r) * DSA_HEAD_DIM
            q_scr[g, r * tq:(r + 1) * tq, :] = qn_ref[:, hcol:hcol + DSA_HEAD_DIM]
    m_scr[...] = jnp.full(m_scr.shape, MASKED_LOGIT, F32)
    l_scr[...] = jnp.zeros_like(l_scr)
    acc_scr[...] = jnp.zeros_like(acc_scr)

    def attend(ci, carry):
        k0 = chunk_start(ci)
        bias = jnp.where(key_scr[:, pl.ds(k0, tk)] >= thr_w, 0.0, -jnp.inf)
        for g in range(DSA_KV_HEADS):
            hs = slice(g * DSA_HEAD_DIM, (g + 1) * DSA_HEAD_DIM)
            s = _dot_nt(q_scr[g], kn_ref[pl.ds(k0, tk), hs])
            s = (s.reshape(grp, tq, tk) + bias[None]).reshape(grp * tq, tk)
            m_prev = m_scr[g]
            m_new = jnp.maximum(m_prev, jnp.max(s, axis=1, keepdims=True))
            p = jnp.exp2(s - pltpu.repeat(m_new, tk // LANE, axis=1))
            alpha = jnp.exp2(m_prev - m_new)
            l_scr[g] = alpha * l_scr[g] + jnp.sum(p, axis=1, keepdims=True)
            acc_scr[g] = alpha * acc_scr[g] + _dot(p.astype(BF16), v_ref[pl.ds(k0, tk), hs])
            m_scr[g] = m_new
        return carry

    lax.fori_loop(0, nk, attend, 0)

    for g in range(DSA_KV_HEADS):
        out = acc_scr[g] / l_scr[g]
        for r in range(grp):
            hcol = (g * grp + r) * DSA_HEAD_DIM
            o_ref[:, hcol:hcol + DSA_HEAD_DIM] = out[r * tq:(r + 1) * tq, :].astype(o_ref.dtype)


def _dsa_call(qn, qi, wi, kn, vv, ki, batch, seq):
    n = qn.shape[0]
    tq = DSA_TQ
    nb = seq // tq
    topk = min(IDX_TOPK_MAX, seq // 4)
    grp = DSA_HEADS // DSA_KV_HEADS
    row = lambda b, i: (b * nb + i, 0)
    per_batch = lambda b, i: (b, 0)
    return pl.pallas_call(
        functools.partial(_dsa_body, topk),
        grid=(batch, nb),
        in_specs=[pl.BlockSpec((tq, qn.shape[1]), row),
                  pl.BlockSpec((tq, qi.shape[1]), row),
                  pl.BlockSpec((tq, LANE), row),
                  pl.BlockSpec((seq, kn.shape[1]), per_batch),
                  pl.BlockSpec((seq, vv.shape[1]), per_batch),
                  pl.BlockSpec((seq, LANE), per_batch)],
        out_specs=pl.BlockSpec((tq, DSA_HEADS * DSA_HEAD_DIM), row),
        out_shape=jax.ShapeDtypeStruct((n, DSA_HEADS * DSA_HEAD_DIM), BF16),
        scratch_shapes=[pltpu.VMEM((tq, seq), I32),
                        pltpu.VMEM((DSA_KV_HEADS, grp * tq, DSA_HEAD_DIM), BF16),
                        pltpu.VMEM((DSA_KV_HEADS, grp * tq, LANE), F32),
                        pltpu.VMEM((DSA_KV_HEADS, grp * tq, LANE), F32),
                        pltpu.VMEM((DSA_KV_HEADS, grp * tq, DSA_HEAD_DIM), F32)],
        compiler_params=_params(("parallel", "arbitrary"), 48),
        name="dsa",
    )(qn, qi, wi, kn, vv, ki)


def _mem_body(q_ref, k_ref, v_ref, o_ref):
    for h in range(MEM_HEADS):
        sl = slice(h * MEM_HEAD_DIM, (h + 1) * MEM_HEAD_DIM)
        s = _dot_nt(q_ref[:, sl], k_ref[:, sl])
        p = jnp.exp(s - jnp.max(s, axis=1, keepdims=True))
        o = _dot(p.astype(BF16), v_ref[:, sl]) / jnp.sum(p, axis=1, keepdims=True)
        o_ref[:, sl] = o.astype(o_ref.dtype)


def _mem_call(qm, kv, batch, seq, mem_len, tm):
    n, d = qm.shape
    nb = seq // tm
    return pl.pallas_call(
        _mem_body,
        grid=(batch, nb),
        in_specs=[pl.BlockSpec((tm, d), lambda b, i: (b * nb + i, 0)),
                  pl.BlockSpec((mem_len, d), lambda b, i: (b, 0)),
                  pl.BlockSpec((mem_len, d), lambda b, i: (b, 1))],
        out_specs=pl.BlockSpec((tm, d), lambda b, i: (b * nb + i, 0)),
        out_shape=jax.ShapeDtypeStruct((n, d), BF16),
        compiler_params=_params(("parallel", "parallel"), 32),
        name="mem_attn",
    )(qm, kv, kv)


def _merge_body(x_ref, og_ref, od_ref, om_ref, gate_ref, wg_ref, wd_ref, wm_ref, wo_ref, o_ref):
    d = D_MODEL
    merged = (gate_ref[:, 0:d] * _dot(og_ref[...], wg_ref[...])
              + gate_ref[:, d:2 * d] * _dot(od_ref[...], wd_ref[...])
              + gate_ref[:, 2 * d:3 * d] * _dot(om_ref[...], wm_ref[...]))
    o_ref[...] = x_ref[...] + _dot(merged.astype(BF16), wo_ref[...])


def _merge_call(x2d, o_gla, o_dsa, o_mem, gate, wg, wd, wm, wo, tm):
    n, d = x2d.shape
    row = lambda i: (i, 0)
    fixed = lambda i: (0, 0)
    return pl.pallas_call(
        _merge_body,
        grid=(n // tm,),
        in_specs=[pl.BlockSpec((tm, d), row), pl.BlockSpec((tm, d), row), pl.BlockSpec((tm, d), row),
                  pl.BlockSpec((tm, d), row), pl.BlockSpec((tm, 3 * d), row),
                  pl.BlockSpec((d, d), fixed), pl.BlockSpec((d, d), fixed),
                  pl.BlockSpec((d, d), fixed), pl.BlockSpec((d, d), fixed)],
        out_specs=pl.BlockSpec((tm, d), row),
        out_shape=jax.ShapeDtypeStruct((n, d), F32),
        compiler_params=_params(("parallel",), 48),
        name="merge",
    )(x2d, o_gla, o_dsa, o_mem, gate, wg, wd, wm, wo)


def _route(z):
    lane = lax.broadcasted_iota(I32, z.shape, 1)
    lanef = lane.astype(F32)
    far = 1e9
    neg = -jnp.inf
    gmask = lane < N_GROUPS
    gmax = jnp.max(jnp.where(gmask, z, neg), axis=1, keepdims=True)
    gsum = jnp.sum(jnp.where(gmask, jnp.exp(z - gmax), 0.0), axis=1, keepdims=True)
    g_w = 1.0 / gsum
    g_sel = jnp.min(jnp.where(gmask & (z == gmax), lanef, far), axis=1, keepdims=True)
    lane_grp = ((lane - N_GROUPS) >> 2).astype(F32)
    emask = (lane >= N_GROUPS) & (lane < N_GROUPS + N_EXPERTS) & (lane_grp == g_sel)
    e1 = jnp.max(jnp.where(emask, z, neg), axis=1, keepdims=True)
    i1 = jnp.min(jnp.where(emask & (z == e1), lanef, far), axis=1, keepdims=True)
    emask2 = emask & (lanef != i1)
    e2 = jnp.max(jnp.where(emask2, z, neg), axis=1, keepdims=True)
    i2 = jnp.min(jnp.where(emask2 & (z == e2), lanef, far), axis=1, keepdims=True)
    t = jnp.exp(e2 - e1)
    p1 = 1.0 / (1.0 + t)
    p2 = t / (1.0 + t)
    return jnp.where(lanef == i1, g_w * p1, jnp.where(lanef == i2, g_w * p2, 0.0))


def _moe_body(x_ref, g_ref, wrh_ref, wrl_ref, br_ref, wg_ref, wu_ref, wd_ref, o_ref, h_scr, comb_scr):
    e = pl.program_id(1)

    @pl.when(e == 0)
    def _():
        x = x_ref[...]
        h = _rms_rows(x) * g_ref[...]
        hh = h.astype(BF16)
        hl = (h - hh.astype(F32)).astype(BF16)
        h_scr[...] = hh
        z = _dot(hh, wrh_ref[...]) + _dot(hl, wrh_ref[...]) + _dot(hh, wrl_ref[...]) + br_ref[...]
        comb_scr[...] = _route(z)
        o_ref[...] = x

    lane = lax.broadcasted_iota(I32, comb_scr.shape, 1)
    ce = jnp.sum(jnp.where(lane == e + N_GROUPS, comb_scr[...], 0.0), axis=1, keepdims=True)
    h = h_scr[...]
    gt = _dot(h, wg_ref[0])
    up = _dot(h, wu_ref[0])
    hid = gt * _sigmoid(gt) * up * ce
    o_ref[...] += _dot(hid.astype(BF16), wd_ref[0])


def _moe_call(x1, g_ffn, wr_hi, wr_lo, br, wg, wu, wd, tm):
    n, d = x1.shape
    row = lambda i, e: (i, 0)
    fixed = lambda i, e: (0, 0)
    per_e = lambda i, e: (e, 0, 0)
    return pl.pallas_call(
        _moe_body,
        grid=(n // tm, N_EXPERTS),
        in_specs=[pl.BlockSpec((tm, d), row), pl.BlockSpec((1, d), fixed),
                  pl.BlockSpec((d, LANE), fixed), pl.BlockSpec((d, LANE), fixed),
                  pl.BlockSpec((1, LANE), fixed),
                  pl.BlockSpec((1, d, D_EXPERT), per_e), pl.BlockSpec((1, d, D_EXPERT), per_e),
                  pl.BlockSpec((1, D_EXPERT, d), per_e)],
        out_specs=pl.BlockSpec((tm, d), row),
        out_shape=jax.ShapeDtypeStruct((n, d), F32),
        scratch_shapes=[pltpu.VMEM((tm, d), BF16), pltpu.VMEM((tm, LANE), F32)],
        compiler_params=_params(("parallel", "arbitrary"), 48),
        name="moe",
    )(x1, g_ffn.reshape(1, d), wr_hi, wr_lo, br, wg, wu, wd)


def _layer(x2d, mem2d, batch, seq, mem_len, g_mix, g_mem, w_in, w_gla_a2, b_gla_a2, gla_norm,
           w_mem_kv, dsa_q_norm, dsa_k_norm, idx_k_norm, mem_q_norm, mem_k_norm, b_gate,
           w_o_gla, w_o_dsa, w_o_mem, w_out, g_ffn, w_r1, b_r1, w_r2, b_r2, w_gate, w_up, w_down):
    n, d = x2d.shape
    sds = jax.ShapeDtypeStruct

    w_gla = jnp.concatenate([_cols(w_in, "gq"), _cols(w_in, "gk"), _cols(w_in, "gv"), _cols(w_in, "gg"),
                             _pad_cols(_cols(w_in, "ga"), LANE)], axis=1).astype(BF16)
    (zg,) = _proj_call("proj_gla", x2d, g_mix, w_gla, _epi_plain, [],
                       [sds((n, w_gla.shape[1]), F32)], [640], 1024, 640, 40)

    (qn,) = _proj_call("proj_dsa_q", x2d, g_mix, _cols(w_in, "dq").astype(BF16),
                       functools.partial(_epi_headnorm, DSA_HEAD_DIM, DSA_HEAD_DIM ** -0.5 * LOG2_E),
                       [dsa_q_norm.reshape(1, -1)], [sds((n, 1024), BF16)], [1024], 1024, 1024, 40)
    (qm,) = _proj_call("proj_mem_q", x2d, g_mix, _cols(w_in, "mq").astype(BF16),
                       functools.partial(_epi_headnorm, MEM_HEAD_DIM, MEM_HEAD_DIM ** -0.5),
                       [mem_q_norm.reshape(1, -1)], [sds((n, 1024), BF16)], [1024], 1024, 1024, 40)

    w_kv = jnp.concatenate([_cols(w_in, "dk"), _cols(w_in, "dv"), _pad_cols(_cols(w_in, "ik"), LANE),
                            _pad_cols(_cols(w_in, "iw"), LANE)], axis=1).astype(BF16)
    kn, vv, ki, wi = _proj_call(
        "proj_dsa_kv", x2d, g_mix, w_kv, _epi_dsa_kv,
        [dsa_k_norm.reshape(1, -1), _pad_cols(idx_k_norm.reshape(1, -1), LANE)],
        [sds((n, 256), BF16), sds((n, 256), BF16), sds((n, LANE), BF16), sds((n, LANE), F32)],
        [256, 256, LANE, LANE], 1024, 768, 40)

    w_iq = _cols(w_in, "iq").reshape(d, IDX_HEADS, IDX_DIM)
    w_iq = jnp.pad(w_iq, ((0, 0), (0, 0), (0, LANE - IDX_DIM))).reshape(d, IDX_HEADS * LANE).astype(BF16)
    (qi,) = _proj_call("proj_idx_q", x2d, g_mix, w_iq, _epi_plain, [],
                       [sds((n, IDX_HEADS * LANE), BF16)], [1024], 1024, 1024, 40)

    (gate,) = _proj_call("proj_gate", x2d, g_mix, _cols(w_in, "gates").astype(BF16), _epi_gate,
                         [b_gate.reshape(1, -1)], [sds((n, 3 * d), F32)], [1024], 1024, 1024, 40)
    wa = jnp.pad(w_gla_a2, ((0, LANE - GLA_GATE_RANK), (0, 0))).astype(BF16)
    o_gla = _gla_call(zg, wa, b_gla_a2.reshape(1, -1), gla_norm.reshape(1, -1), batch, seq, 512)

    o_dsa = _dsa_call(qn, qi, wi, kn, vv, ki, batch, seq)

    (kv,) = _proj_call("proj_mem_kv", mem2d, g_mem, w_mem_kv.astype(BF16), _epi_mem_kv,
                       [mem_k_norm.reshape(1, -1)], [sds((mem2d.shape[0], 2 * d), BF16)], [1024],
                       mem2d.shape[0], 1024, 40)
    o_mem = _mem_call(qm, kv, batch, seq, mem_len, 512)

    x1 = _merge_call(x2d, o_gla, o_dsa, o_mem, gate, w_o_gla.astype(BF16), w_o_dsa.astype(BF16),
                     w_o_mem.astype(BF16), w_out.astype(BF16), 256)

    w_r = _pad_cols(jnp.concatenate([w_r1, w_r2], axis=1), LANE)
    wr_hi = w_r.astype(BF16)
    wr_lo = (w_r - wr_hi.astype(F32)).astype(BF16)
    b_r = _pad_cols(jnp.concatenate([b_r1, b_r2]).reshape(1, -1), LANE)
    return _moe_call(x1, g_ffn, wr_hi, wr_lo, b_r, w_gate.astype(BF16), w_up.astype(BF16),
                     w_down.astype(BF16), 512)


def kernel(x, mem, g_mix, g_mem, w_in, w_gla_a2, b_gla_a2, gla_norm, w_mem_kv, dsa_q_norm, dsa_k_norm, idx_k_norm, mem_q_norm, mem_k_norm, b_gate, w_o_gla, w_o_dsa, w_o_mem, w_out, g_ffn, w_r1, b_r1, w_r2, b_r2, w_gate, w_up, w_down):
    batch, seq, d = x.shape
    mem_len = mem.shape[1]
    x2d = x.reshape(batch * seq, d)
    mem2d = mem.reshape(batch * mem_len, d)
    params = (g_mix, g_mem, w_in, w_gla_a2, b_gla_a2, gla_norm, w_mem_kv, dsa_q_norm, dsa_k_norm,
              idx_k_norm, mem_q_norm, mem_k_norm, b_gate, w_o_gla, w_o_dsa, w_o_mem, w_out, g_ffn,
              w_r1, b_r1, w_r2, b_r2, w_gate, w_up, w_down)
    for layer in range(g_mix.shape[0]):
        x2d = _layer(x2d, mem2d, batch, seq, mem_len, *(p[layer] for p in params))
    return x2d.reshape(batch, seq, d)
```

```python
import functools

import jax
import jax.numpy as jnp
from jax import lax
from jax.experimental import pallas as pl
from jax.experimental.pallas import tpu as pltpu

F32 = jnp.float32
BF16 = jnp.bfloat16
I32 = jnp.int32

D_MODEL = 1024
EPS = 1e-6

GLA_HEADS = 4
GLA_DK = 128
GLA_DV = 256
GLA_GATE_RANK = 16
GLA_GATE_TEMP = 16.0
GLA_CHUNK = 64

DSA_HEADS = 8
DSA_KV_HEADS = 2
DSA_HEAD_DIM = 128
IDX_HEADS = 8
IDX_DIM = 64
IDX_TOPK_MAX = 256

MEM_HEADS = 4
MEM_HEAD_DIM = 256

N_GROUPS = 4
EXPERTS_PER_GROUP = 4
N_EXPERTS = 16
D_EXPERT = 256

LANE = 128
MASKED_LOGIT = -1e30
LOG2_E = 1.4426950408889634

_OFF = {}
_acc = 0
for _name, _w in (("gq", 512), ("gk", 512), ("gv", 1024), ("gg", 1024), ("ga", 16), ("dq", 1024),
                  ("dk", 256), ("dv", 256), ("iq", 512), ("ik", 64), ("iw", 8), ("mq", 1024),
                  ("gates", 3072)):
    _OFF[_name] = (_acc, _acc + _w)
    _acc += _w


def _cols(w_in, name):
    lo, hi = _OFF[name]
    return w_in[:, lo:hi]


def _pad_cols(w, width):
    return jnp.pad(w, ((0, 0), (0, width - w.shape[1])))


def _rms_rows(x):
    return x * lax.rsqrt(jnp.mean(x * x, axis=-1, keepdims=True) + EPS)


def _dot(a, b):
    return jnp.dot(a, b, preferred_element_type=F32)


def _dot_nt(a, b):
    return lax.dot_general(a, b, (((1,), (1,)), ((), ())), preferred_element_type=F32)


def _dot_tn(a, b):
    return lax.dot_general(a, b, (((0,), (0,)), ((), ())), preferred_element_type=F32)


def _sigmoid(x):
    return 1.0 / (1.0 + jnp.exp(-x))


def _params(sem, vmem_mb):
    return pltpu.CompilerParams(dimension_semantics=sem, vmem_limit_bytes=vmem_mb * 1024 * 1024)


def _proj_body(epilogue, n_extra, x_ref, g_ref, w_ref, *rest):
    extra = rest[:n_extra]
    outs = rest[n_extra:-1]
    h_scr = rest[-1]

    @pl.when(pl.program_id(1) == 0)
    def _():
        h_scr[...] = (_rms_rows(x_ref[...]) * g_ref[...]).astype(BF16)

    epilogue(_dot(h_scr[...], w_ref[...]), extra, outs)


def _proj_call(name, x2d, gain, w, epilogue, extras, out_shapes, out_widths, tm, tn, vmem_mb):
    n, d = x2d.shape
    c = w.shape[1]
    in_specs = [pl.BlockSpec((tm, d), lambda i, j: (i, 0)),
                pl.BlockSpec((1, d), lambda i, j: (0, 0)),
                pl.BlockSpec((d, tn), lambda i, j: (0, j))]
    single = c == tn
    for e in extras:
        if e.shape[1] == c and not single:
            in_specs.append(pl.BlockSpec((1, tn), lambda i, j: (0, j)))
        else:
            in_specs.append(pl.BlockSpec(e.shape, lambda i, j: (0, 0)))
    out_specs = [pl.BlockSpec((tm, ow), (lambda i, j: (i, 0)) if single else (lambda i, j: (i, j)))
                 for ow in out_widths]
    return pl.pallas_call(
        functools.partial(_proj_body, epilogue, len(extras)),
        grid=(n // tm, c // tn),
        in_specs=in_specs,
        out_specs=out_specs,
        out_shape=out_shapes,
        scratch_shapes=[pltpu.VMEM((tm, d), BF16)],
        compiler_params=_params(("parallel", "arbitrary"), vmem_mb),
        name=name,
    )(x2d, gain.reshape(1, d), w, *extras)


def _epi_plain(acc, extra, outs):
    outs[0][...] = acc.astype(outs[0].dtype)


def _epi_headnorm(head_dim, scale, acc, extra, outs):
    gain = extra[0][...]
    for h in range(acc.shape[1] // head_dim):
        sl = slice(h * head_dim, (h + 1) * head_dim)
        outs[0][:, sl] = (_rms_rows(acc[:, sl]) * gain * scale).astype(outs[0].dtype)


def _epi_dsa_kv(acc, extra, outs):
    k_gain, idx_gain = extra
    kn_ref, v_ref, ki_ref, wi_ref = outs
    for h in range(DSA_KV_HEADS):
        sl = slice(h * DSA_HEAD_DIM, (h + 1) * DSA_HEAD_DIM)
        kn_ref[:, sl] = (_rms_rows(acc[:, sl]) * k_gain[...]).astype(BF16)
    v_ref[...] = acc[:, 256:512].astype(BF16)
    a = acc[:, 512:640]
    ms = jnp.sum(a * a, axis=-1, keepdims=True) * (1.0 / IDX_DIM)
    ki_ref[...] = (a * lax.rsqrt(ms + EPS) * idx_gain[...]).astype(BF16)
    wi_ref[...] = acc[:, 640:768] * (IDX_HEADS ** -0.5 * IDX_DIM ** -0.5)


def _epi_gate(acc, extra, outs):
    outs[0][...] = _sigmoid(acc + extra[0][...])


def _epi_mem_kv(acc, extra, outs):
    gain = extra[0][...]
    j = pl.program_id(1)

    @pl.when(j == 0)
    def _():
        for h in range(MEM_HEADS):
            sl = slice(h * MEM_HEAD_DIM, (h + 1) * MEM_HEAD_DIM)
            outs[0][:, sl] = (_rms_rows(acc[:, sl]) * gain).astype(BF16)

    @pl.when(j == 1)
    def _():
        outs[0][...] = acc.astype(BF16)


def _split3(x):
    hi = x.astype(BF16)
    r1 = x - hi.astype(F32)
    mid = r1.astype(BF16)
    lo = (r1 - mid.astype(F32)).astype(BF16)
    return hi, mid, lo


def _gla_body(q_ref, k_ref, v_ref, g_ref, a_ref, wa_ref, ba_ref, ng_ref, o_ref,
              st_scr, qd_scr, ki_scr, ke_scr, dec_scr):
    tb = q_ref.shape[0]
    c = GLA_CHUNK

    @pl.when(pl.program_id(1) == 0)
    def _():
        st_scr[...] = jnp.zeros_like(st_scr)

    pre = _dot(a_ref[...].astype(BF16), wa_ref[...]) + ba_ref[...]
    log_a = (jnp.minimum(pre, 0.0) - jnp.log1p(jnp.exp(-jnp.abs(pre)))) * (1.0 / GLA_GATE_TEMP)

    row = lax.broadcasted_iota(I32, (tb, tb), 0)
    col = lax.broadcasted_iota(I32, (tb, tb), 1)
    shift = c.bit_length() - 1
    same_chunk = (row >> shift) == (col >> shift)
    chunk_ones = jnp.where(same_chunk, 1.0, 0.0)
    tri = jnp.where(col <= row, chunk_ones, 0.0).astype(BF16)
    blk = chunk_ones.astype(BF16)
    parts = _split3(log_a)
    cum = _dot(tri, parts[0]) + _dot(tri, parts[1]) + _dot(tri, parts[2])
    tot = _dot(blk, parts[0]) + _dot(blk, parts[1]) + _dot(blk, parts[2])

    kf = k_ref[...]
    qd_scr[...] = (q_ref[...] * (GLA_DK ** -0.5) * jnp.exp(cum)).astype(BF16)
    ki_scr[...] = (kf * jnp.exp(-cum)).astype(BF16)
    ke_scr[...] = (kf * jnp.exp(tot - cum)).astype(BF16)
    dec_scr[...] = jnp.exp(tot)

    causal = lax.broadcasted_iota(I32, (c, c), 1) <= lax.broadcasted_iota(I32, (c, c), 0)

    def chunk(ci, carry):
        r0 = pl.multiple_of(ci * c, c)
        rows = pl.ds(r0, c)
        for h in range(GLA_HEADS):
            ks = slice(h * GLA_DK, (h + 1) * GLA_DK)
            vs = slice(h * GLA_DV, (h + 1) * GLA_DV)
            qd = qd_scr[rows, ks]
            vv = v_ref[rows, vs].astype(BF16)
            att = jnp.where(causal, _dot_nt(qd, ki_scr[rows, ks]), 0.0).astype(BF16)
            st = st_scr[h]
            o = _dot(att, vv) + _dot_nt(qd, st.astype(BF16))
            st_scr[h] = st * dec_scr[pl.ds(r0, 1), ks] + _dot_tn(vv, ke_scr[rows, ks])
            gg = g_ref[rows, vs]
            o_ref[rows, vs] = (_rms_rows(o) * ng_ref[...] * (gg * _sigmoid(gg))).astype(o_ref.dtype)
        return carry

    lax.fori_loop(0, tb // c, chunk, 0)


def _gla_call(zg, wa, ba, ng, batch, seq, tb):
    n = zg.shape[0]
    nb = seq // tb
    row = lambda b, i: b * nb + i
    in_specs = [
        pl.BlockSpec((tb, 512), lambda b, i: (row(b, i), 0)),
        pl.BlockSpec((tb, 512), lambda b, i: (row(b, i), 1)),
        pl.BlockSpec((tb, 1024), lambda b, i: (row(b, i), 1)),
        pl.BlockSpec((tb, 1024), lambda b, i: (row(b, i), 2)),
        pl.BlockSpec((tb, LANE), lambda b, i: (row(b, i), 3072 // LANE)),
        pl.BlockSpec(wa.shape, lambda b, i: (0, 0)),
        pl.BlockSpec(ba.shape, lambda b, i: (0, 0)),
        pl.BlockSpec(ng.shape, lambda b, i: (0, 0)),
    ]
    hk = GLA_HEADS * GLA_DK
    return pl.pallas_call(
        _gla_body,
        grid=(batch, nb),
        in_specs=in_specs,
        out_specs=pl.BlockSpec((tb, GLA_HEADS * GLA_DV), lambda b, i: (row(b, i), 0)),
        out_shape=jax.ShapeDtypeStruct((n, GLA_HEADS * GLA_DV), BF16),
        scratch_shapes=[pltpu.VMEM((GLA_HEADS, GLA_DV, GLA_DK), F32),
                        pltpu.VMEM((tb, hk), BF16), pltpu.VMEM((tb, hk), BF16),
                        pltpu.VMEM((tb, hk), BF16), pltpu.VMEM((tb, hk), F32)],
        compiler_params=_params(("parallel", "arbitrary"), 48),
        name="gla",
    )(zg, zg, zg, zg, zg, wa, ba, ng)


DSA_TQ = 128
DSA_TK = 512
DSA_MAX_BISECT = 320
DSA_MIN_DENOM = 2.0 ** -100


def _dsa_body(topk, qn_ref, qi_ref, wi_ref, kn_ref, v_ref, ki_ref, o_ref,
              sc_scr, qa_scr, kmax_scr, m_scr, l_scr, acc_scr):
    tq, tk, hd = DSA_TQ, DSA_TK, DSA_HEAD_DIM
    grp = DSA_HEADS // DSA_KV_HEADS
    kf = float(topk)
    q0 = pl.program_id(1) * tq
    nk = (q0 + tq + tk - 1) // tk
    qpos = q0 + lax.broadcasted_iota(I32, (tq, tk), 0)
    kofs = lax.broadcasted_iota(I32, (tq, tk), 1)
    inf = jnp.inf

    def chunk_start(ci):
        return pl.multiple_of(ci * tk, tk)

    @pl.when(pl.program_id(1) == 0)
    def _():
        def body(ci, mx):
            kc = kn_ref[pl.ds(chunk_start(ci), tk), :].astype(F32)
            return tuple(jnp.maximum(mx[g], jnp.sum(kc[:, g * hd:(g + 1) * hd] ** 2, axis=1, keepdims=True))
                         for g in range(DSA_KV_HEADS))

        mx = lax.fori_loop(0, kn_ref.shape[0] // tk, body,
                           tuple(jnp.zeros((tk, 1), F32) for _ in range(DSA_KV_HEADS)))
        for g in range(DSA_KV_HEADS):
            kmax_scr[g] = jnp.broadcast_to(jnp.max(mx[g], axis=0, keepdims=True), kmax_scr.shape[1:])

    wi = wi_ref[...]

    def score_chunk(ci, carry):
        rmax, rmin = carry
        k0 = chunk_start(ci)
        kc = ki_ref[pl.ds(k0, tk), :]
        score = jnp.zeros((tq, tk), F32)
        for h in range(IDX_HEADS):
            qh = qi_ref[:, h * LANE:(h + 1) * LANE]
            score = score + wi[:, h:h + 1] * jnp.maximum(_dot_nt(qh, kc), 0.0)
        valid = k0 + kofs <= qpos
        masked = jnp.where(valid, score, -inf)
        sc_scr[:, pl.ds(k0, tk)] = masked
        for t in range(tk // LANE):
            ts = slice(t * LANE, (t + 1) * LANE)
            rmax = jnp.maximum(rmax, masked[:, ts])
            rmin = jnp.minimum(rmin, jnp.where(valid[:, ts], score[:, ts], inf))
        return rmax, rmin

    rmax, rmin = lax.fori_loop(0, nk, score_chunk,
                               (jnp.full((tq, LANE), -inf, F32), jnp.full((tq, LANE), inf, F32)))
    hi0 = jnp.max(rmax, axis=1, keepdims=True)
    lo0 = jnp.min(rmin, axis=1, keepdims=True)

    def count(pred, thr):
        thr_b = jnp.broadcast_to(thr, (tq, LANE))

        def body(ci, cnt):
            sc = sc_scr[:, pl.ds(chunk_start(ci), tk)]
            for t in range(tk // LANE):
                cnt = cnt + jnp.where(pred(sc[:, t * LANE:(t + 1) * LANE], thr_b), 1.0, 0.0)
            return cnt

        cnt = lax.fori_loop(0, nk, body, jnp.zeros((tq, LANE), F32))
        return jnp.sum(cnt, axis=1, keepdims=True)

    ge_pred = lambda a, b: a >= b
    n_causal = (q0 + 1 + lax.broadcasted_iota(I32, (tq, 1), 0)).astype(F32)
    c_hi0 = count(ge_pred, hi0)
    max_tied = c_hi0 >= kf
    lo_init = jnp.where(max_tied, hi0, lo0)
    c_init = jnp.where(max_tied, c_hi0, n_causal)
    fin_init = jnp.where(max_tied | (n_causal <= kf), 1.0, 0.0)

    def bisect_cond(carry):
        it, _, _, _, fin = carry
        return (it < DSA_MAX_BISECT) & (jnp.min(fin) < 1.0)

    def bisect(carry):
        it, lo, hi, c_lo, fin = carry
        mid = 0.5 * lo + 0.5 * hi
        cnt = count(ge_pred, mid)
        stuck = (mid <= lo) | (mid >= hi)
        live = (fin < 1.0) & jnp.logical_not(stuck)
        ge = cnt >= kf
        up = live & ge
        lo = jnp.where(up, mid, lo)
        c_lo = jnp.where(up, cnt, c_lo)
        hi = jnp.where(live & jnp.logical_not(ge), mid, hi)
        fin = jnp.where(stuck | (cnt == kf), 1.0, fin)
        return it + 1, lo, hi, c_lo, fin

    _, thr, _, c_ge, _ = lax.while_loop(bisect_cond, bisect, (jnp.int32(0), lo_init, hi0, c_init, fin_init))
    thr_w = jnp.broadcast_to(thr, (tq, tk))

    @pl.when(jnp.max(c_ge) > kf)
    def _():
        c_eq = count(lambda a, b: a == b, thr)
        allowed = kf - (c_ge - c_eq)
        upper = jnp.where(lax.broadcasted_iota(I32, (tk, tk), 0) <= lax.broadcasted_iota(I32, (tk, tk), 1),
                          1.0, 0.0).astype(BF16)

        def body(ci, seen):
            k0 = chunk_start(ci)
            sc = sc_scr[:, pl.ds(k0, tk)]
            eq = jnp.where(sc == thr_w, 1.0, 0.0)
            rank = seen + _dot(eq.astype(BF16), upper)
            drop = (eq > 0.0) & (rank > allowed)
            sc_scr[:, pl.ds(k0, tk)] = jnp.where(drop, -inf, sc)
            return seen + jnp.sum(eq, axis=1, keepdims=True)

        lax.fori_loop(0, nk, body, jnp.zeros((tq, 1), F32))

    lane_q = lax.broadcasted_iota(I32, (tq, hd), 1)
    for g in range(DSA_KV_HEADS):
        kmax2 = kmax_scr[g][0:1, :]
        for r in range(grp):
            hcol = (g * grp + r) * hd
            rows = slice(r * tq, (r + 1) * tq)
            qh = qn_ref[:, hcol:hcol + hd]
            qf = qh.astype(F32)
            bound = jnp.sqrt(jnp.sum(qf * qf, axis=1, keepdims=True) * kmax2)
            qa_scr[g, rows, 0:hd] = qh
            qa_scr[g, rows, hd:2 * hd] = jnp.where(lane_q == 0, bound, 0.0).astype(BF16)
    acc_scr[...] = jnp.zeros_like(acc_scr)
    lane_k = lax.broadcasted_iota(I32, (tk, hd), 1)
    k_tail = jnp.where(lane_k == 0, -1.0, 0.0).astype(BF16)
    v_tail = jnp.where(lane_k == 0, 1.0, 0.0).astype(BF16)

    def sel_bias(k0):
        return jnp.where(sc_scr[:, pl.ds(k0, tk)] >= thr_w, 0.0, -inf)

    def attend(ci, carry):
        k0 = chunk_start(ci)
        bias = sel_bias(k0)
        for g in range(DSA_KV_HEADS):
            hs = slice(g * hd, (g + 1) * hd)
            ka = jnp.concatenate([kn_ref[pl.ds(k0, tk), hs], k_tail], axis=1)
            va = jnp.concatenate([v_ref[pl.ds(k0, tk), hs], v_tail], axis=1)
            s = _dot_nt(qa_scr[g], ka)
            p = jnp.exp2((s.reshape(grp, tq, tk) + bias[None]).reshape(grp * tq, tk))
            acc_scr[g] += _dot(p.astype(BF16), va)
        return carry

    lax.fori_loop(0, nk, attend, 0)

    def write_out(g, out):
        for r in range(grp):
            hcol = (g * grp + r) * hd
            o_ref[:, hcol:hcol + hd] = out[r * tq:(r + 1) * tq, :].astype(o_ref.dtype)

    l_min = inf
    for g in range(DSA_KV_HEADS):
        acc = acc_scr[g]
        denom = acc[:, hd:hd + 1]
        l_min = jnp.minimum(l_min, jnp.min(denom))
        write_out(g, acc[:, 0:hd] / denom)

    @pl.when(jnp.logical_not(l_min >= DSA_MIN_DENOM))
    def _():
        m_scr[...] = jnp.full(m_scr.shape, MASKED_LOGIT, F32)
        l_scr[...] = jnp.zeros_like(l_scr)
        acc_scr[...] = jnp.zeros_like(acc_scr)

        def attend_online(ci, carry):
            k0 = chunk_start(ci)
            bias = sel_bias(k0)
            for g in range(DSA_KV_HEADS):
                hs = slice(g * hd, (g + 1) * hd)
                s = _dot_nt(qa_scr[g, :, 0:hd], kn_ref[pl.ds(k0, tk), hs])
                s = (s.reshape(grp, tq, tk) + bias[None]).reshape(grp * tq, tk)
                m_prev = m_scr[g]
                m_new = jnp.maximum(m_prev, jnp.max(s, axis=1, keepdims=True))
                p = jnp.exp2(s - jnp.tile(m_new, (1, tk // LANE)))
                alpha = jnp.exp2(m_prev - m_new)
                l_scr[g] = alpha * l_scr[g] + jnp.sum(p, axis=1, keepdims=True)
                acc_scr[g, :, 0:hd] = alpha * acc_scr[g, :, 0:hd] + _dot(p.astype(BF16), v_ref[pl.ds(k0, tk), hs])
                m_scr[g] = m_new
            return carry

        lax.fori_loop(0, nk, attend_online, 0)
        for g in range(DSA_KV_HEADS):
            write_out(g, acc_scr[g, :, 0:hd] / l_scr[g])


def _dsa_call(qn, qi, wi, kn, vv, ki, batch, seq):
    n = qn.shape[0]
    tq, hd = DSA_TQ, DSA_HEAD_DIM
    nb = seq // tq
    topk = min(IDX_TOPK_MAX, seq // 4)
    grp = DSA_HEADS // DSA_KV_HEADS
    row = lambda b, i: (b * nb + i, 0)
    per_batch = lambda b, i: (b, 0)
    return pl.pallas_call(
        functools.partial(_dsa_body, topk),
        grid=(batch, nb),
        in_specs=[pl.BlockSpec((tq, qn.shape[1]), row),
                  pl.BlockSpec((tq, qi.shape[1]), row),
                  pl.BlockSpec((tq, LANE), row),
                  pl.BlockSpec((seq, kn.shape[1]), per_batch),
                  pl.BlockSpec((seq, vv.shape[1]), per_batch),
                  pl.BlockSpec((seq, LANE), per_batch)],
        out_specs=pl.BlockSpec((tq, DSA_HEADS * hd), row),
        out_shape=jax.ShapeDtypeStruct((n, DSA_HEADS * hd), BF16),
        scratch_shapes=[pltpu.VMEM((tq, seq), F32),
                        pltpu.VMEM((DSA_KV_HEADS, grp * tq, 2 * hd), BF16),
                        pltpu.VMEM((DSA_KV_HEADS, 8, LANE), F32),
                        pltpu.VMEM((DSA_KV_HEADS, grp * tq, LANE), F32),
                        pltpu.VMEM((DSA_KV_HEADS, grp * tq, LANE), F32),
                        pltpu.VMEM((DSA_KV_HEADS, grp * tq, 2 * hd), F32)],
        compiler_params=_params(("arbitrary", "arbitrary"), 48),
        name="dsa",
    )(qn, qi, wi, kn, vv, ki)


def _mem_body(q_ref, k_ref, v_ref, o_ref):
    for h in range(MEM_HEADS):
        sl = slice(h * MEM_HEAD_DIM, (h + 1) * MEM_HEAD_DIM)
        s = _dot_nt(q_ref[:, sl], k_ref[:, sl])
        p = jnp.exp(s - jnp.max(s, axis=1, keepdims=True))
        o = _dot(p.astype(BF16), v_ref[:, sl]) / jnp.sum(p, axis=1, keepdims=True)
        o_ref[:, sl] = o.astype(o_ref.dtype)


def _mem_call(qm, kv, batch, seq, mem_len, tm):
    n, d = qm.shape
    nb = seq // tm
    return pl.pallas_call(
        _mem_body,
        grid=(batch, nb),
        in_specs=[pl.BlockSpec((tm, d), lambda b, i: (b * nb + i, 0)),
                  pl.BlockSpec((mem_len, d), lambda b, i: (b, 0)),
                  pl.BlockSpec((mem_len, d), lambda b, i: (b, 1))],
        out_specs=pl.BlockSpec((tm, d), lambda b, i: (b * nb + i, 0)),
        out_shape=jax.ShapeDtypeStruct((n, d), BF16),
        compiler_params=_params(("parallel", "parallel"), 32),
        name="mem_attn",
    )(qm, kv, kv)


def _merge_body(x_ref, og_ref, od_ref, om_ref, gate_ref, wg_ref, wd_ref, wm_ref, wo_ref, o_ref):
    d = D_MODEL
    merged = (gate_ref[:, 0:d] * _dot(og_ref[...], wg_ref[...])
              + gate_ref[:, d:2 * d] * _dot(od_ref[...], wd_ref[...])
              + gate_ref[:, 2 * d:3 * d] * _dot(om_ref[...], wm_ref[...]))
    o_ref[...] = x_ref[...] + _dot(merged.astype(BF16), wo_ref[...])


def _merge_call(x2d, o_gla, o_dsa, o_mem, gate, wg, wd, wm, wo, tm):
    n, d = x2d.shape
    row = lambda i: (i, 0)
    fixed = lambda i: (0, 0)
    return pl.pallas_call(
        _merge_body,
        grid=(n // tm,),
        in_specs=[pl.BlockSpec((tm, d), row), pl.BlockSpec((tm, d), row), pl.BlockSpec((tm, d), row),
                  pl.BlockSpec((tm, d), row), pl.BlockSpec((tm, 3 * d), row),
                  pl.BlockSpec((d, d), fixed), pl.BlockSpec((d, d), fixed),
                  pl.BlockSpec((d, d), fixed), pl.BlockSpec((d, d), fixed)],
        out_specs=pl.BlockSpec((tm, d), row),
        out_shape=jax.ShapeDtypeStruct((n, d), F32),
        compiler_params=_params(("parallel",), 48),
        name="merge",
    )(x2d, o_gla, o_dsa, o_mem, gate, wg, wd, wm, wo)


def _route(z):
    lane = lax.broadcasted_iota(I32, z.shape, 1)
    lanef = lane.astype(F32)
    far = 1e9
    neg = -jnp.inf
    gmask = lane < N_GROUPS
    gmax = jnp.max(jnp.where(gmask, z, neg), axis=1, keepdims=True)
    gsum = jnp.sum(jnp.where(gmask, jnp.exp(z - gmax), 0.0), axis=1, keepdims=True)
    g_w = 1.0 / gsum
    g_sel = jnp.min(jnp.where(gmask & (z == gmax), lanef, far), axis=1, keepdims=True)
    lane_grp = ((lane - N_GROUPS) >> 2).astype(F32)
    emask = (lane >= N_GROUPS) & (lane < N_GROUPS + N_EXPERTS) & (lane_grp == g_sel)
    e1 = jnp.max(jnp.where(emask, z, neg), axis=1, keepdims=True)
    i1 = jnp.min(jnp.where(emask & (z == e1), lanef, far), axis=1, keepdims=True)
    emask2 = emask & (lanef != i1)
    e2 = jnp.max(jnp.where(emask2, z, neg), axis=1, keepdims=True)
    i2 = jnp.min(jnp.where(emask2 & (z == e2), lanef, far), axis=1, keepdims=True)
    t = jnp.exp(e2 - e1)
    p1 = 1.0 / (1.0 + t)
    p2 = t / (1.0 + t)
    return jnp.where(lanef == i1, g_w * p1, jnp.where(lanef == i2, g_w * p2, 0.0))


def _moe_body(x_ref, g_ref, wrh_ref, wrl_ref, br_ref, wg_ref, wu_ref, wd_ref, o_ref, h_scr, comb_scr):
    e = pl.program_id(1)

    @pl.when(e == 0)
    def _():
        x = x_ref[...]
        h = _rms_rows(x) * g_ref[...]
        hh = h.astype(BF16)
        hl = (h - hh.astype(F32)).astype(BF16)
        h_scr[...] = hh
        z = _dot(hh, wrh_ref[...]) + _dot(hl, wrh_ref[...]) + _dot(hh, wrl_ref[...]) + br_ref[...]
        comb_scr[...] = _route(z)
        o_ref[...] = x

    lane = lax.broadcasted_iota(I32, comb_scr.shape, 1)
    ce = jnp.sum(jnp.where(lane == e + N_GROUPS, comb_scr[...], 0.0), axis=1, keepdims=True)
    h = h_scr[...]
    gt = _dot(h, wg_ref[0])
    up = _dot(h, wu_ref[0])
    hid = gt * _sigmoid(gt) * up * ce
    o_ref[...] += _dot(hid.astype(BF16), wd_ref[0])


def _moe_call(x1, g_ffn, wr_hi, wr_lo, br, wg, wu, wd, tm):
    n, d = x1.shape
    row = lambda i, e: (i, 0)
    fixed = lambda i, e: (0, 0)
    per_e = lambda i, e: (e, 0, 0)
    return pl.pallas_call(
        _moe_body,
        grid=(n // tm, N_EXPERTS),
        in_specs=[pl.BlockSpec((tm, d), row), pl.BlockSpec((1, d), fixed),
                  pl.BlockSpec((d, LANE), fixed), pl.BlockSpec((d, LANE), fixed),
                  pl.BlockSpec((1, LANE), fixed),
                  pl.BlockSpec((1, d, D_EXPERT), per_e), pl.BlockSpec((1, d, D_EXPERT), per_e),
                  pl.BlockSpec((1, D_EXPERT, d), per_e)],
        out_specs=pl.BlockSpec((tm, d), row),
        out_shape=jax.ShapeDtypeStruct((n, d), F32),
        scratch_shapes=[pltpu.VMEM((tm, d), BF16), pltpu.VMEM((tm, LANE), F32)],
        compiler_params=_params(("parallel", "arbitrary"), 48),
        name="moe",
    )(x1, g_ffn.reshape(1, d), wr_hi, wr_lo, br, wg, wu, wd)


def _layer(x2d, mem2d, batch, seq, mem_len, g_mix, g_mem, w_in, w_gla_a2, b_gla_a2, gla_norm,
           w_mem_kv, dsa_q_norm, dsa_k_norm, idx_k_norm, mem_q_norm, mem_k_norm, b_gate,
           w_o_gla, w_o_dsa, w_o_mem, w_out, g_ffn, w_r1, b_r1, w_r2, b_r2, w_gate, w_up, w_down):
    n, d = x2d.shape
    sds = jax.ShapeDtypeStruct

    w_gla = jnp.concatenate([_cols(w_in, "gq"), _cols(w_in, "gk"), _cols(w_in, "gv"), _cols(w_in, "gg"),
                             _pad_cols(_cols(w_in, "ga"), LANE)], axis=1).astype(BF16)
    (zg,) = _proj_call("proj_gla", x2d, g_mix, w_gla, _epi_plain, [],
                       [sds((n, w_gla.shape[1]), F32)], [640], 1024, 640, 40)

    (qn,) = _proj_call("proj_dsa_q", x2d, g_mix, _cols(w_in, "dq").astype(BF16),
                       functools.partial(_epi_headnorm, DSA_HEAD_DIM, DSA_HEAD_DIM ** -0.5 * LOG2_E),
                       [dsa_q_norm.reshape(1, -1)], [sds((n, 1024), BF16)], [1024], 1024, 1024, 40)
    (qm,) = _proj_call("proj_mem_q", x2d, g_mix, _cols(w_in, "mq").astype(BF16),
                       functools.partial(_epi_headnorm, MEM_HEAD_DIM, MEM_HEAD_DIM ** -0.5),
                       [mem_q_norm.reshape(1, -1)], [sds((n, 1024), BF16)], [1024], 1024, 1024, 40)

    w_kv = jnp.concatenate([_cols(w_in, "dk"), _cols(w_in, "dv"), _pad_cols(_cols(w_in, "ik"), LANE),
                            _pad_cols(_cols(w_in, "iw"), LANE)], axis=1).astype(BF16)
    kn, vv, ki, wi = _proj_call(
        "proj_dsa_kv", x2d, g_mix, w_kv, _epi_dsa_kv,
        [dsa_k_norm.reshape(1, -1), _pad_cols(idx_k_norm.reshape(1, -1), LANE)],
        [sds((n, 256), BF16), sds((n, 256), BF16), sds((n, LANE), BF16), sds((n, LANE), F32)],
        [256, 256, LANE, LANE], 1024, 768, 40)

    w_iq = _cols(w_in, "iq").reshape(d, IDX_HEADS, IDX_DIM)
    w_iq = jnp.pad(w_iq, ((0, 0), (0, 0), (0, LANE - IDX_DIM))).reshape(d, IDX_HEADS * LANE).astype(BF16)
    (qi,) = _proj_call("proj_idx_q", x2d, g_mix, w_iq, _epi_plain, [],
                       [sds((n, IDX_HEADS * LANE), BF16)], [1024], 1024, 1024, 40)

    (gate,) = _proj_call("proj_gate", x2d, g_mix, _cols(w_in, "gates").astype(BF16), _epi_gate,
                         [b_gate.reshape(1, -1)], [sds((n, 3 * d), F32)], [1024], 1024, 1024, 40)
    wa = jnp.pad(w_gla_a2, ((0, LANE - GLA_GATE_RANK), (0, 0))).astype(BF16)
    o_gla = _gla_call(zg, wa, b_gla_a2.reshape(1, -1), gla_norm.reshape(1, -1), batch, seq, 512)

    o_dsa = _dsa_call(qn, qi, wi, kn, vv, ki, batch, seq)

    (kv,) = _proj_call("proj_mem_kv", mem2d, g_mem, w_mem_kv.astype(BF16), _epi_mem_kv,
                       [mem_k_norm.reshape(1, -1)], [sds((mem2d.shape[0], 2 * d), BF16)], [1024],
                       mem2d.shape[0], 1024, 40)
    o_mem = _mem_call(qm, kv, batch, seq, mem_len, 512)

    x1 = _merge_call(x2d, o_gla, o_dsa, o_mem, gate, w_o_gla.astype(BF16), w_o_dsa.astype(BF16),
                     w_o_mem.astype(BF16), w_out.astype(BF16), 256)

    w_r = _pad_cols(jnp.concatenate([w_r1, w_r2], axis=1), LANE)
    wr_hi = w_r.astype(BF16)
    wr_lo = (w_r - wr_hi.astype(F32)).astype(BF16)
    b_r = _pad_cols(jnp.concatenate([b_r1, b_r2]).reshape(1, -1), LANE)
    return _moe_call(x1, g_ffn, wr_hi, wr_lo, b_r, w_gate.astype(BF16), w_up.astype(BF16),
                     w_down.astype(BF16), 512)


def kernel(x, mem, g_mix, g_mem, w_in, w_gla_a2, b_gla_a2, gla_norm, w_mem_kv, dsa_q_norm, dsa_k_norm, idx_k_norm, mem_q_norm, mem_k_norm, b_gate, w_o_gla, w_o_dsa, w_o_mem, w_out, g_ffn, w_r1, b_r1, w_r2, b_r2, w_gate, w_up, w_down):
    batch, seq, d = x.shape
    mem_len = mem.shape[1]
    x2d = x.reshape(batch * seq, d)
    mem2d = mem.reshape(batch * mem_len, d)
    params = (g_mix, g_mem, w_in, w_gla_a2, b_gla_a2, gla_norm, w_mem_kv, dsa_q_norm, dsa_k_norm,
              idx_k_norm, mem_q_norm, mem_k_norm, b_gate, w_o_gla, w_o_dsa, w_o_mem, w_out, g_ffn,
              w_r1, b_r1, w_r2, b_r2, w_gate, w_up, w_down)
    for layer in range(g_mix.shape[0]):
        x2d = _layer(x2d, mem2d, batch, seq, mem_len, *(p[layer] for p in params))
    return x2d.reshape(batch, seq, d)
```

```python
import functools

import jax
import jax.numpy as jnp
from jax import lax
from jax.experimental import pallas as pl
from jax.experimental.pallas import tpu as pltpu

F32 = jnp.float32
BF16 = jnp.bfloat16
I32 = jnp.int32

D_MODEL = 1024
EPS = 1e-6

GLA_HEADS = 4
GLA_DK = 128
GLA_DV = 256
GLA_GATE_RANK = 16
GLA_GATE_TEMP = 16.0
GLA_CHUNK = 64

DSA_HEADS = 8
DSA_KV_HEADS = 2
DSA_HEAD_DIM = 128
IDX_HEADS = 8
IDX_DIM = 64
IDX_TOPK_MAX = 256

MEM_HEADS = 4
MEM_HEAD_DIM = 256

N_GROUPS = 4
EXPERTS_PER_GROUP = 4
N_EXPERTS = 16
D_EXPERT = 256

LANE = 128
MASKED_LOGIT = -1e30
LOG2_E = 1.4426950408889634

_OFF = {}
_acc = 0
for _name, _w in (("gq", 512), ("gk", 512), ("gv", 1024), ("gg", 1024), ("ga", 16), ("dq", 1024),
                  ("dk", 256), ("dv", 256), ("iq", 512), ("ik", 64), ("iw", 8), ("mq", 1024),
                  ("gates", 3072)):
    _OFF[_name] = (_acc, _acc + _w)
    _acc += _w


def _cols(w_in, name):
    lo, hi = _OFF[name]
    return w_in[:, lo:hi]


def _pad_cols(w, width):
    return jnp.pad(w, ((0, 0), (0, width - w.shape[1])))


def _rms_rows(x):
    return x * lax.rsqrt(jnp.mean(x * x, axis=-1, keepdims=True) + EPS)


def _dot(a, b):
    return jnp.dot(a, b, preferred_element_type=F32)


def _dot_nt(a, b):
    return lax.dot_general(a, b, (((1,), (1,)), ((), ())), preferred_element_type=F32)


def _dot_tn(a, b):
    return lax.dot_general(a, b, (((0,), (0,)), ((), ())), preferred_element_type=F32)


def _sigmoid(x):
    return 1.0 / (1.0 + jnp.exp(-x))


def _params(sem, vmem_mb):
    return pltpu.CompilerParams(dimension_semantics=sem, vmem_limit_bytes=vmem_mb * 1024 * 1024)


def _proj_body(epilogue, n_extra, x_ref, g_ref, w_ref, *rest):
    extra = rest[:n_extra]
    outs = rest[n_extra:-1]
    h_scr = rest[-1]

    @pl.when(pl.program_id(1) == 0)
    def _():
        h_scr[...] = (_rms_rows(x_ref[...]) * g_ref[...]).astype(BF16)

    epilogue(_dot(h_scr[...], w_ref[...]), extra, outs)


def _proj_call(name, x2d, gain, w, epilogue, extras, out_shapes, out_widths, tm, tn, vmem_mb):
    n, d = x2d.shape
    c = w.shape[1]
    in_specs = [pl.BlockSpec((tm, d), lambda i, j: (i, 0)),
                pl.BlockSpec((1, d), lambda i, j: (0, 0)),
                pl.BlockSpec((d, tn), lambda i, j: (0, j))]
    single = c == tn
    for e in extras:
        if e.shape[1] == c and not single:
            in_specs.append(pl.BlockSpec((1, tn), lambda i, j: (0, j)))
        else:
            in_specs.append(pl.BlockSpec(e.shape, lambda i, j: (0, 0)))
    out_specs = [pl.BlockSpec((tm, ow), (lambda i, j: (i, 0)) if single else (lambda i, j: (i, j)))
                 for ow in out_widths]
    return pl.pallas_call(
        functools.partial(_proj_body, epilogue, len(extras)),
        grid=(n // tm, c // tn),
        in_specs=in_specs,
        out_specs=out_specs,
        out_shape=out_shapes,
        scratch_shapes=[pltpu.VMEM((tm, d), BF16)],
        compiler_params=_params(("parallel", "arbitrary"), vmem_mb),
        name=name,
    )(x2d, gain.reshape(1, d), w, *extras)


def _epi_plain(acc, extra, outs):
    outs[0][...] = acc.astype(outs[0].dtype)


def _epi_headnorm(head_dim, scale, acc, extra, outs):
    gain = extra[0][...]
    for h in range(acc.shape[1] // head_dim):
        sl = slice(h * head_dim, (h + 1) * head_dim)
        outs[0][:, sl] = (_rms_rows(acc[:, sl]) * gain * scale).astype(outs[0].dtype)


def _epi_dsa_kv(acc, extra, outs):
    k_gain, idx_gain = extra
    kn_ref, v_ref, ki_ref, wi_ref = outs
    for h in range(DSA_KV_HEADS):
        sl = slice(h * DSA_HEAD_DIM, (h + 1) * DSA_HEAD_DIM)
        kn_ref[:, sl] = (_rms_rows(acc[:, sl]) * k_gain[...]).astype(BF16)
    v_ref[...] = acc[:, 256:512].astype(BF16)
    a = acc[:, 512:640]
    ms = jnp.sum(a * a, axis=-1, keepdims=True) * (1.0 / IDX_DIM)
    ki_ref[...] = (a * lax.rsqrt(ms + EPS) * idx_gain[...]).astype(BF16)
    wi_ref[...] = acc[:, 640:768] * (IDX_HEADS ** -0.5 * IDX_DIM ** -0.5)


def _epi_gate(acc, extra, outs):
    outs[0][...] = _sigmoid(acc + extra[0][...])


def _epi_mem_kv(acc, extra, outs):
    gain = extra[0][...]
    j = pl.program_id(1)

    @pl.when(j == 0)
    def _():
        for h in range(MEM_HEADS):
            sl = slice(h * MEM_HEAD_DIM, (h + 1) * MEM_HEAD_DIM)
            outs[0][:, sl] = (_rms_rows(acc[:, sl]) * gain).astype(BF16)

    @pl.when(j == 1)
    def _():
        outs[0][...] = acc.astype(BF16)


def _split3(x):
    hi = x.astype(BF16)
    r1 = x - hi.astype(F32)
    mid = r1.astype(BF16)
    lo = (r1 - mid.astype(F32)).astype(BF16)
    return hi, mid, lo


def _gla_body(q_ref, k_ref, v_ref, g_ref, a_ref, wa_ref, ba_ref, ng_ref, o_ref,
              st_scr, qd_scr, ki_scr, ke_scr, dec_scr):
    tb = q_ref.shape[0]
    c = GLA_CHUNK

    @pl.when(pl.program_id(1) == 0)
    def _():
        st_scr[...] = jnp.zeros_like(st_scr)

    pre = _dot(a_ref[...].astype(BF16), wa_ref[...]) + ba_ref[...]
    log_a = (jnp.minimum(pre, 0.0) - jnp.log1p(jnp.exp(-jnp.abs(pre)))) * (1.0 / GLA_GATE_TEMP)

    row = lax.broadcasted_iota(I32, (tb, tb), 0)
    col = lax.broadcasted_iota(I32, (tb, tb), 1)
    shift = c.bit_length() - 1
    same_chunk = (row >> shift) == (col >> shift)
    chunk_ones = jnp.where(same_chunk, 1.0, 0.0)
    tri = jnp.where(col <= row, chunk_ones, 0.0).astype(BF16)
    blk = chunk_ones.astype(BF16)
    parts = _split3(log_a)
    cum = _dot(tri, parts[0]) + _dot(tri, parts[1]) + _dot(tri, parts[2])
    tot = _dot(blk, parts[0]) + _dot(blk, parts[1]) + _dot(blk, parts[2])

    kf = k_ref[...]
    qd_scr[...] = (q_ref[...] * (GLA_DK ** -0.5) * jnp.exp(cum)).astype(BF16)
    ki_scr[...] = (kf * jnp.exp(-cum)).astype(BF16)
    ke_scr[...] = (kf * jnp.exp(tot - cum)).astype(BF16)
    dec_scr[...] = jnp.exp(tot)

    causal = lax.broadcasted_iota(I32, (c, c), 1) <= lax.broadcasted_iota(I32, (c, c), 0)

    def chunk(ci, carry):
        r0 = pl.multiple_of(ci * c, c)
        rows = pl.ds(r0, c)
        for h in range(GLA_HEADS):
            ks = slice(h * GLA_DK, (h + 1) * GLA_DK)
            vs = slice(h * GLA_DV, (h + 1) * GLA_DV)
            qd = qd_scr[rows, ks]
            vv = v_ref[rows, vs].astype(BF16)
            att = jnp.where(causal, _dot_nt(qd, ki_scr[rows, ks]), 0.0).astype(BF16)
            st = st_scr[h]
            o = _dot(att, vv) + _dot_nt(qd, st.astype(BF16))
            st_scr[h] = st * dec_scr[pl.ds(r0, 1), ks] + _dot_tn(vv, ke_scr[rows, ks])
            gg = g_ref[rows, vs]
            o_ref[rows, vs] = (_rms_rows(o) * ng_ref[...] * (gg * _sigmoid(gg))).astype(o_ref.dtype)
        return carry

    lax.fori_loop(0, tb // c, chunk, 0)


def _gla_call(zg, wa, ba, ng, batch, seq, tb):
    n = zg.shape[0]
    nb = seq // tb
    row = lambda b, i: b * nb + i
    in_specs = [
        pl.BlockSpec((tb, 512), lambda b, i: (row(b, i), 0)),
        pl.BlockSpec((tb, 512), lambda b, i: (row(b, i), 1)),
        pl.BlockSpec((tb, 1024), lambda b, i: (row(b, i), 1)),
        pl.BlockSpec((tb, 1024), lambda b, i: (row(b, i), 2)),
        pl.BlockSpec((tb, LANE), lambda b, i: (row(b, i), 3072 // LANE)),
        pl.BlockSpec(wa.shape, lambda b, i: (0, 0)),
        pl.BlockSpec(ba.shape, lambda b, i: (0, 0)),
        pl.BlockSpec(ng.shape, lambda b, i: (0, 0)),
    ]
    hk = GLA_HEADS * GLA_DK
    return pl.pallas_call(
        _gla_body,
        grid=(batch, nb),
        in_specs=in_specs,
        out_specs=pl.BlockSpec((tb, GLA_HEADS * GLA_DV), lambda b, i: (row(b, i), 0)),
        out_shape=jax.ShapeDtypeStruct((n, GLA_HEADS * GLA_DV), BF16),
        scratch_shapes=[pltpu.VMEM((GLA_HEADS, GLA_DV, GLA_DK), F32),
                        pltpu.VMEM((tb, hk), BF16), pltpu.VMEM((tb, hk), BF16),
                        pltpu.VMEM((tb, hk), BF16), pltpu.VMEM((tb, hk), F32)],
        compiler_params=_params(("parallel", "arbitrary"), 48),
        name="gla",
    )(zg, zg, zg, zg, zg, wa, ba, ng)


DSA_TQ = 128
DSA_TK = 512
DSA_MAX_BISECT = 320
DSA_MIN_DENOM = 2.0 ** -100


def _dsa_body(topk, qn_ref, qi_ref, wi_ref, kn_ref, v_ref, ki_ref, o_ref,
              sc_scr, qa_scr, kmax_scr, s_scr, m_scr, l_scr, acc_scr):
    tq, tk, hd = DSA_TQ, DSA_TK, DSA_HEAD_DIM
    grp = DSA_HEADS // DSA_KV_HEADS
    kf = float(topk)
    q0 = pl.program_id(1) * tq
    nk = (q0 + tq + tk - 1) // tk
    qpos = q0 + lax.broadcasted_iota(I32, (tq, tk), 0)
    kofs = lax.broadcasted_iota(I32, (tq, tk), 1)
    inf = jnp.inf

    def chunk_start(ci):
        return pl.multiple_of(ci * tk, tk)

    @pl.when(pl.program_id(1) == 0)
    def _():
        def body(ci, mx):
            kc = kn_ref[pl.ds(chunk_start(ci), tk), :].astype(F32)
            return tuple(jnp.maximum(mx[g], jnp.sum(kc[:, g * hd:(g + 1) * hd] ** 2, axis=1, keepdims=True))
                         for g in range(DSA_KV_HEADS))

        mx = lax.fori_loop(0, kn_ref.shape[0] // tk, body,
                           tuple(jnp.zeros((tk, 1), F32) for _ in range(DSA_KV_HEADS)))
        for g in range(DSA_KV_HEADS):
            kmax_scr[g] = jnp.broadcast_to(jnp.max(mx[g], axis=0, keepdims=True), kmax_scr.shape[1:])

    wi = wi_ref[...]

    def score_chunk(ci, carry):
        rmax, rmin = carry
        k0 = chunk_start(ci)
        kc = ki_ref[pl.ds(k0, tk), :]
        score = jnp.zeros((tq, tk), F32)
        for h in range(IDX_HEADS):
            qh = qi_ref[:, h * LANE:(h + 1) * LANE]
            score = score + wi[:, h:h + 1] * jnp.maximum(_dot_nt(qh, kc), 0.0)
        valid = k0 + kofs <= qpos
        masked = jnp.where(valid, score, -inf)
        sc_scr[:, pl.ds(k0, tk)] = masked
        for t in range(tk // LANE):
            ts = slice(t * LANE, (t + 1) * LANE)
            rmax = jnp.maximum(rmax, masked[:, ts])
            rmin = jnp.minimum(rmin, jnp.where(valid[:, ts], score[:, ts], inf))
        return rmax, rmin

    rmax, rmin = lax.fori_loop(0, nk, score_chunk,
                               (jnp.full((tq, LANE), -inf, F32), jnp.full((tq, LANE), inf, F32)))
    hi0 = jnp.max(rmax, axis=1, keepdims=True)
    lo0 = jnp.min(rmin, axis=1, keepdims=True)

    @pl.when(nk % 2 == 1)
    def _():
        sc_scr[:, pl.ds(chunk_start(nk), tk)] = jnp.full((tq, tk), -inf, F32)

    def count(*tests):
        thr_b = [jnp.broadcast_to(thr, (tq, LANE)) for _, thr in tests]

        def body(ci, cnts):
            sc = sc_scr[:, pl.ds(chunk_start(ci), tk)]
            out = []
            for (pred, _), tb, cnt in zip(tests, thr_b, cnts):
                for t in range(tk // LANE):
                    cnt = cnt + jnp.where(pred(sc[:, t * LANE:(t + 1) * LANE], tb), 1.0, 0.0)
                out.append(cnt)
            return tuple(out)

        cnts = lax.fori_loop(0, nk, body, tuple(jnp.zeros((tq, LANE), F32) for _ in tests))
        return [jnp.sum(c, axis=1, keepdims=True) for c in cnts]

    ge_pred = lambda a, b: a >= b
    zero = jnp.zeros((tq, 1), F32)
    n_causal = (q0 + 1 + lax.broadcasted_iota(I32, (tq, 1), 0)).astype(F32)
    c_max, c_pos, c_nn = count((ge_pred, hi0), (lambda a, b: a > b, zero), (ge_pred, zero))
    few = n_causal <= kf
    max_tied = c_max >= kf
    zero_tied = (c_pos < kf) & (c_nn >= kf)
    from_zero = zero_tied | (c_pos >= kf)
    lo_init = jnp.where(few, lo0, jnp.where(max_tied, hi0, jnp.where(from_zero, 0.0, lo0)))
    c_init = jnp.where(few, n_causal, jnp.where(max_tied, c_max, jnp.where(from_zero, c_nn, n_causal)))
    hi_init = jnp.where(c_nn < kf, 0.0, hi0)
    fin_init = jnp.where(few | max_tied | zero_tied, 1.0, 0.0)

    def bisect_cond(carry):
        it, _, _, _, _, pending = carry
        return (it < DSA_MAX_BISECT) & (pending > 0.0)

    def bisect(carry):
        it, lo, hi, c_lo, fin, _ = carry
        pending = 1.0 - jnp.min(fin)
        mid = 0.5 * lo + 0.5 * hi
        (cnt,) = count((ge_pred, mid))
        stuck = (mid <= lo) | (mid >= hi)
        live = (fin < 1.0) & jnp.logical_not(stuck)
        ge = cnt >= kf
        up = live & ge
        lo = jnp.where(up, mid, lo)
        c_lo = jnp.where(up, cnt, c_lo)
        hi = jnp.where(live & jnp.logical_not(ge), mid, hi)
        fin = jnp.where(stuck | (cnt == kf), 1.0, fin)
        return it + 1, lo, hi, c_lo, fin, pending

    _, thr, _, c_ge, _, _ = lax.while_loop(
        bisect_cond, bisect, (jnp.int32(0), lo_init, hi_init, c_init, fin_init, 1.0 - jnp.min(fin_init)))
    thr_w = jnp.broadcast_to(thr, (tq, tk))

    @pl.when(jnp.max(c_ge) > kf)
    def _():
        (c_eq,) = count((lambda a, b: a == b, thr))
        allowed = kf - (c_ge - c_eq)
        upper = jnp.where(lax.broadcasted_iota(I32, (tk, tk), 0) <= lax.broadcasted_iota(I32, (tk, tk), 1),
                          1.0, 0.0).astype(BF16)

        def body(ci, seen):
            k0 = chunk_start(ci)
            sc = sc_scr[:, pl.ds(k0, tk)]
            eq = jnp.where(sc == thr_w, 1.0, 0.0)
            rank = seen + _dot(eq.astype(BF16), upper)
            drop = (eq > 0.0) & (rank > allowed)
            sc_scr[:, pl.ds(k0, tk)] = jnp.where(drop, -inf, sc)
            return seen + jnp.sum(eq, axis=1, keepdims=True)

        lax.fori_loop(0, nk, body, jnp.zeros((tq, 1), F32))

    lane_q = lax.broadcasted_iota(I32, (tq, hd), 1)
    for g in range(DSA_KV_HEADS):
        kmax2 = kmax_scr[g][0:1, :]
        for r in range(grp):
            hcol = (g * grp + r) * hd
            rows = slice(r * tq, (r + 1) * tq)
            qh = qn_ref[:, hcol:hcol + hd]
            qf = qh.astype(F32)
            bound = jnp.sqrt(jnp.sum(qf * qf, axis=1, keepdims=True) * kmax2)
            qa_scr[g, rows, 0:hd] = qh
            qa_scr[g, rows, hd:2 * hd] = jnp.where(lane_q == 0, bound, 0.0).astype(BF16)
    acc_scr[...] = jnp.zeros_like(acc_scr)
    lane_k = lax.broadcasted_iota(I32, (tk, hd), 1)
    k_tail = jnp.where(lane_k == 0, -1.0, 0.0).astype(BF16)
    v_tail = jnp.where(lane_k == 0, 1.0, 0.0).astype(BF16)

    def sel_bias(k0):
        return jnp.where(sc_scr[:, pl.ds(k0, tk)] >= thr_w, 0.0, -inf)

    def logits(ci, g):
        ka = jnp.concatenate([kn_ref[pl.ds(chunk_start(ci), tk), g * hd:(g + 1) * hd], k_tail], axis=1)
        return _dot_nt(qa_scr[g], ka)

    def weighted_values(ci, g, s, bias):
        va = jnp.concatenate([v_ref[pl.ds(chunk_start(ci), tk), g * hd:(g + 1) * hd], v_tail], axis=1)
        p = jnp.exp2((s.reshape(grp, tq, tk) + bias[None]).reshape(grp * tq, tk))
        return _dot(p.astype(BF16), va)

    last_chunk = kn_ref.shape[0] // tk - 1
    for g in range(DSA_KV_HEADS):
        s_scr[g] = logits(0, g)

    def attend(ji, carry):
        ca = 2 * ji
        cb = ca + 1
        bias_a = sel_bias(chunk_start(ca))
        bias_b = sel_bias(chunk_start(cb))
        for g in range(DSA_KV_HEADS):
            s_a = s_scr[g]
            s_b = logits(cb, g)
            pv_a = weighted_values(ca, g, s_a, bias_a)
            s_scr[g] = logits(jnp.minimum(ca + 2, last_chunk), g)
            acc_scr[g] += pv_a + weighted_values(cb, g, s_b, bias_b)
        return carry

    lax.fori_loop(0, (nk + 1) // 2, attend, 0)

    def write_out(g, out):
        for r in range(grp):
            hcol = (g * grp + r) * hd
            o_ref[:, hcol:hcol + hd] = out[r * tq:(r + 1) * tq, :].astype(o_ref.dtype)

    l_min = inf
    for g in range(DSA_KV_HEADS):
        acc = acc_scr[g]
        denom = acc[:, hd:hd + 1]
        l_min = jnp.minimum(l_min, jnp.min(denom))
        write_out(g, acc[:, 0:hd] / denom)

    @pl.when(jnp.logical_not(l_min >= DSA_MIN_DENOM))
    def _():
        m_scr[...] = jnp.full(m_scr.shape, MASKED_LOGIT, F32)
        l_scr[...] = jnp.zeros_like(l_scr)
        acc_scr[...] = jnp.zeros_like(acc_scr)

        def attend_online(ci, carry):
            k0 = chunk_start(ci)
            bias = sel_bias(k0)
            for g in range(DSA_KV_HEADS):
                hs = slice(g * hd, (g + 1) * hd)
                s = _dot_nt(qa_scr[g, :, 0:hd], kn_ref[pl.ds(k0, tk), hs])
                s = (s.reshape(grp, tq, tk) + bias[None]).reshape(grp * tq, tk)
                m_prev = m_scr[g]
                m_new = jnp.maximum(m_prev, jnp.max(s, axis=1, keepdims=True))
                p = jnp.exp2(s - jnp.tile(m_new, (1, tk // LANE)))
                alpha = jnp.exp2(m_prev - m_new)
                l_scr[g] = alpha * l_scr[g] + jnp.sum(p, axis=1, keepdims=True)
                acc_scr[g, :, 0:hd] = alpha * acc_scr[g, :, 0:hd] + _dot(p.astype(BF16), v_ref[pl.ds(k0, tk), hs])
                m_scr[g] = m_new
            return carry

        lax.fori_loop(0, nk, attend_online, 0)
        for g in range(DSA_KV_HEADS):
            write_out(g, acc_scr[g, :, 0:hd] / l_scr[g])


def _dsa_call(qn, qi, wi, kn, vv, ki, batch, seq):
    n = qn.shape[0]
    tq, hd = DSA_TQ, DSA_HEAD_DIM
    nb = seq // tq
    topk = min(IDX_TOPK_MAX, seq // 4)
    grp = DSA_HEADS // DSA_KV_HEADS
    row = lambda b, i: (b * nb + i, 0)
    per_batch = lambda b, i: (b, 0)
    return pl.pallas_call(
        functools.partial(_dsa_body, topk),
        grid=(batch, nb),
        in_specs=[pl.BlockSpec((tq, qn.shape[1]), row),
                  pl.BlockSpec((tq, qi.shape[1]), row),
                  pl.BlockSpec((tq, LANE), row),
                  pl.BlockSpec((seq, kn.shape[1]), per_batch),
                  pl.BlockSpec((seq, vv.shape[1]), per_batch),
                  pl.BlockSpec((seq, LANE), per_batch)],
        out_specs=pl.BlockSpec((tq, DSA_HEADS * hd), row),
        out_shape=jax.ShapeDtypeStruct((n, DSA_HEADS * hd), BF16),
        scratch_shapes=[pltpu.VMEM((tq, seq), F32),
                        pltpu.VMEM((DSA_KV_HEADS, grp * tq, 2 * hd), BF16),
                        pltpu.VMEM((DSA_KV_HEADS, 8, LANE), F32),
                        pltpu.VMEM((DSA_KV_HEADS, grp * tq, DSA_TK), F32),
                        pltpu.VMEM((DSA_KV_HEADS, grp * tq, LANE), F32),
                        pltpu.VMEM((DSA_KV_HEADS, grp * tq, LANE), F32),
                        pltpu.VMEM((DSA_KV_HEADS, grp * tq, 2 * hd), F32)],
        compiler_params=_params(("arbitrary", "arbitrary"), 48),
        name="dsa",
    )(qn, qi, wi, kn, vv, ki)


def _mem_body(q_ref, k_ref, v_ref, o_ref):
    for h in range(MEM_HEADS):
        sl = slice(h * MEM_HEAD_DIM, (h + 1) * MEM_HEAD_DIM)
        s = _dot_nt(q_ref[:, sl], k_ref[:, sl])
        p = jnp.exp(s - jnp.max(s, axis=1, keepdims=True))
        o = _dot(p.astype(BF16), v_ref[:, sl]) / jnp.sum(p, axis=1, keepdims=True)
        o_ref[:, sl] = o.astype(o_ref.dtype)


def _mem_call(qm, kv, batch, seq, mem_len, tm):
    n, d = qm.shape
    nb = seq // tm
    return pl.pallas_call(
        _mem_body,
        grid=(batch, nb),
        in_specs=[pl.BlockSpec((tm, d), lambda b, i: (b * nb + i, 0)),
                  pl.BlockSpec((mem_len, d), lambda b, i: (b, 0)),
                  pl.BlockSpec((mem_len, d), lambda b, i: (b, 1))],
        out_specs=pl.BlockSpec((tm, d), lambda b, i: (b * nb + i, 0)),
        out_shape=jax.ShapeDtypeStruct((n, d), BF16),
        compiler_params=_params(("parallel", "parallel"), 32),
        name="mem_attn",
    )(qm, kv, kv)


def _merge_body(x_ref, og_ref, od_ref, om_ref, gate_ref, wg_ref, wd_ref, wm_ref, wo_ref, o_ref):
    d = D_MODEL
    merged = (gate_ref[:, 0:d] * _dot(og_ref[...], wg_ref[...])
              + gate_ref[:, d:2 * d] * _dot(od_ref[...], wd_ref[...])
              + gate_ref[:, 2 * d:3 * d] * _dot(om_ref[...], wm_ref[...]))
    o_ref[...] = x_ref[...] + _dot(merged.astype(BF16), wo_ref[...])


def _merge_call(x2d, o_gla, o_dsa, o_mem, gate, wg, wd, wm, wo, tm):
    n, d = x2d.shape
    row = lambda i: (i, 0)
    fixed = lambda i: (0, 0)
    return pl.pallas_call(
        _merge_body,
        grid=(n // tm,),
        in_specs=[pl.BlockSpec((tm, d), row), pl.BlockSpec((tm, d), row), pl.BlockSpec((tm, d), row),
                  pl.BlockSpec((tm, d), row), pl.BlockSpec((tm, 3 * d), row),
                  pl.BlockSpec((d, d), fixed), pl.BlockSpec((d, d), fixed),
                  pl.BlockSpec((d, d), fixed), pl.BlockSpec((d, d), fixed)],
        out_specs=pl.BlockSpec((tm, d), row),
        out_shape=jax.ShapeDtypeStruct((n, d), F32),
        compiler_params=_params(("parallel",), 48),
        name="merge",
    )(x2d, o_gla, o_dsa, o_mem, gate, wg, wd, wm, wo)


def _route(z):
    lane = lax.broadcasted_iota(I32, z.shape, 1)
    lanef = lane.astype(F32)
    far = 1e9
    neg = -jnp.inf
    gmask = lane < N_GROUPS
    gmax = jnp.max(jnp.where(gmask, z, neg), axis=1, keepdims=True)
    gsum = jnp.sum(jnp.where(gmask, jnp.exp(z - gmax), 0.0), axis=1, keepdims=True)
    g_w = 1.0 / gsum
    g_sel = jnp.min(jnp.where(gmask & (z == gmax), lanef, far), axis=1, keepdims=True)
    lane_grp = ((lane - N_GROUPS) >> 2).astype(F32)
    emask = (lane >= N_GROUPS) & (lane < N_GROUPS + N_EXPERTS) & (lane_grp == g_sel)
    e1 = jnp.max(jnp.where(emask, z, neg), axis=1, keepdims=True)
    i1 = jnp.min(jnp.where(emask & (z == e1), lanef, far), axis=1, keepdims=True)
    emask2 = emask & (lanef != i1)
    e2 = jnp.max(jnp.where(emask2, z, neg), axis=1, keepdims=True)
    i2 = jnp.min(jnp.where(emask2 & (z == e2), lanef, far), axis=1, keepdims=True)
    t = jnp.exp(e2 - e1)
    p1 = 1.0 / (1.0 + t)
    p2 = t / (1.0 + t)
    return jnp.where(lanef == i1, g_w * p1, jnp.where(lanef == i2, g_w * p2, 0.0))


def _moe_body(x_ref, g_ref, wrh_ref, wrl_ref, br_ref, wg_ref, wu_ref, wd_ref, o_ref, h_scr, comb_scr):
    e = pl.program_id(1)

    @pl.when(e == 0)
    def _():
        x = x_ref[...]
        h = _rms_rows(x) * g_ref[...]
        hh = h.astype(BF16)
        hl = (h - hh.astype(F32)).astype(BF16)
        h_scr[...] = hh
        z = _dot(hh, wrh_ref[...]) + _dot(hl, wrh_ref[...]) + _dot(hh, wrl_ref[...]) + br_ref[...]
        comb_scr[...] = _route(z)
        o_ref[...] = x

    lane = lax.broadcasted_iota(I32, comb_scr.shape, 1)
    ce = jnp.sum(jnp.where(lane == e + N_GROUPS, comb_scr[...], 0.0), axis=1, keepdims=True)
    h = h_scr[...]
    gt = _dot(h, wg_ref[0])
    up = _dot(h, wu_ref[0])
    hid = gt * _sigmoid(gt) * up * ce
    o_ref[...] += _dot(hid.astype(BF16), wd_ref[0])


def _moe_call(x1, g_ffn, wr_hi, wr_lo, br, wg, wu, wd, tm):
    n, d = x1.shape
    row = lambda i, e: (i, 0)
    fixed = lambda i, e: (0, 0)
    per_e = lambda i, e: (e, 0, 0)
    return pl.pallas_call(
        _moe_body,
        grid=(n // tm, N_EXPERTS),
        in_specs=[pl.BlockSpec((tm, d), row), pl.BlockSpec((1, d), fixed),
                  pl.BlockSpec((d, LANE), fixed), pl.BlockSpec((d, LANE), fixed),
                  pl.BlockSpec((1, LANE), fixed),
                  pl.BlockSpec((1, d, D_EXPERT), per_e), pl.BlockSpec((1, d, D_EXPERT), per_e),
                  pl.BlockSpec((1, D_EXPERT, d), per_e)],
        out_specs=pl.BlockSpec((tm, d), row),
        out_shape=jax.ShapeDtypeStruct((n, d), F32),
        scratch_shapes=[pltpu.VMEM((tm, d), BF16), pltpu.VMEM((tm, LANE), F32)],
        compiler_params=_params(("parallel", "arbitrary"), 48),
        name="moe",
    )(x1, g_ffn.reshape(1, d), wr_hi, wr_lo, br, wg, wu, wd)


def _layer(x2d, mem2d, batch, seq, mem_len, g_mix, g_mem, w_in, w_gla_a2, b_gla_a2, gla_norm,
           w_mem_kv, dsa_q_norm, dsa_k_norm, idx_k_norm, mem_q_norm, mem_k_norm, b_gate,
           w_o_gla, w_o_dsa, w_o_mem, w_out, g_ffn, w_r1, b_r1, w_r2, b_r2, w_gate, w_up, w_down):
    n, d = x2d.shape
    sds = jax.ShapeDtypeStruct

    w_gla = jnp.concatenate([_cols(w_in, "gq"), _cols(w_in, "gk"), _cols(w_in, "gv"), _cols(w_in, "gg"),
                             _pad_cols(_cols(w_in, "ga"), LANE)], axis=1).astype(BF16)
    (zg,) = _proj_call("proj_gla", x2d, g_mix, w_gla, _epi_plain, [],
                       [sds((n, w_gla.shape[1]), F32)], [640], 1024, 640, 40)

    (qn,) = _proj_call("proj_dsa_q", x2d, g_mix, _cols(w_in, "dq").astype(BF16),
                       functools.partial(_epi_headnorm, DSA_HEAD_DIM, DSA_HEAD_DIM ** -0.5 * LOG2_E),
                       [dsa_q_norm.reshape(1, -1)], [sds((n, 1024), BF16)], [1024], 1024, 1024, 40)
    (qm,) = _proj_call("proj_mem_q", x2d, g_mix, _cols(w_in, "mq").astype(BF16),
                       functools.partial(_epi_headnorm, MEM_HEAD_DIM, MEM_HEAD_DIM ** -0.5),
                       [mem_q_norm.reshape(1, -1)], [sds((n, 1024), BF16)], [1024], 1024, 1024, 40)

    w_kv = jnp.concatenate([_cols(w_in, "dk"), _cols(w_in, "dv"), _pad_cols(_cols(w_in, "ik"), LANE),
                            _pad_cols(_cols(w_in, "iw"), LANE)], axis=1).astype(BF16)
    kn, vv, ki, wi = _proj_call(
        "proj_dsa_kv", x2d, g_mix, w_kv, _epi_dsa_kv,
        [dsa_k_norm.reshape(1, -1), _pad_cols(idx_k_norm.reshape(1, -1), LANE)],
        [sds((n, 256), BF16), sds((n, 256), BF16), sds((n, LANE), BF16), sds((n, LANE), F32)],
        [256, 256, LANE, LANE], 1024, 768, 40)

    w_iq = _cols(w_in, "iq").reshape(d, IDX_HEADS, IDX_DIM)
    w_iq = jnp.pad(w_iq, ((0, 0), (0, 0), (0, LANE - IDX_DIM))).reshape(d, IDX_HEADS * LANE).astype(BF16)
    (qi,) = _proj_call("proj_idx_q", x2d, g_mix, w_iq, _epi_plain, [],
                       [sds((n, IDX_HEADS * LANE), BF16)], [1024], 1024, 1024, 40)

    (gate,) = _proj_call("proj_gate", x2d, g_mix, _cols(w_in, "gates").astype(BF16), _epi_gate,
                         [b_gate.reshape(1, -1)], [sds((n, 3 * d), F32)], [1024], 1024, 1024, 40)
    wa = jnp.pad(w_gla_a2, ((0, LANE - GLA_GATE_RANK), (0, 0))).astype(BF16)
    o_gla = _gla_call(zg, wa, b_gla_a2.reshape(1, -1), gla_norm.reshape(1, -1), batch, seq, 512)

    o_dsa = _dsa_call(qn, qi, wi, kn, vv, ki, batch, seq)

    (kv,) = _proj_call("proj_mem_kv", mem2d, g_mem, w_mem_kv.astype(BF16), _epi_mem_kv,
                       [mem_k_norm.reshape(1, -1)], [sds((mem2d.shape[0], 2 * d), BF16)], [1024],
                       mem2d.shape[0], 1024, 40)
    o_mem = _mem_call(qm, kv, batch, seq, mem_len, 512)

    x1 = _merge_call(x2d, o_gla, o_dsa, o_mem, gate, w_o_gla.astype(BF16), w_o_dsa.astype(BF16),
                     w_o_mem.astype(BF16), w_out.astype(BF16), 256)

    w_r = _pad_cols(jnp.concatenate([w_r1, w_r2], axis=1), LANE)
    wr_hi = w_r.astype(BF16)
    wr_lo = (w_r - wr_hi.astype(F32)).astype(BF16)
    b_r = _pad_cols(jnp.concatenate([b_r1, b_r2]).reshape(1, -1), LANE)
    return _moe_call(x1, g_ffn, wr_hi, wr_lo, b_r, w_gate.astype(BF16), w_up.astype(BF16),
                     w_down.astype(BF16), 512)


def kernel(x, mem, g_mix, g_mem, w_in, w_gla_a2, b_gla_a2, gla_norm, w_mem_kv, dsa_q_norm, dsa_k_norm, idx_k_norm, mem_q_norm, mem_k_norm, b_gate, w_o_gla, w_o_dsa, w_o_mem, w_out, g_ffn, w_r1, b_r1, w_r2, b_r2, w_gate, w_up, w_down):
    batch, seq, d = x.shape
    mem_len = mem.shape[1]
    x2d = x.reshape(batch * seq, d)
    mem2d = mem.reshape(batch * mem_len, d)
    params = (g_mix, g_mem, w_in, w_gla_a2, b_gla_a2, gla_norm, w_mem_kv, dsa_q_norm, dsa_k_norm,
              idx_k_norm, mem_q_norm, mem_k_norm, b_gate, w_o_gla, w_o_dsa, w_o_mem, w_out, g_ffn,
              w_r1, b_r1, w_r2, b_r2, w_gate, w_up, w_down)
    for layer in range(g_mix.shape[0]):
        x2d = _layer(x2d, mem2d, batch, seq, mem_len, *(p[layer] for p in params))
    return x2d.reshape(batch, seq, d)
```

```python
import functools

import jax
import jax.numpy as jnp
from jax import lax
from jax.experimental import pallas as pl
from jax.experimental.pallas import tpu as pltpu

F32 = jnp.float32
BF16 = jnp.bfloat16
I32 = jnp.int32

D_MODEL = 1024
EPS = 1e-6

GLA_HEADS = 4
GLA_DK = 128
GLA_DV = 256
GLA_GATE_RANK = 16
GLA_GATE_TEMP = 16.0
GLA_CHUNK = 64

DSA_HEADS = 8
DSA_KV_HEADS = 2
DSA_HEAD_DIM = 128
IDX_HEADS = 8
IDX_DIM = 64
IDX_TOPK_MAX = 256

MEM_HEADS = 4
MEM_HEAD_DIM = 256

N_GROUPS = 4
EXPERTS_PER_GROUP = 4
N_EXPERTS = 16
D_EXPERT = 256

LANE = 128
MASKED_LOGIT = -1e30
LOG2_E = 1.4426950408889634

_OFF = {}
_acc = 0
for _name, _w in (("gq", 512), ("gk", 512), ("gv", 1024), ("gg", 1024), ("ga", 16), ("dq", 1024),
                  ("dk", 256), ("dv", 256), ("iq", 512), ("ik", 64), ("iw", 8), ("mq", 1024),
                  ("gates", 3072)):
    _OFF[_name] = (_acc, _acc + _w)
    _acc += _w


def _cols(w_in, name):
    lo, hi = _OFF[name]
    return w_in[:, lo:hi]


def _pad_cols(w, width):
    return jnp.pad(w, ((0, 0), (0, width - w.shape[1])))


def _rms_rows(x):
    return x * lax.rsqrt(jnp.mean(x * x, axis=-1, keepdims=True) + EPS)


def _dot(a, b):
    return jnp.dot(a, b, preferred_element_type=F32)


def _dot_nt(a, b):
    return lax.dot_general(a, b, (((1,), (1,)), ((), ())), preferred_element_type=F32)


def _dot_tn(a, b):
    return lax.dot_general(a, b, (((0,), (0,)), ((), ())), preferred_element_type=F32)


def _sigmoid(x):
    return 1.0 / (1.0 + jnp.exp(-x))


def _params(sem, vmem_mb):
    return pltpu.CompilerParams(dimension_semantics=sem, vmem_limit_bytes=vmem_mb * 1024 * 1024)


def _proj_body(epilogue, n_extra, x_ref, g_ref, w_ref, *rest):
    extra = rest[:n_extra]
    outs = rest[n_extra:-1]
    h_scr = rest[-1]

    @pl.when(pl.program_id(1) == 0)
    def _():
        h_scr[...] = (_rms_rows(x_ref[...]) * g_ref[...]).astype(BF16)

    epilogue(_dot(h_scr[...], w_ref[...]), extra, outs)


def _proj_call(name, x2d, gain, w, epilogue, extras, out_shapes, out_widths, tm, tn, vmem_mb):
    n, d = x2d.shape
    c = w.shape[1]
    in_specs = [pl.BlockSpec((tm, d), lambda i, j: (i, 0)),
                pl.BlockSpec((1, d), lambda i, j: (0, 0)),
                pl.BlockSpec((d, tn), lambda i, j: (0, j))]
    single = c == tn
    for e in extras:
        if e.shape[1] == c and not single:
            in_specs.append(pl.BlockSpec((1, tn), lambda i, j: (0, j)))
        else:
            in_specs.append(pl.BlockSpec(e.shape, lambda i, j: (0, 0)))
    out_specs = [pl.BlockSpec((tm, ow), (lambda i, j: (i, 0)) if single else (lambda i, j: (i, j)))
                 for ow in out_widths]
    return pl.pallas_call(
        functools.partial(_proj_body, epilogue, len(extras)),
        grid=(n // tm, c // tn),
        in_specs=in_specs,
        out_specs=out_specs,
        out_shape=out_shapes,
        scratch_shapes=[pltpu.VMEM((tm, d), BF16)],
        compiler_params=_params(("parallel", "arbitrary"), vmem_mb),
        name=name,
    )(x2d, gain.reshape(1, d), w, *extras)


def _epi_plain(acc, extra, outs):
    outs[0][...] = acc.astype(outs[0].dtype)


def _epi_headnorm(head_dim, scale, acc, extra, outs):
    gain = extra[0][...]
    for h in range(acc.shape[1] // head_dim):
        sl = slice(h * head_dim, (h + 1) * head_dim)
        outs[0][:, sl] = (_rms_rows(acc[:, sl]) * gain * scale).astype(outs[0].dtype)


def _epi_dsa_kv(acc, extra, outs):
    k_gain, idx_gain = extra
    kn_ref, v_ref, ki_ref, wi_ref = outs
    for h in range(DSA_KV_HEADS):
        sl = slice(h * DSA_HEAD_DIM, (h + 1) * DSA_HEAD_DIM)
        kn_ref[:, sl] = (_rms_rows(acc[:, sl]) * k_gain[...]).astype(BF16)
    v_ref[...] = acc[:, 256:512].astype(BF16)
    a = acc[:, 512:640]
    ms = jnp.sum(a * a, axis=-1, keepdims=True) * (1.0 / IDX_DIM)
    ki_ref[...] = (a * lax.rsqrt(ms + EPS) * idx_gain[...]).astype(BF16)
    wi_ref[...] = acc[:, 640:768] * (IDX_HEADS ** -0.5 * IDX_DIM ** -0.5)


def _epi_gate(acc, extra, outs):
    outs[0][...] = _sigmoid(acc + extra[0][...])


def _epi_mem_kv(acc, extra, outs):
    gain = extra[0][...]
    j = pl.program_id(1)

    @pl.when(j == 0)
    def _():
        for h in range(MEM_HEADS):
            sl = slice(h * MEM_HEAD_DIM, (h + 1) * MEM_HEAD_DIM)
            outs[0][:, sl] = (_rms_rows(acc[:, sl]) * gain).astype(BF16)

    @pl.when(j == 1)
    def _():
        outs[0][...] = acc.astype(BF16)


def _split3(x):
    hi = x.astype(BF16)
    r1 = x - hi.astype(F32)
    mid = r1.astype(BF16)
    lo = (r1 - mid.astype(F32)).astype(BF16)
    return hi, mid, lo


def _gla_body(q_ref, k_ref, v_ref, g_ref, a_ref, wa_ref, ba_ref, ng_ref, o_ref,
              st_scr, qd_scr, ki_scr, ke_scr, dec_scr):
    tb = q_ref.shape[0]
    c = GLA_CHUNK

    @pl.when(pl.program_id(1) == 0)
    def _():
        st_scr[...] = jnp.zeros_like(st_scr)

    pre = _dot(a_ref[...].astype(BF16), wa_ref[...]) + ba_ref[...]
    log_a = (jnp.minimum(pre, 0.0) - jnp.log1p(jnp.exp(-jnp.abs(pre)))) * (1.0 / GLA_GATE_TEMP)

    row = lax.broadcasted_iota(I32, (tb, tb), 0)
    col = lax.broadcasted_iota(I32, (tb, tb), 1)
    shift = c.bit_length() - 1
    same_chunk = (row >> shift) == (col >> shift)
    chunk_ones = jnp.where(same_chunk, 1.0, 0.0)
    tri = jnp.where(col <= row, chunk_ones, 0.0).astype(BF16)
    blk = chunk_ones.astype(BF16)
    parts = _split3(log_a)
    cum = _dot(tri, parts[0]) + _dot(tri, parts[1]) + _dot(tri, parts[2])
    tot = _dot(blk, parts[0]) + _dot(blk, parts[1]) + _dot(blk, parts[2])

    kf = k_ref[...]
    qd_scr[...] = (q_ref[...] * (GLA_DK ** -0.5) * jnp.exp(cum)).astype(BF16)
    ki_scr[...] = (kf * jnp.exp(-cum)).astype(BF16)
    ke_scr[...] = (kf * jnp.exp(tot - cum)).astype(BF16)
    dec_scr[...] = jnp.exp(tot)

    causal = lax.broadcasted_iota(I32, (c, c), 1) <= lax.broadcasted_iota(I32, (c, c), 0)

    def chunk(ci, carry):
        r0 = pl.multiple_of(ci * c, c)
        rows = pl.ds(r0, c)
        for h in range(GLA_HEADS):
            ks = slice(h * GLA_DK, (h + 1) * GLA_DK)
            vs = slice(h * GLA_DV, (h + 1) * GLA_DV)
            qd = qd_scr[rows, ks]
            vv = v_ref[rows, vs].astype(BF16)
            att = jnp.where(causal, _dot_nt(qd, ki_scr[rows, ks]), 0.0).astype(BF16)
            st = st_scr[h]
            o = _dot(att, vv) + _dot_nt(qd, st.astype(BF16))
            st_scr[h] = st * dec_scr[pl.ds(r0, 1), ks] + _dot_tn(vv, ke_scr[rows, ks])
            gg = g_ref[rows, vs]
            o_ref[rows, vs] = (_rms_rows(o) * ng_ref[...] * (gg * _sigmoid(gg))).astype(o_ref.dtype)
        return carry

    lax.fori_loop(0, tb // c, chunk, 0)


def _gla_call(zg, wa, ba, ng, batch, seq, tb):
    n = zg.shape[0]
    nb = seq // tb
    row = lambda b, i: b * nb + i
    in_specs = [
        pl.BlockSpec((tb, 512), lambda b, i: (row(b, i), 0)),
        pl.BlockSpec((tb, 512), lambda b, i: (row(b, i), 1)),
        pl.BlockSpec((tb, 1024), lambda b, i: (row(b, i), 1)),
        pl.BlockSpec((tb, 1024), lambda b, i: (row(b, i), 2)),
        pl.BlockSpec((tb, LANE), lambda b, i: (row(b, i), 3072 // LANE)),
        pl.BlockSpec(wa.shape, lambda b, i: (0, 0)),
        pl.BlockSpec(ba.shape, lambda b, i: (0, 0)),
        pl.BlockSpec(ng.shape, lambda b, i: (0, 0)),
    ]
    hk = GLA_HEADS * GLA_DK
    return pl.pallas_call(
        _gla_body,
        grid=(batch, nb),
        in_specs=in_specs,
        out_specs=pl.BlockSpec((tb, GLA_HEADS * GLA_DV), lambda b, i: (row(b, i), 0)),
        out_shape=jax.ShapeDtypeStruct((n, GLA_HEADS * GLA_DV), BF16),
        scratch_shapes=[pltpu.VMEM((GLA_HEADS, GLA_DV, GLA_DK), F32),
                        pltpu.VMEM((tb, hk), BF16), pltpu.VMEM((tb, hk), BF16),
                        pltpu.VMEM((tb, hk), BF16), pltpu.VMEM((tb, hk), F32)],
        compiler_params=_params(("parallel", "arbitrary"), 48),
        name="gla",
    )(zg, zg, zg, zg, zg, wa, ba, ng)


DSA_TQ = 256
DSA_TK = 512
DSA_COUNT_ROWS = 128
DSA_MAX_BISECT = 320
DSA_MIN_DENOM = 2.0 ** -100


def _dsa_body(topk, qn_ref, qi_ref, wi_ref, kn_ref, v_ref, ki_ref, o_ref,
              sc_scr, qa_scr, kmax_scr, s_scr, m_scr, l_scr, acc_scr):
    tq, tk, hd = DSA_TQ, DSA_TK, DSA_HEAD_DIM
    grp = DSA_HEADS // DSA_KV_HEADS
    kf = float(topk)
    q0 = pl.program_id(1) * tq
    nk = (q0 + tq + tk - 1) // tk
    qpos = q0 + lax.broadcasted_iota(I32, (tq, tk), 0)
    kofs = lax.broadcasted_iota(I32, (tq, tk), 1)
    inf = jnp.inf

    def chunk_start(ci):
        return pl.multiple_of(ci * tk, tk)

    @pl.when(pl.program_id(1) == 0)
    def _():
        def body(ci, mx):
            kc = kn_ref[pl.ds(chunk_start(ci), tk), :].astype(F32)
            return tuple(jnp.maximum(mx[g], jnp.sum(kc[:, g * hd:(g + 1) * hd] ** 2, axis=1, keepdims=True))
                         for g in range(DSA_KV_HEADS))

        mx = lax.fori_loop(0, kn_ref.shape[0] // tk, body,
                           tuple(jnp.zeros((tk, 1), F32) for _ in range(DSA_KV_HEADS)))
        for g in range(DSA_KV_HEADS):
            kmax_scr[g] = jnp.broadcast_to(jnp.max(mx[g], axis=0, keepdims=True), kmax_scr.shape[1:])

    wi = wi_ref[...]

    def score_chunk(ci, carry):
        rmax, rmin, n_pos, n_nonneg = carry
        k0 = chunk_start(ci)
        kc = ki_ref[pl.ds(k0, tk), :]
        score = jnp.zeros((tq, tk), F32)
        for h in range(IDX_HEADS):
            qh = qi_ref[:, h * LANE:(h + 1) * LANE]
            score = score + wi[:, h:h + 1] * jnp.maximum(_dot_nt(qh, kc), 0.0)
        valid = k0 + kofs <= qpos
        masked = jnp.where(valid, score, -inf)
        sc_scr[:, pl.ds(k0, tk)] = masked
        for t in range(tk // LANE):
            ts = slice(t * LANE, (t + 1) * LANE)
            rmax = jnp.maximum(rmax, masked[:, ts])
            rmin = jnp.minimum(rmin, jnp.where(valid[:, ts], score[:, ts], inf))
            n_pos = n_pos + jnp.where(masked[:, ts] > 0.0, 1.0, 0.0)
            n_nonneg = n_nonneg + jnp.where(masked[:, ts] >= 0.0, 1.0, 0.0)
        return rmax, rmin, n_pos, n_nonneg

    n_pairs = (nk + 1) // 2
    rmax, rmin, n_pos, n_nonneg = lax.fori_loop(
        0, n_pairs, lambda ji, c: score_chunk(2 * ji + 1, score_chunk(2 * ji, c)),
        (jnp.full((tq, LANE), -inf, F32), jnp.full((tq, LANE), inf, F32),
         jnp.zeros((tq, LANE), F32), jnp.zeros((tq, LANE), F32)))
    hi0 = jnp.broadcast_to(jnp.max(rmax, axis=1, keepdims=True), (tq, LANE))
    lo0 = jnp.broadcast_to(jnp.min(rmin, axis=1, keepdims=True), (tq, LANE))
    ones_sum = jnp.ones((LANE, LANE), BF16)
    c_pos = _dot(n_pos.astype(BF16), ones_sum)
    c_nn = _dot(n_nonneg.astype(BF16), ones_sum)

    def count(*tests):
        partial = []
        for r0 in range(0, tq, DSA_COUNT_ROWS):
            rows = slice(r0, r0 + DSA_COUNT_ROWS)
            thr_b = [thr[rows] for _, thr in tests]

            def body(ci, cnts):
                sc = sc_scr[rows, pl.ds(chunk_start(ci), tk)]
                out = []
                for (pred, _), tb, cnt in zip(tests, thr_b, cnts):
                    for t in range(tk // LANE):
                        cnt = cnt + jnp.where(pred(sc[:, t * LANE:(t + 1) * LANE], tb), 1.0, 0.0)
                    out.append(cnt)
                return tuple(out)

            partial.append(lax.fori_loop(0, nk, body,
                                         tuple(jnp.zeros((DSA_COUNT_ROWS, LANE), F32) for _ in tests)))
        return [jnp.concatenate([_dot(p[k].astype(BF16), ones_sum) for p in partial], axis=0)
                for k in range(len(tests))]

    ge_pred = lambda a, b: a >= b
    n_causal = (q0 + 1 + lax.broadcasted_iota(I32, (tq, LANE), 0)).astype(F32)
    few = n_causal <= kf
    zero_tied = (c_pos < kf) & (c_nn >= kf)
    lo_init = jnp.where(few, lo0, jnp.where(c_nn >= kf, 0.0, lo0))
    hi_init = jnp.where(few | zero_tied, lo_init, jnp.where(c_nn < kf, 0.0, hi0))

    def bisect_cond(carry):
        it, _, _, pending = carry
        return (it < DSA_MAX_BISECT) & (pending > 0.0)

    def bisect(carry):
        it, lo, hi, _ = carry
        mid = 0.5 * lo + 0.5 * hi
        (cnt,) = count((ge_pred, mid))
        pending = jnp.max(jnp.where(mid > lo, jnp.where(mid < hi, 1.0, 0.0), 0.0))
        lo = jnp.where(cnt >= kf, mid, lo)
        hi = jnp.where(cnt > kf, hi, mid)
        return it + 1, lo, hi, pending

    _, thr, _, _ = lax.while_loop(bisect_cond, bisect, (jnp.int32(0), lo_init, hi_init, jnp.float32(1.0)))
    thr_w = jnp.tile(thr, (1, tk // LANE))
    (c_ge,) = count((ge_pred, thr))

    @pl.when(jnp.max(c_ge) > kf)
    def _():
        def low_body(ci, low):
            sc = sc_scr[:, pl.ds(chunk_start(ci), tk)]
            return jnp.minimum(low, jnp.min(jnp.where(sc >= thr_w, sc, inf), axis=1, keepdims=True))

        tied = jnp.broadcast_to(lax.fori_loop(0, nk, low_body, jnp.full((tq, 1), inf, F32)), (tq, LANE))
        tied_w = jnp.tile(tied, (1, tk // LANE))
        (c_eq,) = count((lambda a, b: a == b, tied))
        allowed = jnp.tile(kf - (c_ge - c_eq), (1, tk // LANE))
        upper = jnp.where(lax.broadcasted_iota(I32, (tk, tk), 0) <= lax.broadcasted_iota(I32, (tk, tk), 1),
                          1.0, 0.0).astype(BF16)

        def body(ci, seen):
            k0 = chunk_start(ci)
            sc = sc_scr[:, pl.ds(k0, tk)]
            eq = jnp.where(sc == tied_w, 1.0, 0.0).astype(BF16)
            rank = seen + _dot(eq, upper)
            drop = jnp.where(sc == tied_w, jnp.where(rank > allowed, 1.0, 0.0), 0.0) > 0.0
            sc_scr[:, pl.ds(k0, tk)] = jnp.where(drop, -inf, sc)
            return jnp.broadcast_to(rank[:, tk - 1:tk], (tq, tk))

        lax.fori_loop(0, nk, body, jnp.zeros((tq, tk), F32))

    lane_q = lax.broadcasted_iota(I32, (tq, hd), 1)
    for g in range(DSA_KV_HEADS):
        kmax2 = kmax_scr[g][0:1, :]
        for r in range(grp):
            hcol = (g * grp + r) * hd
            rows = slice(r * tq, (r + 1) * tq)
            qh = qn_ref[:, hcol:hcol + hd]
            qf = qh.astype(F32)
            bound = jnp.sqrt(jnp.sum(qf * qf, axis=1, keepdims=True) * kmax2)
            qa_scr[g, rows, 0:hd] = qh
            qa_scr[g, rows, hd:2 * hd] = jnp.where(lane_q == 0, bound, 0.0).astype(BF16)
    acc_scr[...] = jnp.zeros_like(acc_scr)
    lane_k = lax.broadcasted_iota(I32, (tk, hd), 1)
    k_tail = jnp.where(lane_k == 0, -1.0, 0.0).astype(BF16)
    v_tail = jnp.where(lane_k == 0, 1.0, 0.0).astype(BF16)

    def sel_bias(k0):
        return jnp.where(sc_scr[:, pl.ds(k0, tk)] >= thr_w, 0.0, -inf)

    def logits(ci, g):
        ka = jnp.concatenate([kn_ref[pl.ds(chunk_start(ci), tk), g * hd:(g + 1) * hd], k_tail], axis=1)
        return _dot_nt(qa_scr[g], ka)

    def weighted_values(ci, g, s, bias):
        va = jnp.concatenate([v_ref[pl.ds(chunk_start(ci), tk), g * hd:(g + 1) * hd], v_tail], axis=1)
        p = jnp.exp2((s.reshape(grp, tq, tk) + bias[None]).reshape(grp * tq, tk))
        return _dot(p.astype(BF16), va)

    last_chunk = kn_ref.shape[0] // tk - 1
    for g in range(DSA_KV_HEADS):
        s_scr[g] = logits(0, g)

    def attend(ji, carry):
        ca = 2 * ji
        cb = ca + 1
        bias_a = sel_bias(chunk_start(ca))
        bias_b = sel_bias(chunk_start(cb))
        for g in range(DSA_KV_HEADS):
            s_a = s_scr[g]
            s_b = logits(cb, g)
            pv_a = weighted_values(ca, g, s_a, bias_a)
            s_scr[g] = logits(jnp.minimum(ca + 2, last_chunk), g)
            acc_scr[g] += pv_a + weighted_values(cb, g, s_b, bias_b)
        return carry

    lax.fori_loop(0, n_pairs, attend, 0)

    def write_out(g, out):
        for r in range(grp):
            hcol = (g * grp + r) * hd
            o_ref[:, hcol:hcol + hd] = out[r * tq:(r + 1) * tq, :].astype(o_ref.dtype)

    l_min = inf
    for g in range(DSA_KV_HEADS):
        acc = acc_scr[g]
        denom = acc[:, hd:hd + 1]
        l_min = jnp.minimum(l_min, jnp.min(denom))
        write_out(g, acc[:, 0:hd] / denom)

    @pl.when(jnp.logical_not(l_min >= DSA_MIN_DENOM))
    def _():
        m_scr[...] = jnp.full(m_scr.shape, MASKED_LOGIT, F32)
        l_scr[...] = jnp.zeros_like(l_scr)
        acc_scr[...] = jnp.zeros_like(acc_scr)

        def attend_online(ci, carry):
            k0 = chunk_start(ci)
            bias = sel_bias(k0)
            for g in range(DSA_KV_HEADS):
                hs = slice(g * hd, (g + 1) * hd)
                s = _dot_nt(qa_scr[g, :, 0:hd], kn_ref[pl.ds(k0, tk), hs])
                s = (s.reshape(grp, tq, tk) + bias[None]).reshape(grp * tq, tk)
                m_prev = m_scr[g]
                m_new = jnp.maximum(m_prev, jnp.max(s, axis=1, keepdims=True))
                p = jnp.exp2(s - jnp.tile(m_new, (1, tk // LANE)))
                alpha = jnp.exp2(m_prev - m_new)
                l_scr[g] = alpha * l_scr[g] + jnp.sum(p, axis=1, keepdims=True)
                acc_scr[g, :, 0:hd] = alpha * acc_scr[g, :, 0:hd] + _dot(p.astype(BF16), v_ref[pl.ds(k0, tk), hs])
                m_scr[g] = m_new
            return carry

        lax.fori_loop(0, nk, attend_online, 0)
        for g in range(DSA_KV_HEADS):
            write_out(g, acc_scr[g, :, 0:hd] / l_scr[g])


def _dsa_call(qn, qi, wi, kn, vv, ki, batch, seq):
    n = qn.shape[0]
    tq, hd = DSA_TQ, DSA_HEAD_DIM
    nb = seq // tq
    topk = min(IDX_TOPK_MAX, seq // 4)
    grp = DSA_HEADS // DSA_KV_HEADS
    assert seq // LANE <= 256, "per-lane partial counts must stay exact in bf16"
    row = lambda b, i: (b * nb + i, 0)
    per_batch = lambda b, i: (b, 0)
    return pl.pallas_call(
        functools.partial(_dsa_body, topk),
        grid=(batch, nb),
        in_specs=[pl.BlockSpec((tq, qn.shape[1]), row),
                  pl.BlockSpec((tq, qi.shape[1]), row),
                  pl.BlockSpec((tq, LANE), row),
                  pl.BlockSpec((seq, kn.shape[1]), per_batch),
                  pl.BlockSpec((seq, vv.shape[1]), per_batch),
                  pl.BlockSpec((seq, LANE), per_batch)],
        out_specs=pl.BlockSpec((tq, DSA_HEADS * hd), row),
        out_shape=jax.ShapeDtypeStruct((n, DSA_HEADS * hd), BF16),
        scratch_shapes=[pltpu.VMEM((tq, seq), F32),
                        pltpu.VMEM((DSA_KV_HEADS, grp * tq, 2 * hd), BF16),
                        pltpu.VMEM((DSA_KV_HEADS, 8, LANE), F32),
                        pltpu.VMEM((DSA_KV_HEADS, grp * tq, DSA_TK), F32),
                        pltpu.VMEM((DSA_KV_HEADS, grp * tq, LANE), F32),
                        pltpu.VMEM((DSA_KV_HEADS, grp * tq, LANE), F32),
                        pltpu.VMEM((DSA_KV_HEADS, grp * tq, 2 * hd), F32)],
        compiler_params=_params(("arbitrary", "arbitrary"), 48),
        name="dsa",
    )(qn, qi, wi, kn, vv, ki)


def _mem_body(q_ref, k_ref, v_ref, o_ref):
    for h in range(MEM_HEADS):
        sl = slice(h * MEM_HEAD_DIM, (h + 1) * MEM_HEAD_DIM)
        s = _dot_nt(q_ref[:, sl], k_ref[:, sl])
        p = jnp.exp(s - jnp.max(s, axis=1, keepdims=True))
        o = _dot(p.astype(BF16), v_ref[:, sl]) / jnp.sum(p, axis=1, keepdims=True)
        o_ref[:, sl] = o.astype(o_ref.dtype)


def _mem_call(qm, kv, batch, seq, mem_len, tm):
    n, d = qm.shape
    nb = seq // tm
    return pl.pallas_call(
        _mem_body,
        grid=(batch, nb),
        in_specs=[pl.BlockSpec((tm, d), lambda b, i: (b * nb + i, 0)),
                  pl.BlockSpec((mem_len, d), lambda b, i: (b, 0)),
                  pl.BlockSpec((mem_len, d), lambda b, i: (b, 1))],
        out_specs=pl.BlockSpec((tm, d), lambda b, i: (b * nb + i, 0)),
        out_shape=jax.ShapeDtypeStruct((n, d), BF16),
        compiler_params=_params(("parallel", "parallel"), 32),
        name="mem_attn",
    )(qm, kv, kv)


def _merge_body(x_ref, og_ref, od_ref, om_ref, gate_ref, wg_ref, wd_ref, wm_ref, wo_ref, o_ref):
    d = D_MODEL
    merged = (gate_ref[:, 0:d] * _dot(og_ref[...], wg_ref[...])
              + gate_ref[:, d:2 * d] * _dot(od_ref[...], wd_ref[...])
              + gate_ref[:, 2 * d:3 * d] * _dot(om_ref[...], wm_ref[...]))
    o_ref[...] = x_ref[...] + _dot(merged.astype(BF16), wo_ref[...])


def _merge_call(x2d, o_gla, o_dsa, o_mem, gate, wg, wd, wm, wo, tm):
    n, d = x2d.shape
    row = lambda i: (i, 0)
    fixed = lambda i: (0, 0)
    return pl.pallas_call(
        _merge_body,
        grid=(n // tm,),
        in_specs=[pl.BlockSpec((tm, d), row), pl.BlockSpec((tm, d), row), pl.BlockSpec((tm, d), row),
                  pl.BlockSpec((tm, d), row), pl.BlockSpec((tm, 3 * d), row),
                  pl.BlockSpec((d, d), fixed), pl.BlockSpec((d, d), fixed),
                  pl.BlockSpec((d, d), fixed), pl.BlockSpec((d, d), fixed)],
        out_specs=pl.BlockSpec((tm, d), row),
        out_shape=jax.ShapeDtypeStruct((n, d), F32),
        compiler_params=_params(("parallel",), 48),
        name="merge",
    )(x2d, o_gla, o_dsa, o_mem, gate, wg, wd, wm, wo)


def _route(z):
    lane = lax.broadcasted_iota(I32, z.shape, 1)
    lanef = lane.astype(F32)
    far = 1e9
    neg = -jnp.inf
    gmask = lane < N_GROUPS
    gmax = jnp.max(jnp.where(gmask, z, neg), axis=1, keepdims=True)
    gsum = jnp.sum(jnp.where(gmask, jnp.exp(z - gmax), 0.0), axis=1, keepdims=True)
    g_w = 1.0 / gsum
    g_sel = jnp.min(jnp.where(gmask & (z == gmax), lanef, far), axis=1, keepdims=True)
    lane_grp = ((lane - N_GROUPS) >> 2).astype(F32)
    emask = (lane >= N_GROUPS) & (lane < N_GROUPS + N_EXPERTS) & (lane_grp == g_sel)
    e1 = jnp.max(jnp.where(emask, z, neg), axis=1, keepdims=True)
    i1 = jnp.min(jnp.where(emask & (z == e1), lanef, far), axis=1, keepdims=True)
    emask2 = emask & (lanef != i1)
    e2 = jnp.max(jnp.where(emask2, z, neg), axis=1, keepdims=True)
    i2 = jnp.min(jnp.where(emask2 & (z == e2), lanef, far), axis=1, keepdims=True)
    t = jnp.exp(e2 - e1)
    p1 = 1.0 / (1.0 + t)
    p2 = t / (1.0 + t)
    return jnp.where(lanef == i1, g_w * p1, jnp.where(lanef == i2, g_w * p2, 0.0))


def _moe_body(x_ref, g_ref, wrh_ref, wrl_ref, br_ref, wg_ref, wu_ref, wd_ref, o_ref, h_scr, comb_scr):
    e = pl.program_id(1)

    @pl.when(e == 0)
    def _():
        x = x_ref[...]
        h = _rms_rows(x) * g_ref[...]
        hh = h.astype(BF16)
        hl = (h - hh.astype(F32)).astype(BF16)
        h_scr[...] = hh
        z = _dot(hh, wrh_ref[...]) + _dot(hl, wrh_ref[...]) + _dot(hh, wrl_ref[...]) + br_ref[...]
        comb_scr[...] = _route(z)
        o_ref[...] = x

    lane = lax.broadcasted_iota(I32, comb_scr.shape, 1)
    ce = jnp.sum(jnp.where(lane == e + N_GROUPS, comb_scr[...], 0.0), axis=1, keepdims=True)
    h = h_scr[...]
    gt = _dot(h, wg_ref[0])
    up = _dot(h, wu_ref[0])
    hid = gt * _sigmoid(gt) * up * ce
    o_ref[...] += _dot(hid.astype(BF16), wd_ref[0])


def _moe_call(x1, g_ffn, wr_hi, wr_lo, br, wg, wu, wd, tm):
    n, d = x1.shape
    row = lambda i, e: (i, 0)
    fixed = lambda i, e: (0, 0)
    per_e = lambda i, e: (e, 0, 0)
    return pl.pallas_call(
        _moe_body,
        grid=(n // tm, N_EXPERTS),
        in_specs=[pl.BlockSpec((tm, d), row), pl.BlockSpec((1, d), fixed),
                  pl.BlockSpec((d, LANE), fixed), pl.BlockSpec((d, LANE), fixed),
                  pl.BlockSpec((1, LANE), fixed),
                  pl.BlockSpec((1, d, D_EXPERT), per_e), pl.BlockSpec((1, d, D_EXPERT), per_e),
                  pl.BlockSpec((1, D_EXPERT, d), per_e)],
        out_specs=pl.BlockSpec((tm, d), row),
        out_shape=jax.ShapeDtypeStruct((n, d), F32),
        scratch_shapes=[pltpu.VMEM((tm, d), BF16), pltpu.VMEM((tm, LANE), F32)],
        compiler_params=_params(("parallel", "arbitrary"), 48),
        name="moe",
    )(x1, g_ffn.reshape(1, d), wr_hi, wr_lo, br, wg, wu, wd)


def _layer(x2d, mem2d, batch, seq, mem_len, g_mix, g_mem, w_in, w_gla_a2, b_gla_a2, gla_norm,
           w_mem_kv, dsa_q_norm, dsa_k_norm, idx_k_norm, mem_q_norm, mem_k_norm, b_gate,
           w_o_gla, w_o_dsa, w_o_mem, w_out, g_ffn, w_r1, b_r1, w_r2, b_r2, w_gate, w_up, w_down):
    n, d = x2d.shape
    sds = jax.ShapeDtypeStruct

    w_gla = jnp.concatenate([_cols(w_in, "gq"), _cols(w_in, "gk"), _cols(w_in, "gv"), _cols(w_in, "gg"),
                             _pad_cols(_cols(w_in, "ga"), LANE)], axis=1).astype(BF16)
    (zg,) = _proj_call("proj_gla", x2d, g_mix, w_gla, _epi_plain, [],
                       [sds((n, w_gla.shape[1]), F32)], [640], 1024, 640, 40)

    (qn,) = _proj_call("proj_dsa_q", x2d, g_mix, _cols(w_in, "dq").astype(BF16),
                       functools.partial(_epi_headnorm, DSA_HEAD_DIM, DSA_HEAD_DIM ** -0.5 * LOG2_E),
                       [dsa_q_norm.reshape(1, -1)], [sds((n, 1024), BF16)], [1024], 1024, 1024, 40)
    (qm,) = _proj_call("proj_mem_q", x2d, g_mix, _cols(w_in, "mq").astype(BF16),
                       functools.partial(_epi_headnorm, MEM_HEAD_DIM, MEM_HEAD_DIM ** -0.5),
                       [mem_q_norm.reshape(1, -1)], [sds((n, 1024), BF16)], [1024], 1024, 1024, 40)

    w_kv = jnp.concatenate([_cols(w_in, "dk"), _cols(w_in, "dv"), _pad_cols(_cols(w_in, "ik"), LANE),
                            _pad_cols(_cols(w_in, "iw"), LANE)], axis=1).astype(BF16)
    kn, vv, ki, wi = _proj_call(
        "proj_dsa_kv", x2d, g_mix, w_kv, _epi_dsa_kv,
        [dsa_k_norm.reshape(1, -1), _pad_cols(idx_k_norm.reshape(1, -1), LANE)],
        [sds((n, 256), BF16), sds((n, 256), BF16), sds((n, LANE), BF16), sds((n, LANE), F32)],
        [256, 256, LANE, LANE], 1024, 768, 40)

    w_iq = _cols(w_in, "iq").reshape(d, IDX_HEADS, IDX_DIM)
    w_iq = jnp.pad(w_iq, ((0, 0), (0, 0), (0, LANE - IDX_DIM))).reshape(d, IDX_HEADS * LANE).astype(BF16)
    (qi,) = _proj_call("proj_idx_q", x2d, g_mix, w_iq, _epi_plain, [],
                       [sds((n, IDX_HEADS * LANE), BF16)], [1024], 1024, 1024, 40)

    (gate,) = _proj_call("proj_gate", x2d, g_mix, _cols(w_in, "gates").astype(BF16), _epi_gate,
                         [b_gate.reshape(1, -1)], [sds((n, 3 * d), F32)], [1024], 1024, 1024, 40)
    wa = jnp.pad(w_gla_a2, ((0, LANE - GLA_GATE_RANK), (0, 0))).astype(BF16)
    o_gla = _gla_call(zg, wa, b_gla_a2.reshape(1, -1), gla_norm.reshape(1, -1), batch, seq, 512)

    o_dsa = _dsa_call(qn, qi, wi, kn, vv, ki, batch, seq)

    (kv,) = _proj_call("proj_mem_kv", mem2d, g_mem, w_mem_kv.astype(BF16), _epi_mem_kv,
                       [mem_k_norm.reshape(1, -1)], [sds((mem2d.shape[0], 2 * d), BF16)], [1024],
                       mem2d.shape[0], 1024, 40)
    o_mem = _mem_call(qm, kv, batch, seq, mem_len, 512)

    x1 = _merge_call(x2d, o_gla, o_dsa, o_mem, gate, w_o_gla.astype(BF16), w_o_dsa.astype(BF16),
                     w_o_mem.astype(BF16), w_out.astype(BF16), 256)

    w_r = _pad_cols(jnp.concatenate([w_r1, w_r2], axis=1), LANE)
    wr_hi = w_r.astype(BF16)
    wr_lo = (w_r - wr_hi.astype(F32)).astype(BF16)
    b_r = _pad_cols(jnp.concatenate([b_r1, b_r2]).reshape(1, -1), LANE)
    return _moe_call(x1, g_ffn, wr_hi, wr_lo, b_r, w_gate.astype(BF16), w_up.astype(BF16),
                     w_down.astype(BF16), 1024)


def kernel(x, mem, g_mix, g_mem, w_in, w_gla_a2, b_gla_a2, gla_norm, w_mem_kv, dsa_q_norm, dsa_k_norm, idx_k_norm, mem_q_norm, mem_k_norm, b_gate, w_o_gla, w_o_dsa, w_o_mem, w_out, g_ffn, w_r1, b_r1, w_r2, b_r2, w_gate, w_up, w_down):
    batch, seq, d = x.shape
    mem_len = mem.shape[1]
    x2d = x.reshape(batch * seq, d)
    mem2d = mem.reshape(batch * mem_len, d)
    params = (g_mix, g_mem, w_in, w_gla_a2, b_gla_a2, gla_norm, w_mem_kv, dsa_q_norm, dsa_k_norm,
              idx_k_norm, mem_q_norm, mem_k_norm, b_gate, w_o_gla, w_o_dsa, w_o_mem, w_out, g_ffn,
              w_r1, b_r1, w_r2, b_r2, w_gate, w_up, w_down)
    for layer in range(g_mix.shape[0]):
        x2d = _layer(x2d, mem2d, batch, seq, mem_len, *(p[layer] for p in params))
    return x2d.reshape(batch, seq, d)
```

```python
import functools

import jax
import jax.numpy as jnp
from jax import lax
from jax.experimental import pallas as pl
from jax.experimental.pallas import tpu as pltpu

F32 = jnp.float32
BF16 = jnp.bfloat16
I32 = jnp.int32

D_MODEL = 1024
EPS = 1e-6

GLA_HEADS = 4
GLA_DK = 128
GLA_DV = 256
GLA_GATE_RANK = 16
GLA_GATE_TEMP = 16.0
GLA_CHUNK = 64
GLA_UNROLL = 8

DSA_HEADS = 8
DSA_KV_HEADS = 2
DSA_HEAD_DIM = 128
IDX_HEADS = 8
IDX_DIM = 64
IDX_TOPK_MAX = 256

MEM_HEADS = 4
MEM_HEAD_DIM = 256

N_GROUPS = 4
EXPERTS_PER_GROUP = 4
N_EXPERTS = 16
D_EXPERT = 256

LANE = 128
MASKED_LOGIT = -1e30
LOG2_E = 1.4426950408889634

_OFF = {}
_acc = 0
for _name, _w in (("gq", 512), ("gk", 512), ("gv", 1024), ("gg", 1024), ("ga", 16), ("dq", 1024),
                  ("dk", 256), ("dv", 256), ("iq", 512), ("ik", 64), ("iw", 8), ("mq", 1024),
                  ("gates", 3072)):
    _OFF[_name] = (_acc, _acc + _w)
    _acc += _w


def _cols(w_in, name):
    lo, hi = _OFF[name]
    return w_in[:, lo:hi]


def _pad_cols(w, width):
    return jnp.pad(w, ((0, 0), (0, width - w.shape[1])))


def _rms_rows(x):
    return x * lax.rsqrt(jnp.mean(x * x, axis=-1, keepdims=True) + EPS)


def _dot(a, b):
    return jnp.dot(a, b, preferred_element_type=F32)


def _dot_nt(a, b):
    return lax.dot_general(a, b, (((1,), (1,)), ((), ())), preferred_element_type=F32)


def _dot_tn(a, b):
    return lax.dot_general(a, b, (((0,), (0,)), ((), ())), preferred_element_type=F32)


def _sigmoid(x):
    return 1.0 / (1.0 + jnp.exp(-x))


def _params(sem, vmem_mb):
    return pltpu.CompilerParams(dimension_semantics=sem, vmem_limit_bytes=vmem_mb * 1024 * 1024)


def _proj_body(epilogue, n_extra, x_ref, g_ref, w_ref, *rest):
    extra = rest[:n_extra]
    outs = rest[n_extra:-1]
    h_scr = rest[-1]

    @pl.when(pl.program_id(1) == 0)
    def _():
        h_scr[...] = (_rms_rows(x_ref[...]) * g_ref[...]).astype(BF16)

    epilogue(_dot(h_scr[...], w_ref[...]), extra, outs)


def _proj_call(name, x2d, gain, w, epilogue, extras, out_shapes, out_widths, tm, tn, vmem_mb):
    n, d = x2d.shape
    c = w.shape[1]
    in_specs = [pl.BlockSpec((tm, d), lambda i, j: (i, 0)),
                pl.BlockSpec((1, d), lambda i, j: (0, 0)),
                pl.BlockSpec((d, tn), lambda i, j: (0, j))]
    single = c == tn
    for e in extras:
        if e.shape[1] == c and not single:
            in_specs.append(pl.BlockSpec((1, tn), lambda i, j: (0, j)))
        else:
            in_specs.append(pl.BlockSpec(e.shape, lambda i, j: (0, 0)))
    out_specs = [pl.BlockSpec((tm, ow), (lambda i, j: (i, 0)) if single else (lambda i, j: (i, j)))
                 for ow in out_widths]
    return pl.pallas_call(
        functools.partial(_proj_body, epilogue, len(extras)),
        grid=(n // tm, c // tn),
        in_specs=in_specs,
        out_specs=out_specs,
        out_shape=out_shapes,
        scratch_shapes=[pltpu.VMEM((tm, d), BF16)],
        compiler_params=_params(("parallel", "arbitrary"), vmem_mb),
        name=name,
    )(x2d, gain.reshape(1, d), w, *extras)


def _epi_plain(acc, extra, outs):
    outs[0][...] = acc.astype(outs[0].dtype)


def _epi_headnorm(head_dim, scale, acc, extra, outs):
    gain = extra[0][...]
    for h in range(acc.shape[1] // head_dim):
        sl = slice(h * head_dim, (h + 1) * head_dim)
        outs[0][:, sl] = (_rms_rows(acc[:, sl]) * gain * scale).astype(outs[0].dtype)


def _epi_dsa_kv(acc, extra, outs):
    k_gain, idx_gain = extra
    kn_ref, v_ref, ki_ref, wi_ref = outs
    for h in range(DSA_KV_HEADS):
        sl = slice(h * DSA_HEAD_DIM, (h + 1) * DSA_HEAD_DIM)
        kn_ref[:, sl] = (_rms_rows(acc[:, sl]) * k_gain[...]).astype(BF16)
    v_ref[...] = acc[:, 256:512].astype(BF16)
    a = acc[:, 512:640]
    ms = jnp.sum(a * a, axis=-1, keepdims=True) * (1.0 / IDX_DIM)
    ki_ref[...] = (a * lax.rsqrt(ms + EPS) * idx_gain[...]).astype(BF16)
    wi_ref[...] = acc[:, 640:768] * (IDX_HEADS ** -0.5 * IDX_DIM ** -0.5)


def _epi_mem_kv(acc, extra, outs):
    gain = extra[0][...]
    j = pl.program_id(1)

    @pl.when(j == 0)
    def _():
        for h in range(MEM_HEADS):
            sl = slice(h * MEM_HEAD_DIM, (h + 1) * MEM_HEAD_DIM)
            outs[0][:, sl] = (_rms_rows(acc[:, sl]) * gain).astype(BF16)

    @pl.when(j == 1)
    def _():
        outs[0][...] = acc.astype(BF16)


def _split2(x):
    hi = x.astype(BF16)
    return hi, (x - hi.astype(F32)).astype(BF16)


def _gla_body(q_ref, k_ref, v_ref, g_ref, a_ref, wa_ref, ba_ref, ng_ref, o_ref, st_scr, la_scr):
    tb = q_ref.shape[0]
    c = GLA_CHUNK

    @pl.when(pl.program_id(1) == 0)
    def _():
        st_scr[...] = jnp.zeros_like(st_scr)

    pre = _dot(a_ref[...].astype(BF16), wa_ref[...]) + ba_ref[...]
    la_scr[...] = (jnp.minimum(pre, 0.0) - jnp.log1p(jnp.exp(-jnp.abs(pre)))) * (1.0 / GLA_GATE_TEMP)

    causal = lax.broadcasted_iota(I32, (c, c), 1) <= lax.broadcasted_iota(I32, (c, c), 0)
    tri = jnp.where(causal, 1.0, 0.0).astype(BF16)

    def chunk(ci):
        r0 = pl.multiple_of(ci * c, c)
        rows = pl.ds(r0, c)
        hi, lo = _split2(la_scr[rows, :])
        cum = _dot(tri, hi) + _dot(tri, lo)
        tot = cum[c - 1:c, :]
        kf = k_ref[rows, :]
        q_dec = (q_ref[rows, :] * (GLA_DK ** -0.5) * jnp.exp(cum)).astype(BF16)
        k_inv = (kf * jnp.exp(-cum)).astype(BF16)
        k_end = (kf * jnp.exp(tot - cum)).astype(BF16)
        decay = jnp.exp(tot)
        for h in range(GLA_HEADS):
            ks = slice(h * GLA_DK, (h + 1) * GLA_DK)
            vs = slice(h * GLA_DV, (h + 1) * GLA_DV)
            qd = q_dec[:, ks]
            vv = v_ref[rows, vs].astype(BF16)
            att = jnp.where(causal, _dot_nt(qd, k_inv[:, ks]), 0.0).astype(BF16)
            st = st_scr[h]
            o = _dot(att, vv) + _dot_nt(qd, st.astype(BF16))
            st_scr[h] = st * decay[:, ks] + _dot_tn(vv, k_end[:, ks])
            gg = g_ref[rows, vs].astype(F32)
            o_ref[rows, vs] = (_rms_rows(o) * ng_ref[...] * (gg * _sigmoid(gg))).astype(o_ref.dtype)

    def chunk_group(ji, carry):
        for u in range(GLA_UNROLL):
            chunk(GLA_UNROLL * ji + u)
        return carry

    lax.fori_loop(0, tb // (GLA_UNROLL * c), chunk_group, 0)


def _gla_call(zg, wa, ba, ng, batch, seq, tb):
    n = zg.shape[0]
    nb = seq // tb
    row = lambda b, i: b * nb + i
    in_specs = [
        pl.BlockSpec((tb, 512), lambda b, i: (row(b, i), 0)),
        pl.BlockSpec((tb, 512), lambda b, i: (row(b, i), 1)),
        pl.BlockSpec((tb, 1024), lambda b, i: (row(b, i), 1)),
        pl.BlockSpec((tb, 1024), lambda b, i: (row(b, i), 2)),
        pl.BlockSpec((tb, LANE), lambda b, i: (row(b, i), 3072 // LANE)),
        pl.BlockSpec(wa.shape, lambda b, i: (0, 0)),
        pl.BlockSpec(ba.shape, lambda b, i: (0, 0)),
        pl.BlockSpec(ng.shape, lambda b, i: (0, 0)),
    ]
    hk = GLA_HEADS * GLA_DK
    return pl.pallas_call(
        _gla_body,
        grid=(batch, nb),
        in_specs=in_specs,
        out_specs=pl.BlockSpec((tb, GLA_HEADS * GLA_DV), lambda b, i: (row(b, i), 0)),
        out_shape=jax.ShapeDtypeStruct((n, GLA_HEADS * GLA_DV), BF16),
        scratch_shapes=[pltpu.VMEM((GLA_HEADS, GLA_DV, GLA_DK), F32),
                        pltpu.VMEM((tb, hk), F32)],
        compiler_params=_params(("parallel", "arbitrary"), 48),
        name="gla",
    )(zg, zg, zg, zg, zg, wa, ba, ng)


DSA_TQ = 256
DSA_TK = 512
DSA_COUNT_ROWS = 128
DSA_MAX_BISECT = 320
DSA_MIN_DENOM = 2.0 ** -100


def _dsa_body(topk, qn_ref, qi_ref, wi_ref, kn_ref, v_ref, ki_ref, o_ref,
              sc_scr, qa_scr, kmax_scr, s_scr, m_scr, l_scr, acc_scr):
    tq, tk, hd = DSA_TQ, DSA_TK, DSA_HEAD_DIM
    grp = DSA_HEADS // DSA_KV_HEADS
    kf = float(topk)
    q0 = pl.program_id(1) * tq
    nk = (q0 + tq + tk - 1) // tk
    qpos = q0 + lax.broadcasted_iota(I32, (tq, tk), 0)
    kofs = lax.broadcasted_iota(I32, (tq, tk), 1)
    inf = jnp.inf

    def chunk_start(ci):
        return pl.multiple_of(ci * tk, tk)

    @pl.when(pl.program_id(1) == 0)
    def _():
        def body(ci, mx):
            kc = kn_ref[pl.ds(chunk_start(ci), tk), :].astype(F32)
            return tuple(jnp.maximum(mx[g], jnp.sum(kc[:, g * hd:(g + 1) * hd] ** 2, axis=1, keepdims=True))
                         for g in range(DSA_KV_HEADS))

        mx = lax.fori_loop(0, kn_ref.shape[0] // tk, body,
                           tuple(jnp.zeros((tk, 1), F32) for _ in range(DSA_KV_HEADS)))
        for g in range(DSA_KV_HEADS):
            kmax_scr[g] = jnp.broadcast_to(jnp.max(mx[g], axis=0, keepdims=True), kmax_scr.shape[1:])

    wi = wi_ref[...]

    def score_chunk(ci, carry):
        rmax, rmin, n_pos, n_nonneg = carry
        k0 = chunk_start(ci)
        kc = ki_ref[pl.ds(k0, tk), :]
        score = jnp.zeros((tq, tk), F32)
        for h in range(IDX_HEADS):
            qh = qi_ref[:, h * LANE:(h + 1) * LANE]
            score = score + wi[:, h:h + 1] * jnp.maximum(_dot_nt(qh, kc), 0.0)
        valid = k0 + kofs <= qpos
        masked = jnp.where(valid, score, -inf)
        sc_scr[:, pl.ds(k0, tk)] = masked
        for t in range(tk // LANE):
            ts = slice(t * LANE, (t + 1) * LANE)
            rmax = jnp.maximum(rmax, masked[:, ts])
            rmin = jnp.minimum(rmin, jnp.where(valid[:, ts], score[:, ts], inf))
            n_pos = n_pos + jnp.where(masked[:, ts] > 0.0, 1.0, 0.0)
            n_nonneg = n_nonneg + jnp.where(masked[:, ts] >= 0.0, 1.0, 0.0)
        return rmax, rmin, n_pos, n_nonneg

    n_pairs = (nk + 1) // 2
    rmax, rmin, n_pos, n_nonneg = lax.fori_loop(
        0, n_pairs, lambda ji, c: score_chunk(2 * ji + 1, score_chunk(2 * ji, c)),
        (jnp.full((tq, LANE), -inf, F32), jnp.full((tq, LANE), inf, F32),
         jnp.zeros((tq, LANE), F32), jnp.zeros((tq, LANE), F32)))
    hi0 = jnp.broadcast_to(jnp.max(rmax, axis=1, keepdims=True), (tq, LANE))
    lo0 = jnp.broadcast_to(jnp.min(rmin, axis=1, keepdims=True), (tq, LANE))
    ones_sum = jnp.ones((LANE, LANE), BF16)
    c_pos = _dot(n_pos.astype(BF16), ones_sum)
    c_nn = _dot(n_nonneg.astype(BF16), ones_sum)

    def count(*tests):
        partial = []
        for r0 in range(0, tq, DSA_COUNT_ROWS):
            rows = slice(r0, r0 + DSA_COUNT_ROWS)
            thr_b = [thr[rows] for _, thr in tests]

            def body(ci, cnts):
                sc = sc_scr[rows, pl.ds(chunk_start(ci), tk)]
                out = []
                for (pred, _), tb, cnt in zip(tests, thr_b, cnts):
                    for t in range(tk // LANE):
                        cnt = jnp.where(pred(sc[:, t * LANE:(t + 1) * LANE], tb), cnt + 1.0, cnt)
                    out.append(cnt)
                return tuple(out)

            partial.append(lax.fori_loop(0, nk, body,
                                         tuple(jnp.zeros((DSA_COUNT_ROWS, LANE), F32) for _ in tests)))
        return [jnp.concatenate([_dot(p[k].astype(BF16), ones_sum) for p in partial], axis=0)
                for k in range(len(tests))]

    ge_pred = lambda a, b: a >= b
    n_causal = (q0 + 1 + lax.broadcasted_iota(I32, (tq, LANE), 0)).astype(F32)
    few = n_causal <= kf
    zero_tied = (c_pos < kf) & (c_nn >= kf)
    lo_init = jnp.where(few, lo0, jnp.where(c_nn >= kf, 0.0, lo0))
    hi_init = jnp.where(few | zero_tied, lo_init, jnp.where(c_nn < kf, 0.0, hi0))

    def bisect_cond(carry):
        it, _, _, pending = carry
        return (it < DSA_MAX_BISECT) & (pending > 0.0)

    def bisect(carry):
        it, lo, hi, _ = carry
        mid = 0.5 * lo + 0.5 * hi
        (cnt,) = count((ge_pred, mid))
        pending = jnp.max(jnp.where(mid > lo, jnp.where(mid < hi, 1.0, 0.0), 0.0))
        lo = jnp.where(cnt >= kf, mid, lo)
        hi = jnp.where(cnt > kf, hi, mid)
        return it + 1, lo, hi, pending

    _, thr, _, _ = lax.while_loop(bisect_cond, bisect, (jnp.int32(0), lo_init, hi_init, jnp.float32(1.0)))
    thr_w = jnp.tile(thr, (1, tk // LANE))
    (c_ge,) = count((ge_pred, thr))

    @pl.when(jnp.max(c_ge) > kf)
    def _():
        def low_body(ci, low):
            sc = sc_scr[:, pl.ds(chunk_start(ci), tk)]
            return jnp.minimum(low, jnp.min(jnp.where(sc >= thr_w, sc, inf), axis=1, keepdims=True))

        tied = jnp.broadcast_to(lax.fori_loop(0, nk, low_body, jnp.full((tq, 1), inf, F32)), (tq, LANE))
        tied_w = jnp.tile(tied, (1, tk // LANE))
        (c_eq,) = count((lambda a, b: a == b, tied))
        allowed = jnp.tile(kf - (c_ge - c_eq), (1, tk // LANE))
        upper = jnp.where(lax.broadcasted_iota(I32, (tk, tk), 0) <= lax.broadcasted_iota(I32, (tk, tk), 1),
                          1.0, 0.0).astype(BF16)

        def body(ci, seen):
            k0 = chunk_start(ci)
            sc = sc_scr[:, pl.ds(k0, tk)]
            eq = jnp.where(sc == tied_w, 1.0, 0.0).astype(BF16)
            rank = seen + _dot(eq, upper)
            drop = jnp.where(sc == tied_w, jnp.where(rank > allowed, 1.0, 0.0), 0.0) > 0.0
            sc_scr[:, pl.ds(k0, tk)] = jnp.where(drop, -inf, sc)
            return jnp.broadcast_to(rank[:, tk - 1:tk], (tq, tk))

        lax.fori_loop(0, nk, body, jnp.zeros((tq, tk), F32))

    lane_q = lax.broadcasted_iota(I32, (tq, hd), 1)
    for g in range(DSA_KV_HEADS):
        kmax2 = kmax_scr[g][0:1, :]
        for r in range(grp):
            hcol = (g * grp + r) * hd
            rows = slice(r * tq, (r + 1) * tq)
            qh = qn_ref[:, hcol:hcol + hd]
            qf = qh.astype(F32)
            bound = jnp.sqrt(jnp.sum(qf * qf, axis=1, keepdims=True) * kmax2)
            qa_scr[g, rows, 0:hd] = qh
            qa_scr[g, rows, hd:2 * hd] = jnp.where(lane_q == 0, bound, 0.0).astype(BF16)
    acc_scr[...] = jnp.zeros_like(acc_scr)
    lane_k = lax.broadcasted_iota(I32, (tk, hd), 1)
    k_tail = jnp.where(lane_k == 0, -1.0, 0.0).astype(BF16)
    v_tail = jnp.where(lane_k == 0, 1.0, 0.0).astype(BF16)

    def sel_bias(k0):
        return jnp.where(sc_scr[:, pl.ds(k0, tk)] >= thr_w, 0.0, -inf)

    def logits(ci, g):
        ka = jnp.concatenate([kn_ref[pl.ds(chunk_start(ci), tk), g * hd:(g + 1) * hd], k_tail], axis=1)
        return _dot_nt(qa_scr[g], ka)

    def weighted_values(ci, g, s, bias):
        va = jnp.concatenate([v_ref[pl.ds(chunk_start(ci), tk), g * hd:(g + 1) * hd], v_tail], axis=1)
        p = jnp.exp2((s.reshape(grp, tq, tk) + bias[None]).reshape(grp * tq, tk))
        return _dot(p.astype(BF16), va)

    last_chunk = kn_ref.shape[0] // tk - 1
    for g in range(DSA_KV_HEADS):
        s_scr[g] = logits(0, g)

    def attend(ji, carry):
        ca = 2 * ji
        cb = ca + 1
        bias_a = sel_bias(chunk_start(ca))
        bias_b = sel_bias(chunk_start(cb))
        for g in range(DSA_KV_HEADS):
            s_a = s_scr[g]
            s_b = logits(cb, g)
            pv_a = weighted_values(ca, g, s_a, bias_a)
            s_scr[g] = logits(jnp.minimum(ca + 2, last_chunk), g)
            acc_scr[g] += pv_a + weighted_values(cb, g, s_b, bias_b)
        return carry

    lax.fori_loop(0, n_pairs, attend, 0)

    def write_out(g, out):
        for r in range(grp):
            hcol = (g * grp + r) * hd
            o_ref[:, hcol:hcol + hd] = out[r * tq:(r + 1) * tq, :].astype(o_ref.dtype)

    l_min = inf
    for g in range(DSA_KV_HEADS):
        acc = acc_scr[g]
        denom = acc[:, hd:hd + 1]
        l_min = jnp.minimum(l_min, jnp.min(denom))
        write_out(g, acc[:, 0:hd] / denom)

    @pl.when(jnp.logical_not(l_min >= DSA_MIN_DENOM))
    def _():
        m_scr[...] = jnp.full(m_scr.shape, MASKED_LOGIT, F32)
        l_scr[...] = jnp.zeros_like(l_scr)
        acc_scr[...] = jnp.zeros_like(acc_scr)

        def attend_online(ci, carry):
            k0 = chunk_start(ci)
            bias = sel_bias(k0)
            for g in range(DSA_KV_HEADS):
                hs = slice(g * hd, (g + 1) * hd)
                s = _dot_nt(qa_scr[g, :, 0:hd], kn_ref[pl.ds(k0, tk), hs])
                s = (s.reshape(grp, tq, tk) + bias[None]).reshape(grp * tq, tk)
                m_prev = m_scr[g]
                m_new = jnp.maximum(m_prev, jnp.max(s, axis=1, keepdims=True))
                p = jnp.exp2(s - jnp.tile(m_new, (1, tk // LANE)))
                alpha = jnp.exp2(m_prev - m_new)
                l_scr[g] = alpha * l_scr[g] + jnp.sum(p, axis=1, keepdims=True)
                acc_scr[g, :, 0:hd] = alpha * acc_scr[g, :, 0:hd] + _dot(p.astype(BF16), v_ref[pl.ds(k0, tk), hs])
                m_scr[g] = m_new
            return carry

        lax.fori_loop(0, nk, attend_online, 0)
        for g in range(DSA_KV_HEADS):
            write_out(g, acc_scr[g, :, 0:hd] / l_scr[g])


def _dsa_call(qn, qi, wi, kn, vv, ki, batch, seq):
    n = qn.shape[0]
    tq, hd = DSA_TQ, DSA_HEAD_DIM
    nb = seq // tq
    topk = min(IDX_TOPK_MAX, seq // 4)
    grp = DSA_HEADS // DSA_KV_HEADS
    assert seq // LANE <= 256, "per-lane partial counts must stay exact in bf16"
    row = lambda b, i: (b * nb + i, 0)
    per_batch = lambda b, i: (b, 0)
    return pl.pallas_call(
        functools.partial(_dsa_body, topk),
        grid=(batch, nb),
        in_specs=[pl.BlockSpec((tq, qn.shape[1]), row),
                  pl.BlockSpec((tq, qi.shape[1]), row),
                  pl.BlockSpec((tq, LANE), row),
                  pl.BlockSpec((seq, kn.shape[1]), per_batch),
                  pl.BlockSpec((seq, vv.shape[1]), per_batch),
                  pl.BlockSpec((seq, LANE), per_batch)],
        out_specs=pl.BlockSpec((tq, DSA_HEADS * hd), row),
        out_shape=jax.ShapeDtypeStruct((n, DSA_HEADS * hd), BF16),
        scratch_shapes=[pltpu.VMEM((tq, seq), F32),
                        pltpu.VMEM((DSA_KV_HEADS, grp * tq, 2 * hd), BF16),
                        pltpu.VMEM((DSA_KV_HEADS, 8, LANE), F32),
                        pltpu.VMEM((DSA_KV_HEADS, grp * tq, DSA_TK), F32),
                        pltpu.VMEM((DSA_KV_HEADS, grp * tq, LANE), F32),
                        pltpu.VMEM((DSA_KV_HEADS, grp * tq, LANE), F32),
                        pltpu.VMEM((DSA_KV_HEADS, grp * tq, 2 * hd), F32)],
        compiler_params=_params(("arbitrary", "arbitrary"), 48),
        name="dsa",
    )(qn, qi, wi, kn, vv, ki)


def _mem_attention(q_ref, k_ref, v_ref):
    heads = []
    for h in range(MEM_HEADS):
        sl = slice(h * MEM_HEAD_DIM, (h + 1) * MEM_HEAD_DIM)
        s = _dot_nt(q_ref[:, sl], k_ref[:, sl])
        p = jnp.exp(s - jnp.max(s, axis=1, keepdims=True))
        o = _dot(p.astype(BF16), v_ref[:, sl]) / jnp.sum(p, axis=1, keepdims=True)
        heads.append(o.astype(BF16))
    return jnp.concatenate(heads, axis=1)


def _merge_body(x_ref, gmix_ref, og_ref, od_ref, qm_ref, km_ref, vm_ref, wgate_ref, bgate_ref,
                wg_ref, wd_ref, wm_ref, wo_ref, o_ref):
    d = D_MODEL
    x = x_ref[...]
    h = (_rms_rows(x) * gmix_ref[...]).astype(BF16)
    o_mem = _mem_attention(qm_ref, km_ref, vm_ref)
    branches = (_dot(og_ref[...], wg_ref[...]), _dot(od_ref[...], wd_ref[...]), _dot(o_mem, wm_ref[...]))
    merged = None
    for j, br in enumerate(branches):
        cols = slice(j * d, (j + 1) * d)
        gate = _sigmoid(_dot(h, wgate_ref[:, cols]) + bgate_ref[:, cols])
        merged = gate * br if merged is None else merged + gate * br
    o_ref[...] = x + _dot(merged.astype(BF16), wo_ref[...])


def _merge_call(x2d, g_mix, o_gla, o_dsa, qm, kv, seq, mem_len, w_gate, b_gate, wg, wd, wm, wo, tm):
    n, d = x2d.shape
    row = lambda i: (i, 0)
    fixed = lambda i: (0, 0)
    per_batch = seq // tm
    return pl.pallas_call(
        _merge_body,
        grid=(n // tm,),
        in_specs=[pl.BlockSpec((tm, d), row), pl.BlockSpec((1, d), fixed),
                  pl.BlockSpec((tm, d), row), pl.BlockSpec((tm, d), row), pl.BlockSpec((tm, d), row),
                  pl.BlockSpec((mem_len, d), lambda i: (i // per_batch, 0)),
                  pl.BlockSpec((mem_len, d), lambda i: (i // per_batch, 1)),
                  pl.BlockSpec((d, 3 * d), fixed), pl.BlockSpec((1, 3 * d), fixed),
                  pl.BlockSpec((d, d), fixed), pl.BlockSpec((d, d), fixed),
                  pl.BlockSpec((d, d), fixed), pl.BlockSpec((d, d), fixed)],
        out_specs=pl.BlockSpec((tm, d), row),
        out_shape=jax.ShapeDtypeStruct((n, d), F32),
        compiler_params=_params(("parallel",), 56),
        name="merge",
    )(x2d, g_mix.reshape(1, d), o_gla, o_dsa, qm, kv, kv, w_gate, b_gate.reshape(1, -1), wg, wd, wm, wo)


def _route(z):
    lane = lax.broadcasted_iota(I32, z.shape, 1)
    lanef = lane.astype(F32)
    far = 1e9
    neg = -jnp.inf
    gmask = lane < N_GROUPS
    gmax = jnp.max(jnp.where(gmask, z, neg), axis=1, keepdims=True)
    gsum = jnp.sum(jnp.where(gmask, jnp.exp(z - gmax), 0.0), axis=1, keepdims=True)
    g_w = 1.0 / gsum
    g_sel = jnp.min(jnp.where(gmask & (z == gmax), lanef, far), axis=1, keepdims=True)
    lane_grp = ((lane - N_GROUPS) >> 2).astype(F32)
    emask = (lane >= N_GROUPS) & (lane < N_GROUPS + N_EXPERTS) & (lane_grp == g_sel)
    e1 = jnp.max(jnp.where(emask, z, neg), axis=1, keepdims=True)
    i1 = jnp.min(jnp.where(emask & (z == e1), lanef, far), axis=1, keepdims=True)
    emask2 = emask & (lanef != i1)
    e2 = jnp.max(jnp.where(emask2, z, neg), axis=1, keepdims=True)
    i2 = jnp.min(jnp.where(emask2 & (z == e2), lanef, far), axis=1, keepdims=True)
    t = jnp.exp(e2 - e1)
    p1 = 1.0 / (1.0 + t)
    p2 = t / (1.0 + t)
    return jnp.where(lanef == i1, g_w * p1, jnp.where(lanef == i2, g_w * p2, 0.0))


def _moe_body(x_ref, g_ref, wrh_ref, wrl_ref, br_ref, wg_ref, wu_ref, wd_ref, o_ref, h_scr, comb_scr):
    e = pl.program_id(1)

    @pl.when(e == 0)
    def _():
        x = x_ref[...]
        h = _rms_rows(x) * g_ref[...]
        hh = h.astype(BF16)
        hl = (h - hh.astype(F32)).astype(BF16)
        h_scr[...] = hh
        z = _dot(hh, wrh_ref[...]) + _dot(hl, wrh_ref[...]) + _dot(hh, wrl_ref[...]) + br_ref[...]
        comb_scr[...] = _route(z)
        o_ref[...] = x

    lane = lax.broadcasted_iota(I32, comb_scr.shape, 1)
    ce = jnp.sum(jnp.where(lane == e + N_GROUPS, comb_scr[...], 0.0), axis=1, keepdims=True)
    h = h_scr[...]
    gt = _dot(h, wg_ref[0])
    up = _dot(h, wu_ref[0])
    hid = gt * _sigmoid(gt) * up * ce
    o_ref[...] += _dot(hid.astype(BF16), wd_ref[0])


def _moe_call(x1, g_ffn, wr_hi, wr_lo, br, wg, wu, wd, tm):
    n, d = x1.shape
    row = lambda i, e: (i, 0)
    fixed = lambda i, e: (0, 0)
    per_e = lambda i, e: (e, 0, 0)
    return pl.pallas_call(
        _moe_body,
        grid=(n // tm, N_EXPERTS),
        in_specs=[pl.BlockSpec((tm, d), row), pl.BlockSpec((1, d), fixed),
                  pl.BlockSpec((d, LANE), fixed), pl.BlockSpec((d, LANE), fixed),
                  pl.BlockSpec((1, LANE), fixed),
                  pl.BlockSpec((1, d, D_EXPERT), per_e), pl.BlockSpec((1, d, D_EXPERT), per_e),
                  pl.BlockSpec((1, D_EXPERT, d), per_e)],
        out_specs=pl.BlockSpec((tm, d), row),
        out_shape=jax.ShapeDtypeStruct((n, d), F32),
        scratch_shapes=[pltpu.VMEM((tm, d), BF16), pltpu.VMEM((tm, LANE), F32)],
        compiler_params=_params(("parallel", "arbitrary"), 48),
        name="moe",
    )(x1, g_ffn.reshape(1, d), wr_hi, wr_lo, br, wg, wu, wd)


def _layer(x2d, mem2d, batch, seq, mem_len, g_mix, g_mem, w_in, w_gla_a2, b_gla_a2, gla_norm,
           w_mem_kv, dsa_q_norm, dsa_k_norm, idx_k_norm, mem_q_norm, mem_k_norm, b_gate,
           w_o_gla, w_o_dsa, w_o_mem, w_out, g_ffn, w_r1, b_r1, w_r2, b_r2, w_gate, w_up, w_down):
    n, d = x2d.shape
    sds = jax.ShapeDtypeStruct

    w_gla = jnp.concatenate([_cols(w_in, "gq"), _cols(w_in, "gk"), _cols(w_in, "gv"), _cols(w_in, "gg"),
                             _pad_cols(_cols(w_in, "ga"), LANE)], axis=1).astype(BF16)
    (zg,) = _proj_call("proj_gla", x2d, g_mix, w_gla, _epi_plain, [],
                       [sds((n, w_gla.shape[1]), BF16)], [640], 1024, 640, 40)

    (qn,) = _proj_call("proj_dsa_q", x2d, g_mix, _cols(w_in, "dq").astype(BF16),
                       functools.partial(_epi_headnorm, DSA_HEAD_DIM, DSA_HEAD_DIM ** -0.5 * LOG2_E),
                       [dsa_q_norm.reshape(1, -1)], [sds((n, 1024), BF16)], [1024], 1024, 1024, 40)
    (qm,) = _proj_call("proj_mem_q", x2d, g_mix, _cols(w_in, "mq").astype(BF16),
                       functools.partial(_epi_headnorm, MEM_HEAD_DIM, MEM_HEAD_DIM ** -0.5),
                       [mem_q_norm.reshape(1, -1)], [sds((n, 1024), BF16)], [1024], 1024, 1024, 40)

    w_kv = jnp.concatenate([_cols(w_in, "dk"), _cols(w_in, "dv"), _pad_cols(_cols(w_in, "ik"), LANE),
                            _pad_cols(_cols(w_in, "iw"), LANE)], axis=1).astype(BF16)
    kn, vv, ki, wi = _proj_call(
        "proj_dsa_kv", x2d, g_mix, w_kv, _epi_dsa_kv,
        [dsa_k_norm.reshape(1, -1), _pad_cols(idx_k_norm.reshape(1, -1), LANE)],
        [sds((n, 256), BF16), sds((n, 256), BF16), sds((n, LANE), BF16), sds((n, LANE), F32)],
        [256, 256, LANE, LANE], 1024, 768, 40)

    w_iq = _cols(w_in, "iq").reshape(d, IDX_HEADS, IDX_DIM)
    w_iq = jnp.pad(w_iq, ((0, 0), (0, 0), (0, LANE - IDX_DIM))).reshape(d, IDX_HEADS * LANE).astype(BF16)
    (qi,) = _proj_call("proj_idx_q", x2d, g_mix, w_iq, _epi_plain, [],
                       [sds((n, IDX_HEADS * LANE), BF16)], [1024], 1024, 1024, 40)

    wa = jnp.pad(w_gla_a2, ((0, LANE - GLA_GATE_RANK), (0, 0))).astype(BF16)
    o_gla = _gla_call(zg, wa, b_gla_a2.reshape(1, -1), gla_norm.reshape(1, -1), batch, seq, 512)

    o_dsa = _dsa_call(qn, qi, wi, kn, vv, ki, batch, seq)

    (kv,) = _proj_call("proj_mem_kv", mem2d, g_mem, w_mem_kv.astype(BF16), _epi_mem_kv,
                       [mem_k_norm.reshape(1, -1)], [sds((mem2d.shape[0], 2 * d), BF16)], [1024],
                       mem2d.shape[0], 1024, 40)

    x1 = _merge_call(x2d, g_mix, o_gla, o_dsa, qm, kv, seq, mem_len, _cols(w_in, "gates").astype(BF16),
                     b_gate, w_o_gla.astype(BF16), w_o_dsa.astype(BF16), w_o_mem.astype(BF16),
                     w_out.astype(BF16), 256)

    w_r = _pad_cols(jnp.concatenate([w_r1, w_r2], axis=1), LANE)
    wr_hi = w_r.astype(BF16)
    wr_lo = (w_r - wr_hi.astype(F32)).astype(BF16)
    b_r = _pad_cols(jnp.concatenate([b_r1, b_r2]).reshape(1, -1), LANE)
    return _moe_call(x1, g_ffn, wr_hi, wr_lo, b_r, w_gate.astype(BF16), w_up.astype(BF16),
                     w_down.astype(BF16), 1024)


def kernel(x, mem, g_mix, g_mem, w_in, w_gla_a2, b_gla_a2, gla_norm, w_mem_kv, dsa_q_norm, dsa_k_norm, idx_k_norm, mem_q_norm, mem_k_norm, b_gate, w_o_gla, w_o_dsa, w_o_mem, w_out, g_ffn, w_r1, b_r1, w_r2, b_r2, w_gate, w_up, w_down):
    batch, seq, d = x.shape
    mem_len = mem.shape[1]
    x2d = x.reshape(batch * seq, d)
    mem2d = mem.reshape(batch * mem_len, d)
    params = (g_mix, g_mem, w_in, w_gla_a2, b_gla_a2, gla_norm, w_mem_kv, dsa_q_norm, dsa_k_norm,
              idx_k_norm, mem_q_norm, mem_k_norm, b_gate, w_o_gla, w_o_dsa, w_o_mem, w_out, g_ffn,
              w_r1, b_r1, w_r2, b_r2, w_gate, w_up, w_down)
    for layer in range(g_mix.shape[0]):
        x2d = _layer(x2d, mem2d, batch, seq, mem_len, *(p[layer] for p in params))
    return x2d.reshape(batch, seq, d)
```

```python
import functools

import jax
import jax.numpy as jnp
from jax import lax
from jax.experimental import pallas as pl
from jax.experimental.pallas import tpu as pltpu

F32 = jnp.float32
BF16 = jnp.bfloat16
I32 = jnp.int32

D_MODEL = 1024
EPS = 1e-6

GLA_HEADS = 4
GLA_DK = 128
GLA_DV = 256
GLA_GATE_RANK = 16
GLA_GATE_TEMP = 16.0
GLA_CHUNK = 64
GLA_UNROLL = 8

DSA_HEADS = 8
DSA_KV_HEADS = 2
DSA_HEAD_DIM = 128
IDX_HEADS = 8
IDX_DIM = 64
IDX_TOPK_MAX = 256

MEM_HEADS = 4
MEM_HEAD_DIM = 256

N_GROUPS = 4
EXPERTS_PER_GROUP = 4
N_EXPERTS = 16
D_EXPERT = 256

LANE = 128
MASKED_LOGIT = -1e30
LOG2_E = 1.4426950408889634

_OFF = {}
_acc = 0
for _name, _w in (("gq", 512), ("gk", 512), ("gv", 1024), ("gg", 1024), ("ga", 16), ("dq", 1024),
                  ("dk", 256), ("dv", 256), ("iq", 512), ("ik", 64), ("iw", 8), ("mq", 1024),
                  ("gates", 3072)):
    _OFF[_name] = (_acc, _acc + _w)
    _acc += _w


def _cols(w_in, name):
    lo, hi = _OFF[name]
    return w_in[:, lo:hi]


def _pad_cols(w, width):
    return jnp.pad(w, ((0, 0), (0, width - w.shape[1])))


def _rms_rows(x):
    return x * lax.rsqrt(jnp.mean(x * x, axis=-1, keepdims=True) + EPS)


def _dot(a, b):
    return jnp.dot(a, b, preferred_element_type=F32)


def _dot_nt(a, b):
    return lax.dot_general(a, b, (((1,), (1,)), ((), ())), preferred_element_type=F32)


def _dot_tn(a, b):
    return lax.dot_general(a, b, (((0,), (0,)), ((), ())), preferred_element_type=F32)


def _sigmoid(x):
    return 1.0 / (1.0 + jnp.exp(-x))


def _params(sem, vmem_mb):
    return pltpu.CompilerParams(dimension_semantics=sem, vmem_limit_bytes=vmem_mb * 1024 * 1024)


def _proj_body(epilogue, n_extra, x_ref, g_ref, w_ref, *rest):
    extra = rest[:n_extra]
    outs = rest[n_extra:-1]
    h_scr = rest[-1]

    @pl.when(pl.program_id(1) == 0)
    def _():
        h_scr[...] = (_rms_rows(x_ref[...]) * g_ref[...]).astype(BF16)

    epilogue(_dot(h_scr[...], w_ref[...]), extra, outs)


def _proj_call(name, x2d, gain, w, epilogue, extras, out_shapes, out_widths, tm, tn, vmem_mb):
    n, d = x2d.shape
    c = w.shape[1]
    in_specs = [pl.BlockSpec((tm, d), lambda i, j: (i, 0)),
                pl.BlockSpec((1, d), lambda i, j: (0, 0)),
                pl.BlockSpec((d, tn), lambda i, j: (0, j))]
    single = c == tn
    for e in extras:
        if e.shape[1] == c and not single:
            in_specs.append(pl.BlockSpec((1, tn), lambda i, j: (0, j)))
        else:
            in_specs.append(pl.BlockSpec(e.shape, lambda i, j: (0, 0)))
    out_specs = [pl.BlockSpec((tm, ow), (lambda i, j: (i, 0)) if single else (lambda i, j: (i, j)))
                 for ow in out_widths]
    return pl.pallas_call(
        functools.partial(_proj_body, epilogue, len(extras)),
        grid=(n // tm, c // tn),
        in_specs=in_specs,
        out_specs=out_specs,
        out_shape=out_shapes,
        scratch_shapes=[pltpu.VMEM((tm, d), BF16)],
        compiler_params=_params(("parallel", "arbitrary"), vmem_mb),
        name=name,
    )(x2d, gain.reshape(1, d), w, *extras)


def _epi_plain(acc, extra, outs):
    outs[0][...] = acc.astype(outs[0].dtype)


def _epi_headnorm(head_dim, scale, acc, extra, outs):
    gain = extra[0][...]
    for h in range(acc.shape[1] // head_dim):
        sl = slice(h * head_dim, (h + 1) * head_dim)
        outs[0][:, sl] = (_rms_rows(acc[:, sl]) * gain * scale).astype(outs[0].dtype)


def _epi_dsa_kv(acc, extra, outs):
    k_gain, idx_gain = extra
    kn_ref, v_ref, ki_ref, wi_ref, ga_ref = outs
    for h in range(DSA_KV_HEADS):
        sl = slice(h * DSA_HEAD_DIM, (h + 1) * DSA_HEAD_DIM)
        kn_ref[:, sl] = (_rms_rows(acc[:, sl]) * k_gain[...]).astype(BF16)
    v_ref[...] = acc[:, 256:512].astype(BF16)
    a = acc[:, 512:640]
    ms = jnp.sum(a * a, axis=-1, keepdims=True) * (1.0 / IDX_DIM)
    ki_ref[...] = (a * lax.rsqrt(ms + EPS) * idx_gain[...]).astype(BF16)
    wi_ref[...] = acc[:, 640:768] * (IDX_HEADS ** -0.5 * IDX_DIM ** -0.5)
    ga_ref[...] = acc[:, 768:896].astype(BF16)


def _epi_mem_kv(acc, extra, outs):
    gain = extra[0][...]
    j = pl.program_id(1)

    @pl.when(j == 0)
    def _():
        for h in range(MEM_HEADS):
            sl = slice(h * MEM_HEAD_DIM, (h + 1) * MEM_HEAD_DIM)
            outs[0][:, sl] = (_rms_rows(acc[:, sl]) * gain).astype(BF16)

    @pl.when(j == 1)
    def _():
        outs[0][...] = acc.astype(BF16)


def _split2(x):
    hi = x.astype(BF16)
    return hi, (x - hi.astype(F32)).astype(BF16)


def _gla_body(q_ref, k_ref, v_ref, g_ref, a_ref, wa_ref, ba_ref, ng_ref, o_ref, st_scr, la_scr):
    tb = q_ref.shape[0]
    c = GLA_CHUNK

    @pl.when(pl.program_id(1) == 0)
    def _():
        st_scr[...] = jnp.zeros_like(st_scr)

    pre = _dot(a_ref[...].astype(BF16), wa_ref[...]) + ba_ref[...]
    la_scr[...] = (jnp.minimum(pre, 0.0) - jnp.log1p(jnp.exp(-jnp.abs(pre)))) * (1.0 / GLA_GATE_TEMP)

    causal = lax.broadcasted_iota(I32, (c, c), 1) <= lax.broadcasted_iota(I32, (c, c), 0)
    tri = jnp.where(causal, 1.0, 0.0).astype(BF16)

    def chunk(ci):
        r0 = pl.multiple_of(ci * c, c)
        rows = pl.ds(r0, c)
        hi, lo = _split2(la_scr[rows, :])
        cum = _dot(tri, hi) + _dot(tri, lo)
        tot = cum[c - 1:c, :]
        kf = k_ref[rows, :]
        q_dec = (q_ref[rows, :] * (GLA_DK ** -0.5) * jnp.exp(cum)).astype(BF16)
        k_inv = (kf * jnp.exp(-cum)).astype(BF16)
        k_end = (kf * jnp.exp(tot - cum)).astype(BF16)
        decay = jnp.exp(tot)
        for h in range(GLA_HEADS):
            ks = slice(h * GLA_DK, (h + 1) * GLA_DK)
            vs = slice(h * GLA_DV, (h + 1) * GLA_DV)
            qd = q_dec[:, ks]
            vv = v_ref[rows, vs].astype(BF16)
            att = jnp.where(causal, _dot_nt(qd, k_inv[:, ks]), 0.0).astype(BF16)
            st = st_scr[h]
            o = _dot(att, vv) + _dot_nt(qd, st.astype(BF16))
            st_scr[h] = st * decay[:, ks] + _dot_tn(vv, k_end[:, ks])
            gg = g_ref[rows, vs].astype(F32)
            o_ref[rows, vs] = (_rms_rows(o) * ng_ref[...] * (gg * _sigmoid(gg))).astype(o_ref.dtype)

    def chunk_group(ji, carry):
        for u in range(GLA_UNROLL):
            chunk(GLA_UNROLL * ji + u)
        return carry

    lax.fori_loop(0, tb // (GLA_UNROLL * c), chunk_group, 0)


def _gla_call(zg, ga, wa, ba, ng, batch, seq, tb):
    n = zg.shape[0]
    nb = seq // tb
    row = lambda b, i: b * nb + i
    in_specs = [
        pl.BlockSpec((tb, 512), lambda b, i: (row(b, i), 0)),
        pl.BlockSpec((tb, 512), lambda b, i: (row(b, i), 1)),
        pl.BlockSpec((tb, 1024), lambda b, i: (row(b, i), 1)),
        pl.BlockSpec((tb, 1024), lambda b, i: (row(b, i), 2)),
        pl.BlockSpec((tb, LANE), lambda b, i: (row(b, i), 0)),
        pl.BlockSpec(wa.shape, lambda b, i: (0, 0)),
        pl.BlockSpec(ba.shape, lambda b, i: (0, 0)),
        pl.BlockSpec(ng.shape, lambda b, i: (0, 0)),
    ]
    hk = GLA_HEADS * GLA_DK
    return pl.pallas_call(
        _gla_body,
        grid=(batch, nb),
        in_specs=in_specs,
        out_specs=pl.BlockSpec((tb, GLA_HEADS * GLA_DV), lambda b, i: (row(b, i), 0)),
        out_shape=jax.ShapeDtypeStruct((n, GLA_HEADS * GLA_DV), BF16),
        scratch_shapes=[pltpu.VMEM((GLA_HEADS, GLA_DV, GLA_DK), F32),
                        pltpu.VMEM((tb, hk), F32)],
        compiler_params=_params(("parallel", "arbitrary"), 48),
        name="gla",
    )(zg, zg, zg, zg, ga, wa, ba, ng)


DSA_TQ = 256
DSA_TK = 512
DSA_COUNT_ROWS = 128
DSA_MAX_BISECT = 320
DSA_MIN_DENOM = 2.0 ** -100


def _dsa_body(topk, qn_ref, qi_ref, wi_ref, kn_ref, v_ref, ki_ref, o_ref,
              sc_scr, qa_scr, kmax_scr, s_scr, m_scr, l_scr, acc_scr):
    tq, tk, hd = DSA_TQ, DSA_TK, DSA_HEAD_DIM
    grp = DSA_HEADS // DSA_KV_HEADS
    kf = float(topk)
    q0 = pl.program_id(1) * tq
    nk = (q0 + tq + tk - 1) // tk
    qpos = q0 + lax.broadcasted_iota(I32, (tq, tk), 0)
    kofs = lax.broadcasted_iota(I32, (tq, tk), 1)
    inf = jnp.inf

    def chunk_start(ci):
        return pl.multiple_of(ci * tk, tk)

    @pl.when(pl.program_id(1) == 0)
    def _():
        def body(ci, mx):
            kc = kn_ref[pl.ds(chunk_start(ci), tk), :].astype(F32)
            return tuple(jnp.maximum(mx[g], jnp.sum(kc[:, g * hd:(g + 1) * hd] ** 2, axis=1, keepdims=True))
                         for g in range(DSA_KV_HEADS))

        mx = lax.fori_loop(0, kn_ref.shape[0] // tk, body,
                           tuple(jnp.zeros((tk, 1), F32) for _ in range(DSA_KV_HEADS)))
        for g in range(DSA_KV_HEADS):
            kmax_scr[g] = jnp.broadcast_to(jnp.max(mx[g], axis=0, keepdims=True), kmax_scr.shape[1:])

    wi = wi_ref[...]

    def score_chunk(ci, carry):
        rmax, rmin, n_pos, n_nonneg = carry
        k0 = chunk_start(ci)
        kc = ki_ref[pl.ds(k0, tk), :]
        score = jnp.zeros((tq, tk), F32)
        for h in range(IDX_HEADS):
            qh = qi_ref[:, h * LANE:(h + 1) * LANE]
            score = score + wi[:, h:h + 1] * jnp.maximum(_dot_nt(qh, kc), 0.0)
        valid = k0 + kofs <= qpos
        masked = jnp.where(valid, score, -inf)
        sc_scr[:, pl.ds(k0, tk)] = masked
        for t in range(tk // LANE):
            ts = slice(t * LANE, (t + 1) * LANE)
            rmax = jnp.maximum(rmax, masked[:, ts])
            rmin = jnp.minimum(rmin, jnp.where(valid[:, ts], score[:, ts], inf))
            n_pos = n_pos + jnp.where(masked[:, ts] > 0.0, 1.0, 0.0)
            n_nonneg = n_nonneg + jnp.where(masked[:, ts] >= 0.0, 1.0, 0.0)
        return rmax, rmin, n_pos, n_nonneg

    n_pairs = (nk + 1) // 2
    rmax, rmin, n_pos, n_nonneg = lax.fori_loop(
        0, n_pairs, lambda ji, c: score_chunk(2 * ji + 1, score_chunk(2 * ji, c)),
        (jnp.full((tq, LANE), -inf, F32), jnp.full((tq, LANE), inf, F32),
         jnp.zeros((tq, LANE), F32), jnp.zeros((tq, LANE), F32)))
    hi0 = jnp.broadcast_to(jnp.max(rmax, axis=1, keepdims=True), (tq, LANE))
    lo0 = jnp.broadcast_to(jnp.min(rmin, axis=1, keepdims=True), (tq, LANE))
    ones_sum = jnp.ones((LANE, LANE), BF16)
    c_pos = _dot(n_pos.astype(BF16), ones_sum)
    c_nn = _dot(n_nonneg.astype(BF16), ones_sum)

    def count(*tests):
        partial = []
        for r0 in range(0, tq, DSA_COUNT_ROWS):
            rows = slice(r0, r0 + DSA_COUNT_ROWS)
            thr_b = [thr[rows] for _, thr in tests]

            def body(ci, cnts):
                sc = sc_scr[rows, pl.ds(chunk_start(ci), tk)]
                out = []
                for (pred, _), tb, cnt in zip(tests, thr_b, cnts):
                    for t in range(tk // LANE):
                        cnt = cnt + jnp.where(pred(sc[:, t * LANE:(t + 1) * LANE], tb), 1.0, 0.0)
                    out.append(cnt)
                return tuple(out)

            partial.append(lax.fori_loop(0, nk, body,
                                         tuple(jnp.zeros((DSA_COUNT_ROWS, LANE), F32) for _ in tests)))
        return [jnp.concatenate([_dot(p[k].astype(BF16), ones_sum) for p in partial], axis=0)
                for k in range(len(tests))]

    ge_pred = lambda a, b: a >= b
    n_causal = (q0 + 1 + lax.broadcasted_iota(I32, (tq, LANE), 0)).astype(F32)
    few = n_causal <= kf
    zero_tied = (c_pos < kf) & (c_nn >= kf)
    lo_init = jnp.where(few, lo0, jnp.where(c_nn >= kf, 0.0, lo0))
    hi_init = jnp.where(few | zero_tied, lo_init, jnp.where(c_nn < kf, 0.0, hi0))

    def bisect_cond(carry):
        it, _, _, pending = carry
        return (it < DSA_MAX_BISECT) & (pending > 0.0)

    def bisect(carry):
        it, lo, hi, _ = carry
        mid = 0.5 * lo + 0.5 * hi
        (cnt,) = count((ge_pred, mid))
        pending = jnp.max(jnp.where(mid > lo, jnp.where(mid < hi, 1.0, 0.0), 0.0))
        lo = jnp.where(cnt >= kf, mid, lo)
        hi = jnp.where(cnt > kf, hi, mid)
        return it + 1, lo, hi, pending

    _, thr, _, _ = lax.while_loop(bisect_cond, bisect, (jnp.int32(0), lo_init, hi_init, jnp.float32(1.0)))
    thr_w = jnp.tile(thr, (1, tk // LANE))
    (c_ge,) = count((ge_pred, thr))

    @pl.when(jnp.max(c_ge) > kf)
    def _():
        def low_body(ci, low):
            sc = sc_scr[:, pl.ds(chunk_start(ci), tk)]
            return jnp.minimum(low, jnp.min(jnp.where(sc >= thr_w, sc, inf), axis=1, keepdims=True))

        tied = jnp.broadcast_to(lax.fori_loop(0, nk, low_body, jnp.full((tq, 1), inf, F32)), (tq, LANE))
        tied_w = jnp.tile(tied, (1, tk // LANE))
        (c_eq,) = count((lambda a, b: a == b, tied))
        allowed = jnp.tile(kf - (c_ge - c_eq), (1, tk // LANE))
        upper = jnp.where(lax.broadcasted_iota(I32, (tk, tk), 0) <= lax.broadcasted_iota(I32, (tk, tk), 1),
                          1.0, 0.0).astype(BF16)

        def body(ci, seen):
            k0 = chunk_start(ci)
            sc = sc_scr[:, pl.ds(k0, tk)]
            eq = jnp.where(sc == tied_w, 1.0, 0.0).astype(BF16)
            rank = seen + _dot(eq, upper)
            drop = jnp.where(sc == tied_w, jnp.where(rank > allowed, 1.0, 0.0), 0.0) > 0.0
            sc_scr[:, pl.ds(k0, tk)] = jnp.where(drop, -inf, sc)
            return jnp.broadcast_to(rank[:, tk - 1:tk], (tq, tk))

        lax.fori_loop(0, nk, body, jnp.zeros((tq, tk), F32))

    lane_q = lax.broadcasted_iota(I32, (tq, hd), 1)
    for g in range(DSA_KV_HEADS):
        kmax2 = kmax_scr[g][0:1, :]
        for r in range(grp):
            hcol = (g * grp + r) * hd
            rows = slice(r * tq, (r + 1) * tq)
            qh = qn_ref[:, hcol:hcol + hd]
            qf = qh.astype(F32)
            bound = jnp.sqrt(jnp.sum(qf * qf, axis=1, keepdims=True) * kmax2)
            qa_scr[g, rows, 0:hd] = qh
            qa_scr[g, rows, hd:2 * hd] = jnp.where(lane_q == 0, bound, 0.0).astype(BF16)
    acc_scr[...] = jnp.zeros_like(acc_scr)
    lane_k = lax.broadcasted_iota(I32, (tk, hd), 1)
    k_tail = jnp.where(lane_k == 0, -1.0, 0.0).astype(BF16)
    v_tail = jnp.where(lane_k == 0, 1.0, 0.0).astype(BF16)

    def sel_bias(k0):
        return jnp.where(sc_scr[:, pl.ds(k0, tk)] >= thr_w, 0.0, -inf)

    def logits(ci, g):
        ka = jnp.concatenate([kn_ref[pl.ds(chunk_start(ci), tk), g * hd:(g + 1) * hd], k_tail], axis=1)
        return _dot_nt(qa_scr[g], ka)

    def weighted_values(ci, g, s, bias):
        va = jnp.concatenate([v_ref[pl.ds(chunk_start(ci), tk), g * hd:(g + 1) * hd], v_tail], axis=1)
        p = jnp.exp2((s.reshape(grp, tq, tk) + bias[None]).reshape(grp * tq, tk))
        return _dot(p.astype(BF16), va)

    last_chunk = kn_ref.shape[0] // tk - 1
    for g in range(DSA_KV_HEADS):
        s_scr[g] = logits(0, g)

    def attend(ji, carry):
        ca = 2 * ji
        cb = ca + 1
        bias_a = sel_bias(chunk_start(ca))
        bias_b = sel_bias(chunk_start(cb))
        for g in range(DSA_KV_HEADS):
            s_a = s_scr[g]
            s_b = logits(cb, g)
            pv_a = weighted_values(ca, g, s_a, bias_a)
            s_scr[g] = logits(jnp.minimum(ca + 2, last_chunk), g)
            acc_scr[g] += pv_a + weighted_values(cb, g, s_b, bias_b)
        return carry

    lax.fori_loop(0, n_pairs, attend, 0)

    def write_out(g, out):
        for r in range(grp):
            hcol = (g * grp + r) * hd
            o_ref[:, hcol:hcol + hd] = out[r * tq:(r + 1) * tq, :].astype(o_ref.dtype)

    l_min = inf
    for g in range(DSA_KV_HEADS):
        acc = acc_scr[g]
        denom = acc[:, hd:hd + 1]
        l_min = jnp.minimum(l_min, jnp.min(denom))
        write_out(g, acc[:, 0:hd] / denom)

    @pl.when(jnp.logical_not(l_min >= DSA_MIN_DENOM))
    def _():
        m_scr[...] = jnp.full(m_scr.shape, MASKED_LOGIT, F32)
        l_scr[...] = jnp.zeros_like(l_scr)
        acc_scr[...] = jnp.zeros_like(acc_scr)

        def attend_online(ci, carry):
            k0 = chunk_start(ci)
            bias = sel_bias(k0)
            for g in range(DSA_KV_HEADS):
                hs = slice(g * hd, (g + 1) * hd)
                s = _dot_nt(qa_scr[g, :, 0:hd], kn_ref[pl.ds(k0, tk), hs])
                s = (s.reshape(grp, tq, tk) + bias[None]).reshape(grp * tq, tk)
                m_prev = m_scr[g]
                m_new = jnp.maximum(m_prev, jnp.max(s, axis=1, keepdims=True))
                p = jnp.exp2(s - jnp.tile(m_new, (1, tk // LANE)))
                alpha = jnp.exp2(m_prev - m_new)
                l_scr[g] = alpha * l_scr[g] + jnp.sum(p, axis=1, keepdims=True)
                acc_scr[g, :, 0:hd] = alpha * acc_scr[g, :, 0:hd] + _dot(p.astype(BF16), v_ref[pl.ds(k0, tk), hs])
                m_scr[g] = m_new
            return carry

        lax.fori_loop(0, nk, attend_online, 0)
        for g in range(DSA_KV_HEADS):
            write_out(g, acc_scr[g, :, 0:hd] / l_scr[g])


def _dsa_call(qn, qi, wi, kn, vv, ki, batch, seq):
    n = qn.shape[0]
    tq, hd = DSA_TQ, DSA_HEAD_DIM
    nb = seq // tq
    topk = min(IDX_TOPK_MAX, seq // 4)
    grp = DSA_HEADS // DSA_KV_HEADS
    assert seq // LANE <= 256, "per-lane partial counts must stay exact in bf16"
    row = lambda b, i: (b * nb + i, 0)
    per_batch = lambda b, i: (b, 0)
    return pl.pallas_call(
        functools.partial(_dsa_body, topk),
        grid=(batch, nb),
        in_specs=[pl.BlockSpec((tq, qn.shape[1]), row),
                  pl.BlockSpec((tq, qi.shape[1]), row),
                  pl.BlockSpec((tq, LANE), row),
                  pl.BlockSpec((seq, kn.shape[1]), per_batch),
                  pl.BlockSpec((seq, vv.shape[1]), per_batch),
                  pl.BlockSpec((seq, LANE), per_batch)],
        out_specs=pl.BlockSpec((tq, DSA_HEADS * hd), row),
        out_shape=jax.ShapeDtypeStruct((n, DSA_HEADS * hd), BF16),
        scratch_shapes=[pltpu.VMEM((tq, seq), F32),
                        pltpu.VMEM((DSA_KV_HEADS, grp * tq, 2 * hd), BF16),
                        pltpu.VMEM((DSA_KV_HEADS, 8, LANE), F32),
                        pltpu.VMEM((DSA_KV_HEADS, grp * tq, DSA_TK), F32),
                        pltpu.VMEM((DSA_KV_HEADS, grp * tq, LANE), F32),
                        pltpu.VMEM((DSA_KV_HEADS, grp * tq, LANE), F32),
                        pltpu.VMEM((DSA_KV_HEADS, grp * tq, 2 * hd), F32)],
        compiler_params=_params(("arbitrary", "arbitrary"), 48),
        name="dsa",
    )(qn, qi, wi, kn, vv, ki)


def _mem_attention(q_ref, k_ref, v_ref):
    heads = []
    for h in range(MEM_HEADS):
        sl = slice(h * MEM_HEAD_DIM, (h + 1) * MEM_HEAD_DIM)
        s = _dot_nt(q_ref[:, sl], k_ref[:, sl])
        p = jnp.exp(s - jnp.max(s, axis=1, keepdims=True))
        o = _dot(p.astype(BF16), v_ref[:, sl]) / jnp.sum(p, axis=1, keepdims=True)
        heads.append(o.astype(BF16))
    return jnp.concatenate(heads, axis=1)


def _merge_body(x_ref, gmix_ref, og_ref, od_ref, qm_ref, km_ref, vm_ref, wgate_ref, bgate_ref,
                wg_ref, wd_ref, wm_ref, wo_ref, o_ref):
    d = D_MODEL
    x = x_ref[...]
    h = (_rms_rows(x) * gmix_ref[...]).astype(BF16)
    o_mem = _mem_attention(qm_ref, km_ref, vm_ref)
    branches = (_dot(og_ref[...], wg_ref[...]), _dot(od_ref[...], wd_ref[...]), _dot(o_mem, wm_ref[...]))
    merged = None
    for j, br in enumerate(branches):
        cols = slice(j * d, (j + 1) * d)
        gate = _sigmoid(_dot(h, wgate_ref[:, cols]) + bgate_ref[:, cols])
        merged = gate * br if merged is None else merged + gate * br
    o_ref[...] = x + _dot(merged.astype(BF16), wo_ref[...])


def _merge_call(x2d, g_mix, o_gla, o_dsa, qm, kv, seq, mem_len, w_gate, b_gate, wg, wd, wm, wo, tm):
    n, d = x2d.shape
    row = lambda i: (i, 0)
    fixed = lambda i: (0, 0)
    per_batch = seq // tm
    return pl.pallas_call(
        _merge_body,
        grid=(n // tm,),
        in_specs=[pl.BlockSpec((tm, d), row), pl.BlockSpec((1, d), fixed),
                  pl.BlockSpec((tm, d), row), pl.BlockSpec((tm, d), row), pl.BlockSpec((tm, d), row),
                  pl.BlockSpec((mem_len, d), lambda i: (i // per_batch, 0)),
                  pl.BlockSpec((mem_len, d), lambda i: (i // per_batch, 1)),
                  pl.BlockSpec((d, 3 * d), fixed), pl.BlockSpec((1, 3 * d), fixed),
                  pl.BlockSpec((d, d), fixed), pl.BlockSpec((d, d), fixed),
                  pl.BlockSpec((d, d), fixed), pl.BlockSpec((d, d), fixed)],
        out_specs=pl.BlockSpec((tm, d), row),
        out_shape=jax.ShapeDtypeStruct((n, d), F32),
        compiler_params=_params(("parallel",), 56),
        name="merge",
    )(x2d, g_mix.reshape(1, d), o_gla, o_dsa, qm, kv, kv, w_gate, b_gate.reshape(1, -1), wg, wd, wm, wo)


def _route(z):
    lane = lax.broadcasted_iota(I32, z.shape, 1)
    lanef = lane.astype(F32)
    far = 1e9
    neg = -jnp.inf
    gmask = lane < N_GROUPS
    gmax = jnp.max(jnp.where(gmask, z, neg), axis=1, keepdims=True)
    gsum = jnp.sum(jnp.where(gmask, jnp.exp(z - gmax), 0.0), axis=1, keepdims=True)
    g_w = 1.0 / gsum
    g_sel = jnp.min(jnp.where(gmask & (z == gmax), lanef, far), axis=1, keepdims=True)
    lane_grp = ((lane - N_GROUPS) >> 2).astype(F32)
    emask = (lane >= N_GROUPS) & (lane < N_GROUPS + N_EXPERTS) & (lane_grp == g_sel)
    e1 = jnp.max(jnp.where(emask, z, neg), axis=1, keepdims=True)
    i1 = jnp.min(jnp.where(emask & (z == e1), lanef, far), axis=1, keepdims=True)
    emask2 = emask & (lanef != i1)
    e2 = jnp.max(jnp.where(emask2, z, neg), axis=1, keepdims=True)
    i2 = jnp.min(jnp.where(emask2 & (z == e2), lanef, far), axis=1, keepdims=True)
    t = jnp.exp(e2 - e1)
    p1 = 1.0 / (1.0 + t)
    p2 = t / (1.0 + t)
    return jnp.where(lanef == i1, g_w * p1, jnp.where(lanef == i2, g_w * p2, 0.0))


def _moe_body(x_ref, g_ref, wrh_ref, wrl_ref, br_ref, wg_ref, wu_ref, wd_ref, o_ref, h_scr, comb_scr):
    grp = pl.program_id(1)

    @pl.when(grp == 0)
    def _():
        x = x_ref[...]
        h = _rms_rows(x) * g_ref[...]
        hh = h.astype(BF16)
        hl = (h - hh.astype(F32)).astype(BF16)
        h_scr[...] = hh
        z = _dot(hh, wrh_ref[...]) + _dot(hl, wrh_ref[...]) + _dot(hh, wrl_ref[...]) + br_ref[...]
        comb_scr[...] = _route(z)
        o_ref[...] = x

    lane = lax.broadcasted_iota(I32, comb_scr.shape, 1)
    comb = comb_scr[...]
    h = h_scr[...]
    hidden = []
    for j in range(EXPERTS_PER_GROUP):
        first_lane = N_GROUPS + grp * EXPERTS_PER_GROUP + j
        ce = jnp.sum(jnp.where(lane == first_lane, comb, 0.0), axis=1, keepdims=True)
        gt = _dot(h, wg_ref[j])
        up = _dot(h, wu_ref[j])
        hidden.append((gt * _sigmoid(gt) * up * ce).astype(BF16))
    o_ref[...] += _dot(jnp.concatenate(hidden, axis=1), wd_ref[...])


def _moe_call(x1, g_ffn, wr_hi, wr_lo, br, wg, wu, wd, tm):
    n, d = x1.shape
    row = lambda i, e: (i, 0)
    fixed = lambda i, e: (0, 0)
    per_group = lambda i, e: (e, 0, 0)
    epg = EXPERTS_PER_GROUP
    wd = wd.reshape(N_EXPERTS * D_EXPERT, d)
    return pl.pallas_call(
        _moe_body,
        grid=(n // tm, N_GROUPS),
        in_specs=[pl.BlockSpec((tm, d), row), pl.BlockSpec((1, d), fixed),
                  pl.BlockSpec((d, LANE), fixed), pl.BlockSpec((d, LANE), fixed),
                  pl.BlockSpec((1, LANE), fixed),
                  pl.BlockSpec((epg, d, D_EXPERT), per_group), pl.BlockSpec((epg, d, D_EXPERT), per_group),
                  pl.BlockSpec((epg * D_EXPERT, d), lambda i, e: (e, 0))],
        out_specs=pl.BlockSpec((tm, d), row),
        out_shape=jax.ShapeDtypeStruct((n, d), F32),
        scratch_shapes=[pltpu.VMEM((tm, d), BF16), pltpu.VMEM((tm, LANE), F32)],
        compiler_params=_params(("parallel", "arbitrary"), 48),
        name="moe",
    )(x1, g_ffn.reshape(1, d), wr_hi, wr_lo, br, wg, wu, wd)


def _layer(x2d, mem2d, batch, seq, mem_len, g_mix, g_mem, w_in, w_gla_a2, b_gla_a2, gla_norm,
           w_mem_kv, dsa_q_norm, dsa_k_norm, idx_k_norm, mem_q_norm, mem_k_norm, b_gate,
           w_o_gla, w_o_dsa, w_o_mem, w_out, g_ffn, w_r1, b_r1, w_r2, b_r2, w_gate, w_up, w_down):
    n, d = x2d.shape
    sds = jax.ShapeDtypeStruct

    w_gla = w_in[:, _OFF["gq"][0]:_OFF["gg"][1]].astype(BF16)
    (zg,) = _proj_call("proj_gla", x2d, g_mix, w_gla, _epi_plain, [],
                       [sds((n, w_gla.shape[1]), BF16)], [1024], 1024, 1024, 40)

    (qn,) = _proj_call("proj_dsa_q", x2d, g_mix, _cols(w_in, "dq").astype(BF16),
                       functools.partial(_epi_headnorm, DSA_HEAD_DIM, DSA_HEAD_DIM ** -0.5 * LOG2_E),
                       [dsa_q_norm.reshape(1, -1)], [sds((n, 1024), BF16)], [1024], 1024, 1024, 40)
    (qm,) = _proj_call("proj_mem_q", x2d, g_mix, _cols(w_in, "mq").astype(BF16),
                       functools.partial(_epi_headnorm, MEM_HEAD_DIM, MEM_HEAD_DIM ** -0.5),
                       [mem_q_norm.reshape(1, -1)], [sds((n, 1024), BF16)], [1024], 1024, 1024, 40)

    w_kv = jnp.concatenate([_cols(w_in, "dk"), _cols(w_in, "dv"), _pad_cols(_cols(w_in, "ik"), LANE),
                            _pad_cols(_cols(w_in, "iw"), LANE), _pad_cols(_cols(w_in, "ga"), LANE)],
                           axis=1).astype(BF16)
    kn, vv, ki, wi, ga = _proj_call(
        "proj_dsa_kv", x2d, g_mix, w_kv, _epi_dsa_kv,
        [dsa_k_norm.reshape(1, -1), _pad_cols(idx_k_norm.reshape(1, -1), LANE)],
        [sds((n, 256), BF16), sds((n, 256), BF16), sds((n, LANE), BF16), sds((n, LANE), F32),
         sds((n, LANE), BF16)],
        [256, 256, LANE, LANE, LANE], 1024, w_kv.shape[1], 40)

    w_iq = _cols(w_in, "iq").reshape(d, IDX_HEADS, IDX_DIM)
    w_iq = jnp.pad(w_iq, ((0, 0), (0, 0), (0, LANE - IDX_DIM))).reshape(d, IDX_HEADS * LANE).astype(BF16)
    (qi,) = _proj_call("proj_idx_q", x2d, g_mix, w_iq, _epi_plain, [],
                       [sds((n, IDX_HEADS * LANE), BF16)], [1024], 1024, 1024, 40)

    wa = jnp.pad(w_gla_a2, ((0, LANE - GLA_GATE_RANK), (0, 0))).astype(BF16)
    o_gla = _gla_call(zg, ga, wa, b_gla_a2.reshape(1, -1), gla_norm.reshape(1, -1), batch, seq, 512)

    o_dsa = _dsa_call(qn, qi, wi, kn, vv, ki, batch, seq)

    (kv,) = _proj_call("proj_mem_kv", mem2d, g_mem, w_mem_kv.astype(BF16), _epi_mem_kv,
                       [mem_k_norm.reshape(1, -1)], [sds((mem2d.shape[0], 2 * d), BF16)], [1024],
                       mem2d.shape[0], 1024, 40)

    x1 = _merge_call(x2d, g_mix, o_gla, o_dsa, qm, kv, seq, mem_len, _cols(w_in, "gates").astype(BF16),
                     b_gate, w_o_gla.astype(BF16), w_o_dsa.astype(BF16), w_o_mem.astype(BF16),
                     w_out.astype(BF16), 256)

    w_r = _pad_cols(jnp.concatenate([w_r1, w_r2], axis=1), LANE)
    wr_hi = w_r.astype(BF16)
    wr_lo = (w_r - wr_hi.astype(F32)).astype(BF16)
    b_r = _pad_cols(jnp.concatenate([b_r1, b_r2]).reshape(1, -1), LANE)
    return _moe_call(x1, g_ffn, wr_hi, wr_lo, b_r, w_gate.astype(BF16), w_up.astype(BF16),
                     w_down.astype(BF16), 1024)


def kernel(x, mem, g_mix, g_mem, w_in, w_gla_a2, b_gla_a2, gla_norm, w_mem_kv, dsa_q_norm, dsa_k_norm, idx_k_norm, mem_q_norm, mem_k_norm, b_gate, w_o_gla, w_o_dsa, w_o_mem, w_out, g_ffn, w_r1, b_r1, w_r2, b_r2, w_gate, w_up, w_down):
    batch, seq, d = x.shape
    mem_len = mem.shape[1]
    x2d = x.reshape(batch * seq, d)
    mem2d = mem.reshape(batch * mem_len, d)
    params = (g_mix, g_mem, w_in, w_gla_a2, b_gla_a2, gla_norm, w_mem_kv, dsa_q_norm, dsa_k_norm,
              idx_k_norm, mem_q_norm, mem_k_norm, b_gate, w_o_gla, w_o_dsa, w_o_mem, w_out, g_ffn,
              w_r1, b_r1, w_r2, b_r2, w_gate, w_up, w_down)
    for layer in range(g_mix.shape[0]):
        x2d = _layer(x2d, mem2d, batch, seq, mem_len, *(p[layer] for p in params))
    return x2d.reshape(batch, seq, d)
```

```python
import functools

import jax
import jax.numpy as jnp
from jax import lax
from jax.experimental import pallas as pl
from jax.experimental.pallas import tpu as pltpu

F32 = jnp.float32
BF16 = jnp.bfloat16
I32 = jnp.int32

D_MODEL = 1024
EPS = 1e-6

GLA_HEADS = 4
GLA_DK = 128
GLA_DV = 256
GLA_GATE_RANK = 16
GLA_GATE_TEMP = 16.0
GLA_CHUNK = 64
GLA_UNROLL = 8

DSA_HEADS = 8
DSA_KV_HEADS = 2
DSA_HEAD_DIM = 128
IDX_HEADS = 8
IDX_DIM = 64
IDX_TOPK_MAX = 256

MEM_HEADS = 4
MEM_HEAD_DIM = 256

N_GROUPS = 4
EXPERTS_PER_GROUP = 4
N_EXPERTS = 16
D_EXPERT = 256

LANE = 128
MASKED_LOGIT = -1e30
LOG2_E = 1.4426950408889634

_OFF = {}
_acc = 0
for _name, _w in (("gq", 512), ("gk", 512), ("gv", 1024), ("gg", 1024), ("ga", 16), ("dq", 1024),
                  ("dk", 256), ("dv", 256), ("iq", 512), ("ik", 64), ("iw", 8), ("mq", 1024),
                  ("gates", 3072)):
    _OFF[_name] = (_acc, _acc + _w)
    _acc += _w


def _cols(w_in, name):
    lo, hi = _OFF[name]
    return w_in[:, lo:hi]


def _pad_cols(w, width):
    return jnp.pad(w, ((0, 0), (0, width - w.shape[1])))


def _rms_rows(x):
    return x * lax.rsqrt(jnp.mean(x * x, axis=-1, keepdims=True) + EPS)


def _dot(a, b):
    return jnp.dot(a, b, preferred_element_type=F32)


def _dot_nt(a, b):
    return lax.dot_general(a, b, (((1,), (1,)), ((), ())), preferred_element_type=F32)


def _dot_tn(a, b):
    return lax.dot_general(a, b, (((0,), (0,)), ((), ())), preferred_element_type=F32)


def _sigmoid(x):
    return 1.0 / (1.0 + jnp.exp(-x))


def _params(sem, vmem_mb):
    return pltpu.CompilerParams(dimension_semantics=sem, vmem_limit_bytes=vmem_mb * 1024 * 1024)


def _proj_body(epilogue, n_extra, x_ref, g_ref, w_ref, *rest):
    extra = rest[:n_extra]
    outs = rest[n_extra:-1]
    h_scr = rest[-1]

    @pl.when(pl.program_id(1) == 0)
    def _():
        h_scr[...] = (_rms_rows(x_ref[...]) * g_ref[...]).astype(BF16)

    epilogue(_dot(h_scr[...], w_ref[...]), extra, outs)


def _proj_call(name, x2d, gain, w, epilogue, extras, out_shapes, out_widths, tm, tn, vmem_mb):
    n, d = x2d.shape
    c = w.shape[1]
    in_specs = [pl.BlockSpec((tm, d), lambda i, j: (i, 0)),
                pl.BlockSpec((1, d), lambda i, j: (0, 0)),
                pl.BlockSpec((d, tn), lambda i, j: (0, j))]
    single = c == tn
    for e in extras:
        if e.shape[1] == c and not single:
            in_specs.append(pl.BlockSpec((1, tn), lambda i, j: (0, j)))
        else:
            in_specs.append(pl.BlockSpec(e.shape, lambda i, j: (0, 0)))
    out_specs = [pl.BlockSpec((tm, ow), (lambda i, j: (i, 0)) if single else (lambda i, j: (i, j)))
                 for ow in out_widths]
    return pl.pallas_call(
        functools.partial(_proj_body, epilogue, len(extras)),
        grid=(n // tm, c // tn),
        in_specs=in_specs,
        out_specs=out_specs,
        out_shape=out_shapes,
        scratch_shapes=[pltpu.VMEM((tm, d), BF16)],
        compiler_params=_params(("parallel", "arbitrary"), vmem_mb),
        name=name,
    )(x2d, gain.reshape(1, d), w, *extras)


def _epi_plain(acc, extra, outs):
    outs[0][...] = acc.astype(outs[0].dtype)


def _epi_headnorm(head_dim, scale, acc, extra, outs):
    gain = extra[0][...]
    for h in range(acc.shape[1] // head_dim):
        sl = slice(h * head_dim, (h + 1) * head_dim)
        outs[0][:, sl] = (_rms_rows(acc[:, sl]) * gain * scale).astype(outs[0].dtype)


def _epi_dsa_kv(acc, extra, outs):
    k_gain, idx_gain = extra
    kn_ref, v_ref, ki_ref, wi_ref, ga_ref = outs
    for h in range(DSA_KV_HEADS):
        sl = slice(h * DSA_HEAD_DIM, (h + 1) * DSA_HEAD_DIM)
        kn_ref[:, sl] = (_rms_rows(acc[:, sl]) * k_gain[...]).astype(BF16)
    v_ref[...] = acc[:, 256:512].astype(BF16)
    a = acc[:, 512:640]
    ms = jnp.sum(a * a, axis=-1, keepdims=True) * (1.0 / IDX_DIM)
    ki_ref[...] = (a * lax.rsqrt(ms + EPS) * idx_gain[...]).astype(BF16)
    wi_ref[...] = acc[:, 640:768] * (IDX_HEADS ** -0.5 * IDX_DIM ** -0.5)
    ga_ref[...] = acc[:, 768:896].astype(BF16)


def _epi_mem_kv(acc, extra, outs):
    gain = extra[0][...]
    j = pl.program_id(1)

    @pl.when(j == 0)
    def _():
        for h in range(MEM_HEADS):
            sl = slice(h * MEM_HEAD_DIM, (h + 1) * MEM_HEAD_DIM)
            outs[0][:, sl] = (_rms_rows(acc[:, sl]) * gain).astype(BF16)

    @pl.when(j == 1)
    def _():
        outs[0][...] = acc.astype(BF16)


def _split2(x):
    hi = x.astype(BF16)
    return hi, (x - hi.astype(F32)).astype(BF16)


def _gla_body(q_ref, k_ref, v_ref, g_ref, a_ref, wa_ref, ba_ref, ng_ref, o_ref, st_scr, la_scr):
    tb = q_ref.shape[0]
    c = GLA_CHUNK

    @pl.when(pl.program_id(1) == 0)
    def _():
        st_scr[...] = jnp.zeros_like(st_scr)

    pre = _dot(a_ref[...].astype(BF16), wa_ref[...]) + ba_ref[...]
    la_scr[...] = (jnp.minimum(pre, 0.0) - jnp.log1p(jnp.exp(-jnp.abs(pre)))) * (1.0 / GLA_GATE_TEMP)

    causal = lax.broadcasted_iota(I32, (c, c), 1) <= lax.broadcasted_iota(I32, (c, c), 0)
    tri = jnp.where(causal, 1.0, 0.0).astype(BF16)

    def chunk(ci):
        r0 = pl.multiple_of(ci * c, c)
        rows = pl.ds(r0, c)
        hi, lo = _split2(la_scr[rows, :])
        cum = _dot(tri, hi) + _dot(tri, lo)
        tot = cum[c - 1:c, :]
        kf = k_ref[rows, :]
        q_dec = (q_ref[rows, :] * (GLA_DK ** -0.5) * jnp.exp(cum)).astype(BF16)
        k_inv = (kf * jnp.exp(-cum)).astype(BF16)
        k_end = (kf * jnp.exp(tot - cum)).astype(BF16)
        decay = jnp.exp(tot)
        for h in range(GLA_HEADS):
            ks = slice(h * GLA_DK, (h + 1) * GLA_DK)
            vs = slice(h * GLA_DV, (h + 1) * GLA_DV)
            qd = q_dec[:, ks]
            vv = v_ref[rows, vs].astype(BF16)
            att = jnp.where(causal, _dot_nt(qd, k_inv[:, ks]), 0.0).astype(BF16)
            st = st_scr[h]
            o = _dot(att, vv) + _dot_nt(qd, st.astype(BF16))
            st_scr[h] = st * decay[:, ks] + _dot_tn(vv, k_end[:, ks])
            gg = g_ref[rows, vs].astype(F32)
            o_ref[rows, vs] = (_rms_rows(o) * ng_ref[...] * (gg * _sigmoid(gg))).astype(o_ref.dtype)

    def chunk_group(ji, carry):
        for u in range(GLA_UNROLL):
            chunk(GLA_UNROLL * ji + u)
        return carry

    lax.fori_loop(0, tb // (GLA_UNROLL * c), chunk_group, 0)


def _gla_call(zg, ga, wa, ba, ng, batch, seq, tb):
    n = zg.shape[0]
    nb = seq // tb
    row = lambda b, i: b * nb + i
    in_specs = [
        pl.BlockSpec((tb, 512), lambda b, i: (row(b, i), 0)),
        pl.BlockSpec((tb, 512), lambda b, i: (row(b, i), 1)),
        pl.BlockSpec((tb, 1024), lambda b, i: (row(b, i), 1)),
        pl.BlockSpec((tb, 1024), lambda b, i: (row(b, i), 2)),
        pl.BlockSpec((tb, LANE), lambda b, i: (row(b, i), 0)),
        pl.BlockSpec(wa.shape, lambda b, i: (0, 0)),
        pl.BlockSpec(ba.shape, lambda b, i: (0, 0)),
        pl.BlockSpec(ng.shape, lambda b, i: (0, 0)),
    ]
    hk = GLA_HEADS * GLA_DK
    return pl.pallas_call(
        _gla_body,
        grid=(batch, nb),
        in_specs=in_specs,
        out_specs=pl.BlockSpec((tb, GLA_HEADS * GLA_DV), lambda b, i: (row(b, i), 0)),
        out_shape=jax.ShapeDtypeStruct((n, GLA_HEADS * GLA_DV), BF16),
        scratch_shapes=[pltpu.VMEM((GLA_HEADS, GLA_DV, GLA_DK), F32),
                        pltpu.VMEM((tb, hk), F32)],
        compiler_params=_params(("parallel", "arbitrary"), 48),
        name="gla",
    )(zg, zg, zg, zg, ga, wa, ba, ng)


DSA_TQ = 512
DSA_TK = 512
DSA_COUNT_ROWS = 128
DSA_MAX_BISECT = 320
DSA_MIN_DENOM = 2.0 ** -100


def _dsa_body(topk, qn_ref, qi_ref, wi_ref, kn_ref, v_ref, ki_ref, o_ref,
              sc_scr, qa_scr, kmax_scr, s_scr, acc_scr):
    tq, tk, hd = DSA_TQ, DSA_TK, DSA_HEAD_DIM
    grp = DSA_HEADS // DSA_KV_HEADS
    kf = float(topk)
    q0 = pl.program_id(1) * tq
    nk = (q0 + tq + tk - 1) // tk
    qpos = q0 + lax.broadcasted_iota(I32, (tq, tk), 0)
    kofs = lax.broadcasted_iota(I32, (tq, tk), 1)
    inf = jnp.inf

    def chunk_start(ci):
        return pl.multiple_of(ci * tk, tk)

    @pl.when(pl.program_id(1) == 0)
    def _():
        def body(ci, mx):
            kc = kn_ref[pl.ds(chunk_start(ci), tk), :].astype(F32)
            return tuple(jnp.maximum(mx[g], jnp.sum(kc[:, g * hd:(g + 1) * hd] ** 2, axis=1, keepdims=True))
                         for g in range(DSA_KV_HEADS))

        mx = lax.fori_loop(0, kn_ref.shape[0] // tk, body,
                           tuple(jnp.zeros((tk, 1), F32) for _ in range(DSA_KV_HEADS)))
        for g in range(DSA_KV_HEADS):
            kmax_scr[g] = jnp.broadcast_to(jnp.max(mx[g], axis=0, keepdims=True), kmax_scr.shape[1:])

    wi = wi_ref[...]

    def score_chunk(ci, carry):
        rmax, rmin, n_pos, n_nonneg = carry
        k0 = chunk_start(ci)
        kc = ki_ref[pl.ds(k0, tk), :]
        score = jnp.zeros((tq, tk), F32)
        for h in range(IDX_HEADS):
            qh = qi_ref[:, h * LANE:(h + 1) * LANE]
            score = score + wi[:, h:h + 1] * jnp.maximum(_dot_nt(qh, kc), 0.0)
        valid = k0 + kofs <= qpos
        masked = jnp.where(valid, score, -inf)
        sc_scr[:, pl.ds(k0, tk)] = masked
        for t in range(tk // LANE):
            ts = slice(t * LANE, (t + 1) * LANE)
            rmax = jnp.maximum(rmax, masked[:, ts])
            rmin = jnp.minimum(rmin, jnp.where(valid[:, ts], score[:, ts], inf))
            n_pos = n_pos + jnp.where(masked[:, ts] > 0.0, 1.0, 0.0)
            n_nonneg = n_nonneg + jnp.where(masked[:, ts] >= 0.0, 1.0, 0.0)
        return rmax, rmin, n_pos, n_nonneg

    n_pairs = (nk + 1) // 2
    rmax, rmin, n_pos, n_nonneg = lax.fori_loop(
        0, n_pairs, lambda ji, c: score_chunk(2 * ji + 1, score_chunk(2 * ji, c)),
        (jnp.full((tq, LANE), -inf, F32), jnp.full((tq, LANE), inf, F32),
         jnp.zeros((tq, LANE), F32), jnp.zeros((tq, LANE), F32)))
    hi0 = jnp.broadcast_to(jnp.max(rmax, axis=1, keepdims=True), (tq, LANE))
    lo0 = jnp.broadcast_to(jnp.min(rmin, axis=1, keepdims=True), (tq, LANE))
    ones_sum = jnp.ones((LANE, LANE), BF16)
    c_pos = _dot(n_pos.astype(BF16), ones_sum)
    c_nn = _dot(n_nonneg.astype(BF16), ones_sum)

    def count(*tests):
        partial = []
        for r0 in range(0, tq, DSA_COUNT_ROWS):
            rows = slice(r0, r0 + DSA_COUNT_ROWS)
            thr_b = [thr[rows] for _, thr in tests]

            def body(ci, cnts):
                sc = sc_scr[rows, pl.ds(chunk_start(ci), tk)]
                out = []
                for (pred, _), tb, cnt in zip(tests, thr_b, cnts):
                    for t in range(tk // LANE):
                        cnt = cnt + jnp.where(pred(sc[:, t * LANE:(t + 1) * LANE], tb), 1.0, 0.0)
                    out.append(cnt)
                return tuple(out)

            partial.append(lax.fori_loop(0, nk, body,
                                         tuple(jnp.zeros((DSA_COUNT_ROWS, LANE), F32) for _ in tests)))
        return [jnp.concatenate([_dot(p[k].astype(BF16), ones_sum) for p in partial], axis=0)
                for k in range(len(tests))]

    ge_pred = lambda a, b: a >= b
    n_causal = (q0 + 1 + lax.broadcasted_iota(I32, (tq, LANE), 0)).astype(F32)
    few = n_causal <= kf
    zero_tied = (c_pos < kf) & (c_nn >= kf)
    lo_init = jnp.where(few, lo0, jnp.where(c_nn >= kf, 0.0, lo0))
    hi_init = jnp.where(few | zero_tied, lo_init, jnp.where(c_nn < kf, 0.0, hi0))

    def bisect_cond(carry):
        it, _, _, pending = carry
        return (it < DSA_MAX_BISECT) & (pending > 0.0)

    def bisect(carry):
        it, lo, hi, _ = carry
        mid = 0.5 * lo + 0.5 * hi
        (cnt,) = count((ge_pred, mid))
        pending = jnp.max(jnp.where(mid > lo, jnp.where(mid < hi, 1.0, 0.0), 0.0))
        lo = jnp.where(cnt >= kf, mid, lo)
        hi = jnp.where(cnt > kf, hi, mid)
        return it + 1, lo, hi, pending

    _, thr, _, _ = lax.while_loop(bisect_cond, bisect, (jnp.int32(0), lo_init, hi_init, jnp.float32(1.0)))
    thr_w = jnp.tile(thr, (1, tk // LANE))
    (c_ge,) = count((ge_pred, thr))

    @pl.when(jnp.max(c_ge) > kf)
    def _():
        def low_body(ci, low):
            sc = sc_scr[:, pl.ds(chunk_start(ci), tk)]
            return jnp.minimum(low, jnp.min(jnp.where(sc >= thr_w, sc, inf), axis=1, keepdims=True))

        tied = jnp.broadcast_to(lax.fori_loop(0, nk, low_body, jnp.full((tq, 1), inf, F32)), (tq, LANE))
        tied_w = jnp.tile(tied, (1, tk // LANE))
        (c_eq,) = count((lambda a, b: a == b, tied))
        allowed = jnp.tile(kf - (c_ge - c_eq), (1, tk // LANE))
        upper = jnp.where(lax.broadcasted_iota(I32, (tk, tk), 0) <= lax.broadcasted_iota(I32, (tk, tk), 1),
                          1.0, 0.0).astype(BF16)

        def body(ci, seen):
            k0 = chunk_start(ci)
            sc = sc_scr[:, pl.ds(k0, tk)]
            eq = jnp.where(sc == tied_w, 1.0, 0.0).astype(BF16)
            rank = seen + _dot(eq, upper)
            drop = jnp.where(sc == tied_w, jnp.where(rank > allowed, 1.0, 0.0), 0.0) > 0.0
            sc_scr[:, pl.ds(k0, tk)] = jnp.where(drop, -inf, sc)
            return jnp.broadcast_to(rank[:, tk - 1:tk], (tq, tk))

        lax.fori_loop(0, nk, body, jnp.zeros((tq, tk), F32))

    lane_q = lax.broadcasted_iota(I32, (tq, hd), 1)
    for g in range(DSA_KV_HEADS):
        kmax2 = kmax_scr[g][0:1, :]
        for r in range(grp):
            hcol = (g * grp + r) * hd
            rows = slice(r * tq, (r + 1) * tq)
            qh = qn_ref[:, hcol:hcol + hd]
            qf = qh.astype(F32)
            bound = jnp.sqrt(jnp.sum(qf * qf, axis=1, keepdims=True) * kmax2)
            qa_scr[g, rows, 0:hd] = qh
            qa_scr[g, rows, hd:2 * hd] = jnp.where(lane_q == 0, bound, 0.0).astype(BF16)
    acc_scr[...] = jnp.zeros_like(acc_scr)
    lane_k = lax.broadcasted_iota(I32, (tk, hd), 1)
    k_tail = jnp.where(lane_k == 0, -1.0, 0.0).astype(BF16)
    v_tail = jnp.where(lane_k == 0, 1.0, 0.0).astype(BF16)

    def sel_bias(k0):
        return jnp.where(sc_scr[:, pl.ds(k0, tk)] >= thr_w, 0.0, -inf)

    def logits(ci, g):
        ka = jnp.concatenate([kn_ref[pl.ds(chunk_start(ci), tk), g * hd:(g + 1) * hd], k_tail], axis=1)
        return _dot_nt(qa_scr[g], ka)

    def weighted_values(ci, g, s, bias):
        va = jnp.concatenate([v_ref[pl.ds(chunk_start(ci), tk), g * hd:(g + 1) * hd], v_tail], axis=1)
        p = jnp.exp2((s.reshape(grp, tq, tk) + bias[None]).reshape(grp * tq, tk))
        return _dot(p.astype(BF16), va)

    last_chunk = kn_ref.shape[0] // tk - 1
    for g in range(DSA_KV_HEADS):
        s_scr[g] = logits(0, g)

    def attend(ji, carry):
        ca = 2 * ji
        cb = ca + 1
        bias_a = sel_bias(chunk_start(ca))
        bias_b = sel_bias(chunk_start(cb))
        for g in range(DSA_KV_HEADS):
            s_a = s_scr[g]
            s_b = logits(cb, g)
            pv_a = weighted_values(ca, g, s_a, bias_a)
            s_scr[g] = logits(jnp.minimum(ca + 2, last_chunk), g)
            acc_scr[g] += pv_a + weighted_values(cb, g, s_b, bias_b)
        return carry

    lax.fori_loop(0, n_pairs, attend, 0)

    def write_out(g, out):
        for r in range(grp):
            hcol = (g * grp + r) * hd
            o_ref[:, hcol:hcol + hd] = out[r * tq:(r + 1) * tq, :].astype(o_ref.dtype)

    l_min = inf
    for g in range(DSA_KV_HEADS):
        acc = acc_scr[g]
        denom = acc[:, hd:hd + 1]
        l_min = jnp.minimum(l_min, jnp.min(denom))
        write_out(g, acc[:, 0:hd] / denom)

    @pl.when(jnp.logical_not(l_min >= DSA_MIN_DENOM))
    def _():
        for g in range(DSA_KV_HEADS):
            s_scr[g, :, 0:LANE] = jnp.full((grp * tq, LANE), MASKED_LOGIT, F32)
        acc_scr[...] = jnp.zeros_like(acc_scr)

        def attend_online(ci, carry):
            k0 = chunk_start(ci)
            bias = sel_bias(k0)
            for g in range(DSA_KV_HEADS):
                hs = slice(g * hd, (g + 1) * hd)
                s = _dot_nt(qa_scr[g, :, 0:hd], kn_ref[pl.ds(k0, tk), hs])
                s = (s.reshape(grp, tq, tk) + bias[None]).reshape(grp * tq, tk)
                m_prev = s_scr[g, :, 0:LANE]
                m_new = jnp.maximum(m_prev, jnp.max(s, axis=1, keepdims=True))
                p = jnp.exp2(s - jnp.tile(m_new, (1, tk // LANE)))
                alpha = jnp.exp2(m_prev - m_new)
                acc_scr[g, :, hd:2 * hd] = alpha * acc_scr[g, :, hd:2 * hd] + jnp.sum(p, axis=1, keepdims=True)
                acc_scr[g, :, 0:hd] = alpha * acc_scr[g, :, 0:hd] + _dot(p.astype(BF16), v_ref[pl.ds(k0, tk), hs])
                s_scr[g, :, 0:LANE] = m_new
            return carry

        lax.fori_loop(0, nk, attend_online, 0)
        for g in range(DSA_KV_HEADS):
            write_out(g, acc_scr[g, :, 0:hd] / acc_scr[g, :, hd:2 * hd])


def _dsa_call(qn, qi, wi, kn, vv, ki, batch, seq):
    n = qn.shape[0]
    tq, hd = DSA_TQ, DSA_HEAD_DIM
    nb = seq // tq
    topk = min(IDX_TOPK_MAX, seq // 4)
    grp = DSA_HEADS // DSA_KV_HEADS
    assert seq // LANE <= 256, "per-lane partial counts must stay exact in bf16"
    row = lambda b, i: (b * nb + i, 0)
    per_batch = lambda b, i: (b, 0)
    return pl.pallas_call(
        functools.partial(_dsa_body, topk),
        grid=(batch, nb),
        in_specs=[pl.BlockSpec((tq, qn.shape[1]), row),
                  pl.BlockSpec((tq, qi.shape[1]), row),
                  pl.BlockSpec((tq, LANE), row),
                  pl.BlockSpec((seq, kn.shape[1]), per_batch, pipeline_mode=pl.Buffered(1)),
                  pl.BlockSpec((seq, vv.shape[1]), per_batch, pipeline_mode=pl.Buffered(1)),
                  pl.BlockSpec((seq, LANE), per_batch, pipeline_mode=pl.Buffered(1))],
        out_specs=pl.BlockSpec((tq, DSA_HEADS * hd), row),
        out_shape=jax.ShapeDtypeStruct((n, DSA_HEADS * hd), BF16),
        scratch_shapes=[pltpu.VMEM((tq, seq), F32),
                        pltpu.VMEM((DSA_KV_HEADS, grp * tq, 2 * hd), BF16),
                        pltpu.VMEM((DSA_KV_HEADS, 8, LANE), F32),
                        pltpu.VMEM((DSA_KV_HEADS, grp * tq, DSA_TK), F32),
                        pltpu.VMEM((DSA_KV_HEADS, grp * tq, 2 * hd), F32)],
        compiler_params=_params(("arbitrary", "arbitrary"), 58),
        name="dsa",
    )(qn, qi, wi, kn, vv, ki)


def _mem_attention(q_ref, k_ref, v_ref):
    heads = []
    for h in range(MEM_HEADS):
        sl = slice(h * MEM_HEAD_DIM, (h + 1) * MEM_HEAD_DIM)
        s = _dot_nt(q_ref[:, sl], k_ref[:, sl])
        p = jnp.exp(s - jnp.max(s, axis=1, keepdims=True))
        o = _dot(p.astype(BF16), v_ref[:, sl]) / jnp.sum(p, axis=1, keepdims=True)
        heads.append(o.astype(BF16))
    return jnp.concatenate(heads, axis=1)


def _merge_body(x_ref, gmix_ref, og_ref, od_ref, qm_ref, km_ref, vm_ref, wgate_ref, bgate_ref,
                wg_ref, wd_ref, wm_ref, wo_ref, o_ref):
    d = D_MODEL
    x = x_ref[...]
    h = (_rms_rows(x) * gmix_ref[...]).astype(BF16)
    o_mem = _mem_attention(qm_ref, km_ref, vm_ref)
    branches = (_dot(og_ref[...], wg_ref[...]), _dot(od_ref[...], wd_ref[...]), _dot(o_mem, wm_ref[...]))
    merged = None
    for j, br in enumerate(branches):
        cols = slice(j * d, (j + 1) * d)
        gate = _sigmoid(_dot(h, wgate_ref[:, cols]) + bgate_ref[:, cols])
        merged = gate * br if merged is None else merged + gate * br
    o_ref[...] = x + _dot(merged.astype(BF16), wo_ref[...])


def _merge_call(x2d, g_mix, o_gla, o_dsa, qm, kv, seq, mem_len, w_gate, b_gate, wg, wd, wm, wo, tm):
    n, d = x2d.shape
    row = lambda i: (i, 0)
    fixed = lambda i: (0, 0)
    per_batch = seq // tm
    return pl.pallas_call(
        _merge_body,
        grid=(n // tm,),
        in_specs=[pl.BlockSpec((tm, d), row), pl.BlockSpec((1, d), fixed),
                  pl.BlockSpec((tm, d), row), pl.BlockSpec((tm, d), row), pl.BlockSpec((tm, d), row),
                  pl.BlockSpec((mem_len, d), lambda i: (i // per_batch, 0)),
                  pl.BlockSpec((mem_len, d), lambda i: (i // per_batch, 1)),
                  pl.BlockSpec((d, 3 * d), fixed), pl.BlockSpec((1, 3 * d), fixed),
                  pl.BlockSpec((d, d), fixed), pl.BlockSpec((d, d), fixed),
                  pl.BlockSpec((d, d), fixed), pl.BlockSpec((d, d), fixed)],
        out_specs=pl.BlockSpec((tm, d), row),
        out_shape=jax.ShapeDtypeStruct((n, d), F32),
        compiler_params=_params(("parallel",), 56),
        name="merge",
    )(x2d, g_mix.reshape(1, d), o_gla, o_dsa, qm, kv, kv, w_gate, b_gate.reshape(1, -1), wg, wd, wm, wo)


def _route(z):
    lane = lax.broadcasted_iota(I32, z.shape, 1)
    lanef = lane.astype(F32)
    far = 1e9
    neg = -jnp.inf
    gmask = lane < N_GROUPS
    gmax = jnp.max(jnp.where(gmask, z, neg), axis=1, keepdims=True)
    gsum = jnp.sum(jnp.where(gmask, jnp.exp(z - gmax), 0.0), axis=1, keepdims=True)
    g_w = 1.0 / gsum
    g_sel = jnp.min(jnp.where(gmask & (z == gmax), lanef, far), axis=1, keepdims=True)
    lane_grp = ((lane - N_GROUPS) >> 2).astype(F32)
    emask = (lane >= N_GROUPS) & (lane < N_GROUPS + N_EXPERTS) & (lane_grp == g_sel)
    e1 = jnp.max(jnp.where(emask, z, neg), axis=1, keepdims=True)
    i1 = jnp.min(jnp.where(emask & (z == e1), lanef, far), axis=1, keepdims=True)
    emask2 = emask & (lanef != i1)
    e2 = jnp.max(jnp.where(emask2, z, neg), axis=1, keepdims=True)
    i2 = jnp.min(jnp.where(emask2 & (z == e2), lanef, far), axis=1, keepdims=True)
    t = jnp.exp(e2 - e1)
    p1 = 1.0 / (1.0 + t)
    p2 = t / (1.0 + t)
    return jnp.where(lanef == i1, g_w * p1, jnp.where(lanef == i2, g_w * p2, 0.0))


def _moe_body(x_ref, g_ref, wrh_ref, wrl_ref, br_ref, wg_ref, wu_ref, wd_ref, o_ref, h_scr, comb_scr):
    grp = pl.program_id(1)

    @pl.when(grp == 0)
    def _():
        x = x_ref[...]
        h = _rms_rows(x) * g_ref[...]
        hh = h.astype(BF16)
        hl = (h - hh.astype(F32)).astype(BF16)
        h_scr[...] = hh
        z = _dot(hh, wrh_ref[...]) + _dot(hl, wrh_ref[...]) + _dot(hh, wrl_ref[...]) + br_ref[...]
        comb_scr[...] = _route(z)
        o_ref[...] = x

    lane = lax.broadcasted_iota(I32, comb_scr.shape, 1)
    comb = comb_scr[...]
    h = h_scr[...]
    hidden = []
    for j in range(EXPERTS_PER_GROUP):
        first_lane = N_GROUPS + grp * EXPERTS_PER_GROUP + j
        ce = jnp.sum(jnp.where(lane == first_lane, comb, 0.0), axis=1, keepdims=True)
        gt = _dot(h, wg_ref[j])
        up = _dot(h, wu_ref[j])
        hidden.append((gt * _sigmoid(gt) * up * ce).astype(BF16))
    o_ref[...] += _dot(jnp.concatenate(hidden, axis=1), wd_ref[...])


def _moe_call(x1, g_ffn, wr_hi, wr_lo, br, wg, wu, wd, tm):
    n, d = x1.shape
    row = lambda i, e: (i, 0)
    fixed = lambda i, e: (0, 0)
    per_group = lambda i, e: (e, 0, 0)
    epg = EXPERTS_PER_GROUP
    wd = wd.reshape(N_EXPERTS * D_EXPERT, d)
    return pl.pallas_call(
        _moe_body,
        grid=(n // tm, N_GROUPS),
        in_specs=[pl.BlockSpec((tm, d), row), pl.BlockSpec((1, d), fixed),
                  pl.BlockSpec((d, LANE), fixed), pl.BlockSpec((d, LANE), fixed),
                  pl.BlockSpec((1, LANE), fixed),
                  pl.BlockSpec((epg, d, D_EXPERT), per_group), pl.BlockSpec((epg, d, D_EXPERT), per_group),
                  pl.BlockSpec((epg * D_EXPERT, d), lambda i, e: (e, 0))],
        out_specs=pl.BlockSpec((tm, d), row),
        out_shape=jax.ShapeDtypeStruct((n, d), F32),
        scratch_shapes=[pltpu.VMEM((tm, d), BF16), pltpu.VMEM((tm, LANE), F32)],
        compiler_params=_params(("parallel", "arbitrary"), 48),
        name="moe",
    )(x1, g_ffn.reshape(1, d), wr_hi, wr_lo, br, wg, wu, wd)


def _layer(x2d, mem2d, batch, seq, mem_len, g_mix, g_mem, w_in, w_gla_a2, b_gla_a2, gla_norm,
           w_mem_kv, dsa_q_norm, dsa_k_norm, idx_k_norm, mem_q_norm, mem_k_norm, b_gate,
           w_o_gla, w_o_dsa, w_o_mem, w_out, g_ffn, w_r1, b_r1, w_r2, b_r2, w_gate, w_up, w_down):
    n, d = x2d.shape
    sds = jax.ShapeDtypeStruct

    w_gla = w_in[:, _OFF["gq"][0]:_OFF["gg"][1]].astype(BF16)
    (zg,) = _proj_call("proj_gla", x2d, g_mix, w_gla, _epi_plain, [],
                       [sds((n, w_gla.shape[1]), BF16)], [1024], 1024, 1024, 40)

    (qn,) = _proj_call("proj_dsa_q", x2d, g_mix, _cols(w_in, "dq").astype(BF16),
                       functools.partial(_epi_headnorm, DSA_HEAD_DIM, DSA_HEAD_DIM ** -0.5 * LOG2_E),
                       [dsa_q_norm.reshape(1, -1)], [sds((n, 1024), BF16)], [1024], 1024, 1024, 40)
    (qm,) = _proj_call("proj_mem_q", x2d, g_mix, _cols(w_in, "mq").astype(BF16),
                       functools.partial(_epi_headnorm, MEM_HEAD_DIM, MEM_HEAD_DIM ** -0.5),
                       [mem_q_norm.reshape(1, -1)], [sds((n, 1024), BF16)], [1024], 1024, 1024, 40)

    w_kv = jnp.concatenate([_cols(w_in, "dk"), _cols(w_in, "dv"), _pad_cols(_cols(w_in, "ik"), LANE),
                            _pad_cols(_cols(w_in, "iw"), LANE), _pad_cols(_cols(w_in, "ga"), LANE)],
                           axis=1).astype(BF16)
    kn, vv, ki, wi, ga = _proj_call(
        "proj_dsa_kv", x2d, g_mix, w_kv, _epi_dsa_kv,
        [dsa_k_norm.reshape(1, -1), _pad_cols(idx_k_norm.reshape(1, -1), LANE)],
        [sds((n, 256), BF16), sds((n, 256), BF16), sds((n, LANE), BF16), sds((n, LANE), F32),
         sds((n, LANE), BF16)],
        [256, 256, LANE, LANE, LANE], 1024, w_kv.shape[1], 40)

    w_iq = _cols(w_in, "iq").reshape(d, IDX_HEADS, IDX_DIM)
    w_iq = jnp.pad(w_iq, ((0, 0), (0, 0), (0, LANE - IDX_DIM))).reshape(d, IDX_HEADS * LANE).astype(BF16)
    (qi,) = _proj_call("proj_idx_q", x2d, g_mix, w_iq, _epi_plain, [],
                       [sds((n, IDX_HEADS * LANE), BF16)], [1024], 1024, 1024, 40)

    wa = jnp.pad(w_gla_a2, ((0, LANE - GLA_GATE_RANK), (0, 0))).astype(BF16)
    o_gla = _gla_call(zg, ga, wa, b_gla_a2.reshape(1, -1), gla_norm.reshape(1, -1), batch, seq, 512)

    o_dsa = _dsa_call(qn, qi, wi, kn, vv, ki, batch, seq)

    (kv,) = _proj_call("proj_mem_kv", mem2d, g_mem, w_mem_kv.astype(BF16), _epi_mem_kv,
                       [mem_k_norm.reshape(1, -1)], [sds((mem2d.shape[0], 2 * d), BF16)], [1024],
                       mem2d.shape[0], 1024, 40)

    x1 = _merge_call(x2d, g_mix, o_gla, o_dsa, qm, kv, seq, mem_len, _cols(w_in, "gates").astype(BF16),
                     b_gate, w_o_gla.astype(BF16), w_o_dsa.astype(BF16), w_o_mem.astype(BF16),
                     w_out.astype(BF16), 256)

    w_r = _pad_cols(jnp.concatenate([w_r1, w_r2], axis=1), LANE)
    wr_hi = w_r.astype(BF16)
    wr_lo = (w_r - wr_hi.astype(F32)).astype(BF16)
    b_r = _pad_cols(jnp.concatenate([b_r1, b_r2]).reshape(1, -1), LANE)
    return _moe_call(x1, g_ffn, wr_hi, wr_lo, b_r, w_gate.astype(BF16), w_up.astype(BF16),
                     w_down.astype(BF16), 1024)


def kernel(x, mem, g_mix, g_mem, w_in, w_gla_a2, b_gla_a2, gla_norm, w_mem_kv, dsa_q_norm, dsa_k_norm, idx_k_norm, mem_q_norm, mem_k_norm, b_gate, w_o_gla, w_o_dsa, w_o_mem, w_out, g_ffn, w_r1, b_r1, w_r2, b_r2, w_gate, w_up, w_down):
    batch, seq, d = x.shape
    mem_len = mem.shape[1]
    x2d = x.reshape(batch * seq, d)
    mem2d = mem.reshape(batch * mem_len, d)
    params = (g_mix, g_mem, w_in, w_gla_a2, b_gla_a2, gla_norm, w_mem_kv, dsa_q_norm, dsa_k_norm,
              idx_k_norm, mem_q_norm, mem_k_norm, b_gate, w_o_gla, w_o_dsa, w_o_mem, w_out, g_ffn,
              w_r1, b_r1, w_r2, b_r2, w_gate, w_up, w_down)
    for layer in range(g_mix.shape[0]):
        x2d = _layer(x2d, mem2d, batch, seq, mem_len, *(p[layer] for p in params))
    return x2d.reshape(batch, seq, d)
```

```python
import functools

import jax
import jax.numpy as jnp
from jax import lax
from jax.experimental import pallas as pl
from jax.experimental.pallas import tpu as pltpu

F32 = jnp.float32
BF16 = jnp.bfloat16
I32 = jnp.int32

D_MODEL = 1024
EPS = 1e-6

GLA_HEADS = 4
GLA_DK = 128
GLA_DV = 256
GLA_GATE_RANK = 16
GLA_GATE_TEMP = 16.0
GLA_CHUNK = 64
GLA_UNROLL = 8

DSA_HEADS = 8
DSA_KV_HEADS = 2
DSA_HEAD_DIM = 128
IDX_HEADS = 8
IDX_DIM = 64
IDX_TOPK_MAX = 256

MEM_HEADS = 4
MEM_HEAD_DIM = 256

N_GROUPS = 4
EXPERTS_PER_GROUP = 4
N_EXPERTS = 16
D_EXPERT = 256

LANE = 128
MASKED_LOGIT = -1e30
LOG2_E = 1.4426950408889634

_OFF = {}
_acc = 0
for _name, _w in (("gq", 512), ("gk", 512), ("gv", 1024), ("gg", 1024), ("ga", 16), ("dq", 1024),
                  ("dk", 256), ("dv", 256), ("iq", 512), ("ik", 64), ("iw", 8), ("mq", 1024),
                  ("gates", 3072)):
    _OFF[_name] = (_acc, _acc + _w)
    _acc += _w


def _cols(w_in, name):
    lo, hi = _OFF[name]
    return w_in[:, lo:hi]


def _pad_cols(w, width):
    return jnp.pad(w, ((0, 0), (0, width - w.shape[1])))


def _rms_rows(x):
    return x * lax.rsqrt(jnp.mean(x * x, axis=-1, keepdims=True) + EPS)


def _dot(a, b):
    return jnp.dot(a, b, preferred_element_type=F32)


def _dot_nt(a, b):
    return lax.dot_general(a, b, (((1,), (1,)), ((), ())), preferred_element_type=F32)


def _dot_tn(a, b):
    return lax.dot_general(a, b, (((0,), (0,)), ((), ())), preferred_element_type=F32)


def _sigmoid(x):
    return 1.0 / (1.0 + jnp.exp(-x))


def _params(sem, vmem_mb):
    return pltpu.CompilerParams(dimension_semantics=sem, vmem_limit_bytes=vmem_mb * 1024 * 1024)


def _proj_body(epilogue, n_extra, x_ref, g_ref, w_ref, *rest):
    extra = rest[:n_extra]
    outs = rest[n_extra:-1]
    h_scr = rest[-1]

    @pl.when(pl.program_id(1) == 0)
    def _():
        h_scr[...] = (_rms_rows(x_ref[...]) * g_ref[...]).astype(BF16)

    epilogue(_dot(h_scr[...], w_ref[...]), extra, outs)


def _proj_call(name, x2d, gain, w, epilogue, extras, out_shapes, out_widths, tm, tn, vmem_mb):
    n, d = x2d.shape
    c = w.shape[1]
    in_specs = [pl.BlockSpec((tm, d), lambda i, j: (i, 0)),
                pl.BlockSpec((1, d), lambda i, j: (0, 0)),
                pl.BlockSpec((d, tn), lambda i, j: (0, j))]
    single = c == tn
    for e in extras:
        if e.shape[1] == c and not single:
            in_specs.append(pl.BlockSpec((1, tn), lambda i, j: (0, j)))
        else:
            in_specs.append(pl.BlockSpec(e.shape, lambda i, j: (0, 0)))
    out_specs = [pl.BlockSpec((tm, ow), (lambda i, j: (i, 0)) if single else (lambda i, j: (i, j)))
                 for ow in out_widths]
    return pl.pallas_call(
        functools.partial(_proj_body, epilogue, len(extras)),
        grid=(n // tm, c // tn),
        in_specs=in_specs,
        out_specs=out_specs,
        out_shape=out_shapes,
        scratch_shapes=[pltpu.VMEM((tm, d), BF16)],
        compiler_params=_params(("parallel", "arbitrary"), vmem_mb),
        name=name,
    )(x2d, gain.reshape(1, d), w, *extras)


def _epi_plain(acc, extra, outs):
    outs[0][...] = acc.astype(outs[0].dtype)


def _epi_headnorm(head_dim, scale, acc, extra, outs):
    gain = extra[0][...]
    for h in range(acc.shape[1] // head_dim):
        sl = slice(h * head_dim, (h + 1) * head_dim)
        outs[0][:, sl] = (_rms_rows(acc[:, sl]) * gain * scale).astype(outs[0].dtype)


def _epi_dsa_kv(acc, extra, outs):
    k_gain, idx_gain = extra
    kn_ref, v_ref, ki_ref, wi_ref, ga_ref = outs
    for h in range(DSA_KV_HEADS):
        sl = slice(h * DSA_HEAD_DIM, (h + 1) * DSA_HEAD_DIM)
        kn_ref[:, sl] = (_rms_rows(acc[:, sl]) * k_gain[...]).astype(BF16)
    v_ref[...] = acc[:, 256:512].astype(BF16)
    a = acc[:, 512:640]
    ms = jnp.sum(a * a, axis=-1, keepdims=True) * (1.0 / IDX_DIM)
    ki_ref[...] = (a * lax.rsqrt(ms + EPS) * idx_gain[...]).astype(BF16)
    wi_ref[...] = acc[:, 640:768] * (IDX_HEADS ** -0.5 * IDX_DIM ** -0.5)
    ga_ref[...] = acc[:, 768:896].astype(BF16)


def _epi_mem_kv(acc, extra, outs):
    gain = extra[0][...]
    j = pl.program_id(1)

    @pl.when(j == 0)
    def _():
        for h in range(MEM_HEADS):
            sl = slice(h * MEM_HEAD_DIM, (h + 1) * MEM_HEAD_DIM)
            outs[0][:, sl] = (_rms_rows(acc[:, sl]) * gain).astype(BF16)

    @pl.when(j == 1)
    def _():
        outs[0][...] = acc.astype(BF16)


def _split2(x):
    hi = x.astype(BF16)
    return hi, (x - hi.astype(F32)).astype(BF16)


def _gla_body(q_ref, k_ref, v_ref, g_ref, a_ref, wa_ref, ba_ref, ng_ref, o_ref, st_scr, la_scr):
    tb = q_ref.shape[0]
    c = GLA_CHUNK

    @pl.when(pl.program_id(1) == 0)
    def _():
        st_scr[...] = jnp.zeros_like(st_scr)

    pre = _dot(a_ref[...].astype(BF16), wa_ref[...]) + ba_ref[...]
    la_scr[...] = (jnp.minimum(pre, 0.0) - jnp.log1p(jnp.exp(-jnp.abs(pre)))) * (1.0 / GLA_GATE_TEMP)

    causal = lax.broadcasted_iota(I32, (c, c), 1) <= lax.broadcasted_iota(I32, (c, c), 0)
    tri = jnp.where(causal, 1.0, 0.0).astype(BF16)

    def chunk(ci):
        r0 = pl.multiple_of(ci * c, c)
        rows = pl.ds(r0, c)
        hi, lo = _split2(la_scr[rows, :])
        cum = _dot(tri, hi) + _dot(tri, lo)
        tot = cum[c - 1:c, :]
        kf = k_ref[rows, :]
        q_dec = (q_ref[rows, :] * (GLA_DK ** -0.5) * jnp.exp(cum)).astype(BF16)
        k_inv = (kf * jnp.exp(-cum)).astype(BF16)
        k_end = (kf * jnp.exp(tot - cum)).astype(BF16)
        decay = jnp.exp(tot)
        for h in range(GLA_HEADS):
            ks = slice(h * GLA_DK, (h + 1) * GLA_DK)
            vs = slice(h * GLA_DV, (h + 1) * GLA_DV)
            qd = q_dec[:, ks]
            vv = v_ref[rows, vs].astype(BF16)
            att = jnp.where(causal, _dot_nt(qd, k_inv[:, ks]), 0.0).astype(BF16)
            st = st_scr[h]
            o = _dot(att, vv) + _dot_nt(qd, st.astype(BF16))
            st_scr[h] = st * decay[:, ks] + _dot_tn(vv, k_end[:, ks])
            gg = g_ref[rows, vs].astype(F32)
            o_ref[rows, vs] = (_rms_rows(o) * ng_ref[...] * (gg * _sigmoid(gg))).astype(o_ref.dtype)

    def chunk_group(ji, carry):
        for u in range(GLA_UNROLL):
            chunk(GLA_UNROLL * ji + u)
        return carry

    lax.fori_loop(0, tb // (GLA_UNROLL * c), chunk_group, 0)


def _gla_call(zg, ga, wa, ba, ng, batch, seq, tb):
    n = zg.shape[0]
    nb = seq // tb
    row = lambda b, i: b * nb + i
    in_specs = [
        pl.BlockSpec((tb, 512), lambda b, i: (row(b, i), 0)),
        pl.BlockSpec((tb, 512), lambda b, i: (row(b, i), 1)),
        pl.BlockSpec((tb, 1024), lambda b, i: (row(b, i), 1)),
        pl.BlockSpec((tb, 1024), lambda b, i: (row(b, i), 2)),
        pl.BlockSpec((tb, LANE), lambda b, i: (row(b, i), 0)),
        pl.BlockSpec(wa.shape, lambda b, i: (0, 0)),
        pl.BlockSpec(ba.shape, lambda b, i: (0, 0)),
        pl.BlockSpec(ng.shape, lambda b, i: (0, 0)),
    ]
    hk = GLA_HEADS * GLA_DK
    return pl.pallas_call(
        _gla_body,
        grid=(batch, nb),
        in_specs=in_specs,
        out_specs=pl.BlockSpec((tb, GLA_HEADS * GLA_DV), lambda b, i: (row(b, i), 0)),
        out_shape=jax.ShapeDtypeStruct((n, GLA_HEADS * GLA_DV), BF16),
        scratch_shapes=[pltpu.VMEM((GLA_HEADS, GLA_DV, GLA_DK), F32),
                        pltpu.VMEM((tb, hk), F32)],
        compiler_params=_params(("parallel", "arbitrary"), 48),
        name="gla",
    )(zg, zg, zg, zg, ga, wa, ba, ng)


DSA_TQ = 256
DSA_TK = 512
DSA_COUNT_ROWS = 128
DSA_MAX_BISECT = 320
DSA_MIN_DENOM = 2.0 ** -100


def _dsa_body(topk, qn_ref, qi_ref, wi_ref, kn_ref, v_ref, ki_ref, o_ref,
              sc_scr, stat_scr, qa_scr, kmax_scr, s_scr, acc_scr):
    tq, tk, hd = DSA_TQ, DSA_TK, DSA_HEAD_DIM
    grp = DSA_HEADS // DSA_KV_HEADS
    kf = float(topk)
    q0 = pl.program_id(1) * tq
    nk = (q0 + tq + tk - 1) // tk
    qpos = q0 + lax.broadcasted_iota(I32, (tq, tk), 0)
    kofs = lax.broadcasted_iota(I32, (tq, tk), 1)
    inf = jnp.inf

    def chunk_start(ci):
        return pl.multiple_of(ci * tk, tk)

    @pl.when(pl.program_id(1) == 0)
    def _():
        def body(ci, mx):
            kc = kn_ref[pl.ds(chunk_start(ci), tk), :].astype(F32)
            return tuple(jnp.maximum(mx[g], jnp.sum(kc[:, g * hd:(g + 1) * hd] ** 2, axis=1, keepdims=True))
                         for g in range(DSA_KV_HEADS))

        mx = lax.fori_loop(0, kn_ref.shape[0] // tk, body,
                           tuple(jnp.zeros((tk, 1), F32) for _ in range(DSA_KV_HEADS)))
        for g in range(DSA_KV_HEADS):
            kmax_scr[g] = jnp.broadcast_to(jnp.max(mx[g], axis=0, keepdims=True), kmax_scr.shape[1:])

    wi = wi_ref[...]

    stat_scr[0] = jnp.full((tq, LANE), -inf, F32)
    stat_scr[1] = jnp.full((tq, LANE), inf, F32)
    stat_scr[2] = jnp.zeros((tq, LANE), F32)
    stat_scr[3] = jnp.zeros((tq, LANE), F32)

    def score_chunk(ci):
        k0 = chunk_start(ci)
        kc = ki_ref[pl.ds(k0, tk), :]
        score = jnp.zeros((tq, tk), F32)
        for h in range(IDX_HEADS):
            qh = qi_ref[:, h * LANE:(h + 1) * LANE]
            score = score + wi[:, h:h + 1] * jnp.maximum(_dot_nt(qh, kc), 0.0)
        valid = k0 + kofs <= qpos
        masked = jnp.where(valid, score, -inf)
        sc_scr[:, pl.ds(k0, tk)] = masked
        rmax, rmin, n_pos, n_nonneg = stat_scr[0], stat_scr[1], stat_scr[2], stat_scr[3]
        for t in range(tk // LANE):
            ts = slice(t * LANE, (t + 1) * LANE)
            rmax = jnp.maximum(rmax, masked[:, ts])
            rmin = jnp.minimum(rmin, jnp.where(valid[:, ts], score[:, ts], inf))
            n_pos = n_pos + jnp.where(masked[:, ts] > 0.0, 1.0, 0.0)
            n_nonneg = n_nonneg + jnp.where(masked[:, ts] >= 0.0, 1.0, 0.0)
        stat_scr[0], stat_scr[1], stat_scr[2], stat_scr[3] = rmax, rmin, n_pos, n_nonneg

    n_pairs = nk // 2
    odd_tail = nk % 2 == 1

    def score_pair(ji, carry):
        score_chunk(2 * ji)
        score_chunk(2 * ji + 1)
        return carry

    lax.fori_loop(0, n_pairs, score_pair, 0)
    pl.when(odd_tail)(lambda: score_chunk(nk - 1))
    hi0 = jnp.broadcast_to(jnp.max(stat_scr[0], axis=1, keepdims=True), (tq, LANE))
    lo0 = jnp.broadcast_to(jnp.min(stat_scr[1], axis=1, keepdims=True), (tq, LANE))
    ones_sum = jnp.ones((LANE, LANE), BF16)
    c_pos = _dot(stat_scr[2].astype(BF16), ones_sum)
    c_nn = _dot(stat_scr[3].astype(BF16), ones_sum)

    def count(*tests):
        partial = []
        for r0 in range(0, tq, DSA_COUNT_ROWS):
            rows = slice(r0, r0 + DSA_COUNT_ROWS)
            thr_b = [thr[rows] for _, thr in tests]

            def body(ci, cnts):
                sc = sc_scr[rows, pl.ds(chunk_start(ci), tk)]
                out = []
                for (pred, _), tb, cnt in zip(tests, thr_b, cnts):
                    for t in range(tk // LANE):
                        cnt = cnt + jnp.where(pred(sc[:, t * LANE:(t + 1) * LANE], tb), 1.0, 0.0)
                    out.append(cnt)
                return tuple(out)

            partial.append(lax.fori_loop(0, nk, body,
                                         tuple(jnp.zeros((DSA_COUNT_ROWS, LANE), F32) for _ in tests)))
        return [jnp.concatenate([_dot(p[k].astype(BF16), ones_sum) for p in partial], axis=0)
                for k in range(len(tests))]

    ge_pred = lambda a, b: a >= b
    n_causal = (q0 + 1 + lax.broadcasted_iota(I32, (tq, LANE), 0)).astype(F32)
    few = n_causal <= kf
    zero_tied = (c_pos < kf) & (c_nn >= kf)
    lo_init = jnp.where(few, lo0, jnp.where(c_nn >= kf, 0.0, lo0))
    hi_init = jnp.where(few | zero_tied, lo_init, jnp.where(c_nn < kf, 0.0, hi0))

    def bisect_cond(carry):
        it, _, _, pending = carry
        return (it < DSA_MAX_BISECT) & (pending > 0.0)

    def bisect(carry):
        it, lo, hi, _ = carry
        mid = 0.5 * lo + 0.5 * hi
        (cnt,) = count((ge_pred, mid))
        pending = jnp.max(jnp.where(mid > lo, jnp.where(mid < hi, 1.0, 0.0), 0.0))
        lo = jnp.where(cnt >= kf, mid, lo)
        hi = jnp.where(cnt > kf, hi, mid)
        return it + 1, lo, hi, pending

    _, thr, _, _ = lax.while_loop(bisect_cond, bisect, (jnp.int32(0), lo_init, hi_init, jnp.float32(1.0)))
    thr_w = jnp.tile(thr, (1, tk // LANE))
    (c_ge,) = count((ge_pred, thr))

    def break_ties(rows):
        nr = DSA_COUNT_ROWS
        thr_g, c_ge_g = thr_w[rows], c_ge[rows]

        def low_body(ci, low):
            sc = sc_scr[rows, pl.ds(chunk_start(ci), tk)]
            cand = jnp.where(sc >= thr_g, sc, inf)
            for t in range(tk // LANE):
                low = jnp.minimum(low, cand[:, t * LANE:(t + 1) * LANE])
            return low

        low = lax.fori_loop(0, nk, low_body, jnp.full((nr, LANE), inf, F32))
        tied_w = jnp.broadcast_to(jnp.min(low, axis=1, keepdims=True), (nr, tk))

        def eq_body(ci, cnt):
            sc = sc_scr[rows, pl.ds(chunk_start(ci), tk)]
            for t in range(tk // LANE):
                cnt = cnt + jnp.where(sc[:, t * LANE:(t + 1) * LANE] == tied_w[:, :LANE], 1.0, 0.0)
            return cnt

        c_eq = _dot(lax.fori_loop(0, nk, eq_body, jnp.zeros((nr, LANE), F32)).astype(BF16), ones_sum)
        allowed = jnp.tile(kf - (c_ge_g - c_eq), (1, tk // LANE))
        upper = jnp.where(lax.broadcasted_iota(I32, (tk, tk), 0) <= lax.broadcasted_iota(I32, (tk, tk), 1),
                          1.0, 0.0).astype(BF16)

        def body(ci, seen):
            k0 = chunk_start(ci)
            sc = sc_scr[rows, pl.ds(k0, tk)]
            eq = jnp.where(sc == tied_w, 1.0, 0.0).astype(BF16)
            rank = seen + _dot(eq, upper)
            drop = jnp.where(sc == tied_w, jnp.where(rank > allowed, 1.0, 0.0), 0.0) > 0.0
            sc_scr[rows, pl.ds(k0, tk)] = jnp.where(drop, -inf, sc)
            return jnp.broadcast_to(rank[:, tk - 1:tk], (nr, tk))

        lax.fori_loop(0, nk, body, jnp.zeros((nr, tk), F32))

    for r0 in range(0, tq, DSA_COUNT_ROWS):
        rows = slice(r0, r0 + DSA_COUNT_ROWS)
        pl.when(jnp.max(c_ge[rows]) > kf)(functools.partial(break_ties, rows))

    lane_q = lax.broadcasted_iota(I32, (tq, hd), 1)
    for g in range(DSA_KV_HEADS):
        kmax2 = kmax_scr[g][0:1, :]
        for r in range(grp):
            hcol = (g * grp + r) * hd
            rows = slice(r * tq, (r + 1) * tq)
            qh = qn_ref[:, hcol:hcol + hd]
            qf = qh.astype(F32)
            bound = jnp.sqrt(jnp.sum(qf * qf, axis=1, keepdims=True) * kmax2)
            qa_scr[g, rows, 0:hd] = qh
            qa_scr[g, rows, hd:2 * hd] = jnp.where(lane_q == 0, bound, 0.0).astype(BF16)
    acc_scr[...] = jnp.zeros_like(acc_scr)
    lane_k = lax.broadcasted_iota(I32, (tk, hd), 1)
    k_tail = jnp.where(lane_k == 0, -1.0, 0.0).astype(BF16)
    v_tail = jnp.where(lane_k == 0, 1.0, 0.0).astype(BF16)

    def sel_bias(k0):
        return jnp.where(sc_scr[:, pl.ds(k0, tk)] >= thr_w, 0.0, -inf)

    def logits(ci, g):
        ka = jnp.concatenate([kn_ref[pl.ds(chunk_start(ci), tk), g * hd:(g + 1) * hd], k_tail], axis=1)
        return _dot_nt(qa_scr[g], ka)

    def weighted_values(ci, g, s, bias):
        va = jnp.concatenate([v_ref[pl.ds(chunk_start(ci), tk), g * hd:(g + 1) * hd], v_tail], axis=1)
        p = jnp.exp2((s.reshape(grp, tq, tk) + bias[None]).reshape(grp * tq, tk))
        return _dot(p.astype(BF16), va)

    last_chunk = kn_ref.shape[0] // tk - 1
    for g in range(DSA_KV_HEADS):
        s_scr[g] = logits(0, g)

    def attend(ji, carry):
        ca = 2 * ji
        cb = ca + 1
        bias_a = sel_bias(chunk_start(ca))
        bias_b = sel_bias(chunk_start(cb))
        for g in range(DSA_KV_HEADS):
            s_a = s_scr[g]
            s_b = logits(cb, g)
            pv_a = weighted_values(ca, g, s_a, bias_a)
            s_scr[g] = logits(jnp.minimum(ca + 2, last_chunk), g)
            acc_scr[g] += pv_a + weighted_values(cb, g, s_b, bias_b)
        return carry

    lax.fori_loop(0, n_pairs, attend, 0)

    @pl.when(odd_tail)
    def _():
        bias = sel_bias(chunk_start(nk - 1))
        for g in range(DSA_KV_HEADS):
            acc_scr[g] += weighted_values(nk - 1, g, s_scr[g], bias)

    def write_out(g, out):
        for r in range(grp):
            hcol = (g * grp + r) * hd
            o_ref[:, hcol:hcol + hd] = out[r * tq:(r + 1) * tq, :].astype(o_ref.dtype)

    l_min = inf
    for g in range(DSA_KV_HEADS):
        acc = acc_scr[g]
        denom = acc[:, hd:hd + 1]
        l_min = jnp.minimum(l_min, jnp.min(denom))
        write_out(g, acc[:, 0:hd] / denom)

    @pl.when(jnp.logical_not(l_min >= DSA_MIN_DENOM))
    def _():
        for g in range(DSA_KV_HEADS):
            s_scr[g, :, 0:LANE] = jnp.full((grp * tq, LANE), MASKED_LOGIT, F32)
        acc_scr[...] = jnp.zeros_like(acc_scr)

        def attend_online(ci, carry):
            k0 = chunk_start(ci)
            bias = sel_bias(k0)
            for g in range(DSA_KV_HEADS):
                hs = slice(g * hd, (g + 1) * hd)
                s = _dot_nt(qa_scr[g, :, 0:hd], kn_ref[pl.ds(k0, tk), hs])
                s = (s.reshape(grp, tq, tk) + bias[None]).reshape(grp * tq, tk)
                m_prev = s_scr[g, :, 0:LANE]
                m_new = jnp.maximum(m_prev, jnp.max(s, axis=1, keepdims=True))
                p = jnp.exp2(s - jnp.tile(m_new, (1, tk // LANE)))
                alpha = jnp.exp2(m_prev - m_new)
                acc_scr[g, :, hd:2 * hd] = alpha * acc_scr[g, :, hd:2 * hd] + jnp.sum(p, axis=1, keepdims=True)
                acc_scr[g, :, 0:hd] = alpha * acc_scr[g, :, 0:hd] + _dot(p.astype(BF16), v_ref[pl.ds(k0, tk), hs])
                s_scr[g, :, 0:LANE] = m_new
            return carry

        lax.fori_loop(0, nk, attend_online, 0)
        for g in range(DSA_KV_HEADS):
            write_out(g, acc_scr[g, :, 0:hd] / acc_scr[g, :, hd:2 * hd])


def _dsa_call(qn, qi, wi, kn, vv, ki, batch, seq):
    n = qn.shape[0]
    tq, hd = DSA_TQ, DSA_HEAD_DIM
    nb = seq // tq
    topk = min(IDX_TOPK_MAX, seq // 4)
    grp = DSA_HEADS // DSA_KV_HEADS
    assert seq // LANE <= 256, "per-lane partial counts must stay exact in bf16"
    assert seq % DSA_TK == 0 and seq % tq == 0 and tq % DSA_COUNT_ROWS == 0
    row = lambda b, i: (b * nb + i, 0)
    per_batch = lambda b, i: (b, 0)
    return pl.pallas_call(
        functools.partial(_dsa_body, topk),
        grid=(batch, nb),
        in_specs=[pl.BlockSpec((tq, qn.shape[1]), row),
                  pl.BlockSpec((tq, qi.shape[1]), row),
                  pl.BlockSpec((tq, LANE), row),
                  pl.BlockSpec((seq, kn.shape[1]), per_batch, pipeline_mode=pl.Buffered(1)),
                  pl.BlockSpec((seq, vv.shape[1]), per_batch, pipeline_mode=pl.Buffered(1)),
                  pl.BlockSpec((seq, LANE), per_batch, pipeline_mode=pl.Buffered(1))],
        out_specs=pl.BlockSpec((tq, DSA_HEADS * hd), row),
        out_shape=jax.ShapeDtypeStruct((n, DSA_HEADS * hd), BF16),
        scratch_shapes=[pltpu.VMEM((tq, seq), F32),
                        pltpu.VMEM((4, tq, LANE), F32),
                        pltpu.VMEM((DSA_KV_HEADS, grp * tq, 2 * hd), BF16),
                        pltpu.VMEM((DSA_KV_HEADS, 8, LANE), F32),
                        pltpu.VMEM((DSA_KV_HEADS, grp * tq, DSA_TK), F32),
                        pltpu.VMEM((DSA_KV_HEADS, grp * tq, 2 * hd), F32)],
        compiler_params=_params(("arbitrary", "arbitrary"), 48),
        name="dsa",
    )(qn, qi, wi, kn, vv, ki)


def _mem_attention(q_ref, k_ref, v_ref):
    heads = []
    for h in range(MEM_HEADS):
        sl = slice(h * MEM_HEAD_DIM, (h + 1) * MEM_HEAD_DIM)
        s = _dot_nt(q_ref[:, sl], k_ref[:, sl])
        p = jnp.exp(s - jnp.max(s, axis=1, keepdims=True))
        o = _dot(p.astype(BF16), v_ref[:, sl]) / jnp.sum(p, axis=1, keepdims=True)
        heads.append(o.astype(BF16))
    return jnp.concatenate(heads, axis=1)


def _merge_body(x_ref, gmix_ref, og_ref, od_ref, qm_ref, km_ref, vm_ref, wgate_ref, bgate_ref,
                wg_ref, wd_ref, wm_ref, wo_ref, o_ref):
    d = D_MODEL
    x = x_ref[...]
    h = (_rms_rows(x) * gmix_ref[...]).astype(BF16)
    o_mem = _mem_attention(qm_ref, km_ref, vm_ref)
    branches = (_dot(og_ref[...], wg_ref[...]), _dot(od_ref[...], wd_ref[...]), _dot(o_mem, wm_ref[...]))
    merged = None
    for j, br in enumerate(branches):
        cols = slice(j * d, (j + 1) * d)
        gate = _sigmoid(_dot(h, wgate_ref[:, cols]) + bgate_ref[:, cols])
        merged = gate * br if merged is None else merged + gate * br
    o_ref[...] = x + _dot(merged.astype(BF16), wo_ref[...])


def _merge_call(x2d, g_mix, o_gla, o_dsa, qm, kv, seq, mem_len, w_gate, b_gate, wg, wd, wm, wo, tm):
    n, d = x2d.shape
    row = lambda i: (i, 0)
    fixed = lambda i: (0, 0)
    per_batch = seq // tm
    return pl.pallas_call(
        _merge_body,
        grid=(n // tm,),
        in_specs=[pl.BlockSpec((tm, d), row), pl.BlockSpec((1, d), fixed),
                  pl.BlockSpec((tm, d), row), pl.BlockSpec((tm, d), row), pl.BlockSpec((tm, d), row),
                  pl.BlockSpec((mem_len, d), lambda i: (i // per_batch, 0)),
                  pl.BlockSpec((mem_len, d), lambda i: (i // per_batch, 1)),
                  pl.BlockSpec((d, 3 * d), fixed), pl.BlockSpec((1, 3 * d), fixed),
                  pl.BlockSpec((d, d), fixed), pl.BlockSpec((d, d), fixed),
                  pl.BlockSpec((d, d), fixed), pl.BlockSpec((d, d), fixed)],
        out_specs=pl.BlockSpec((tm, d), row),
        out_shape=jax.ShapeDtypeStruct((n, d), F32),
        compiler_params=_params(("parallel",), 56),
        name="merge",
    )(x2d, g_mix.reshape(1, d), o_gla, o_dsa, qm, kv, kv, w_gate, b_gate.reshape(1, -1), wg, wd, wm, wo)


def _route(z):
    lane = lax.broadcasted_iota(I32, z.shape, 1)
    lanef = lane.astype(F32)
    far = 1e9
    neg = -jnp.inf
    gmask = lane < N_GROUPS
    gmax = jnp.max(jnp.where(gmask, z, neg), axis=1, keepdims=True)
    gsum = jnp.sum(jnp.where(gmask, jnp.exp(z - gmax), 0.0), axis=1, keepdims=True)
    g_w = 1.0 / gsum
    g_sel = jnp.min(jnp.where(gmask & (z == gmax), lanef, far), axis=1, keepdims=True)
    lane_grp = ((lane - N_GROUPS) >> 2).astype(F32)
    emask = (lane >= N_GROUPS) & (lane < N_GROUPS + N_EXPERTS) & (lane_grp == g_sel)
    e1 = jnp.max(jnp.where(emask, z, neg), axis=1, keepdims=True)
    i1 = jnp.min(jnp.where(emask & (z == e1), lanef, far), axis=1, keepdims=True)
    emask2 = emask & (lanef != i1)
    e2 = jnp.max(jnp.where(emask2, z, neg), axis=1, keepdims=True)
    i2 = jnp.min(jnp.where(emask2 & (z == e2), lanef, far), axis=1, keepdims=True)
    t = jnp.exp(e2 - e1)
    p1 = 1.0 / (1.0 + t)
    p2 = t / (1.0 + t)
    return jnp.where(lanef == i1, g_w * p1, jnp.where(lanef == i2, g_w * p2, 0.0))


def _moe_body(x_ref, g_ref, wrh_ref, wrl_ref, br_ref, wg_ref, wu_ref, wd_ref, o_ref, h_scr, comb_scr):
    grp = pl.program_id(1)

    @pl.when(grp == 0)
    def _():
        x = x_ref[...]
        h = _rms_rows(x) * g_ref[...]
        hh = h.astype(BF16)
        hl = (h - hh.astype(F32)).astype(BF16)
        h_scr[...] = hh
        z = _dot(hh, wrh_ref[...]) + _dot(hl, wrh_ref[...]) + _dot(hh, wrl_ref[...]) + br_ref[...]
        comb_scr[...] = _route(z)
        o_ref[...] = x

    lane = lax.broadcasted_iota(I32, comb_scr.shape, 1)
    comb = comb_scr[...]
    h = h_scr[...]
    hidden = []
    for j in range(EXPERTS_PER_GROUP):
        first_lane = N_GROUPS + grp * EXPERTS_PER_GROUP + j
        ce = jnp.sum(jnp.where(lane == first_lane, comb, 0.0), axis=1, keepdims=True)
        gt = _dot(h, wg_ref[j])
        up = _dot(h, wu_ref[j])
        hidden.append((gt * _sigmoid(gt) * up * ce).astype(BF16))
    o_ref[...] += _dot(jnp.concatenate(hidden, axis=1), wd_ref[...])


def _moe_call(x1, g_ffn, wr_hi, wr_lo, br, wg, wu, wd, tm):
    n, d = x1.shape
    row = lambda i, e: (i, 0)
    fixed = lambda i, e: (0, 0)
    per_group = lambda i, e: (e, 0, 0)
    epg = EXPERTS_PER_GROUP
    wd = wd.reshape(N_EXPERTS * D_EXPERT, d)
    return pl.pallas_call(
        _moe_body,
        grid=(n // tm, N_GROUPS),
        in_specs=[pl.BlockSpec((tm, d), row), pl.BlockSpec((1, d), fixed),
                  pl.BlockSpec((d, LANE), fixed), pl.BlockSpec((d, LANE), fixed),
                  pl.BlockSpec((1, LANE), fixed),
                  pl.BlockSpec((epg, d, D_EXPERT), per_group), pl.BlockSpec((epg, d, D_EXPERT), per_group),
                  pl.BlockSpec((epg * D_EXPERT, d), lambda i, e: (e, 0))],
        out_specs=pl.BlockSpec((tm, d), row),
        out_shape=jax.ShapeDtypeStruct((n, d), F32),
        scratch_shapes=[pltpu.VMEM((tm, d), BF16), pltpu.VMEM((tm, LANE), F32)],
        compiler_params=_params(("parallel", "arbitrary"), 48),
        name="moe",
    )(x1, g_ffn.reshape(1, d), wr_hi, wr_lo, br, wg, wu, wd)


def _layer(x2d, mem2d, batch, seq, mem_len, g_mix, g_mem, w_in, w_gla_a2, b_gla_a2, gla_norm,
           w_mem_kv, dsa_q_norm, dsa_k_norm, idx_k_norm, mem_q_norm, mem_k_norm, b_gate,
           w_o_gla, w_o_dsa, w_o_mem, w_out, g_ffn, w_r1, b_r1, w_r2, b_r2, w_gate, w_up, w_down):
    n, d = x2d.shape
    sds = jax.ShapeDtypeStruct

    w_gla = w_in[:, _OFF["gq"][0]:_OFF["gg"][1]].astype(BF16)
    (zg,) = _proj_call("proj_gla", x2d, g_mix, w_gla, _epi_plain, [],
                       [sds((n, w_gla.shape[1]), BF16)], [1024], 1024, 1024, 40)

    (qn,) = _proj_call("proj_dsa_q", x2d, g_mix, _cols(w_in, "dq").astype(BF16),
                       functools.partial(_epi_headnorm, DSA_HEAD_DIM, DSA_HEAD_DIM ** -0.5 * LOG2_E),
                       [dsa_q_norm.reshape(1, -1)], [sds((n, 1024), BF16)], [1024], 1024, 1024, 40)
    (qm,) = _proj_call("proj_mem_q", x2d, g_mix, _cols(w_in, "mq").astype(BF16),
                       functools.partial(_epi_headnorm, MEM_HEAD_DIM, MEM_HEAD_DIM ** -0.5),
                       [mem_q_norm.reshape(1, -1)], [sds((n, 1024), BF16)], [1024], 1024, 1024, 40)

    w_kv = jnp.concatenate([_cols(w_in, "dk"), _cols(w_in, "dv"), _pad_cols(_cols(w_in, "ik"), LANE),
                            _pad_cols(_cols(w_in, "iw"), LANE), _pad_cols(_cols(w_in, "ga"), LANE)],
                           axis=1).astype(BF16)
    kn, vv, ki, wi, ga = _proj_call(
        "proj_dsa_kv", x2d, g_mix, w_kv, _epi_dsa_kv,
        [dsa_k_norm.reshape(1, -1), _pad_cols(idx_k_norm.reshape(1, -1), LANE)],
        [sds((n, 256), BF16), sds((n, 256), BF16), sds((n, LANE), BF16), sds((n, LANE), F32),
         sds((n, LANE), BF16)],
        [256, 256, LANE, LANE, LANE], 1024, w_kv.shape[1], 40)

    w_iq = _cols(w_in, "iq").reshape(d, IDX_HEADS, IDX_DIM)
    w_iq = jnp.pad(w_iq, ((0, 0), (0, 0), (0, LANE - IDX_DIM))).reshape(d, IDX_HEADS * LANE).astype(BF16)
    (qi,) = _proj_call("proj_idx_q", x2d, g_mix, w_iq, _epi_plain, [],
                       [sds((n, IDX_HEADS * LANE), BF16)], [1024], 1024, 1024, 40)

    wa = jnp.pad(w_gla_a2, ((0, LANE - GLA_GATE_RANK), (0, 0))).astype(BF16)
    o_gla = _gla_call(zg, ga, wa, b_gla_a2.reshape(1, -1), gla_norm.reshape(1, -1), batch, seq, 512)

    o_dsa = _dsa_call(qn, qi, wi, kn, vv, ki, batch, seq)

    (kv,) = _proj_call("proj_mem_kv", mem2d, g_mem, w_mem_kv.astype(BF16), _epi_mem_kv,
                       [mem_k_norm.reshape(1, -1)], [sds((mem2d.shape[0], 2 * d), BF16)], [1024],
                       mem2d.shape[0], 1024, 40)

    x1 = _merge_call(x2d, g_mix, o_gla, o_dsa, qm, kv, seq, mem_len, _cols(w_in, "gates").astype(BF16),
                     b_gate, w_o_gla.astype(BF16), w_o_dsa.astype(BF16), w_o_mem.astype(BF16),
                     w_out.astype(BF16), 256)

    w_r = _pad_cols(jnp.concatenate([w_r1, w_r2], axis=1), LANE)
    wr_hi = w_r.astype(BF16)
    wr_lo = (w_r - wr_hi.astype(F32)).astype(BF16)
    b_r = _pad_cols(jnp.concatenate([b_r1, b_r2]).reshape(1, -1), LANE)
    return _moe_call(x1, g_ffn, wr_hi, wr_lo, b_r, w_gate.astype(BF16), w_up.astype(BF16),
                     w_down.astype(BF16), 1024)


def kernel(x, mem, g_mix, g_mem, w_in, w_gla_a2, b_gla_a2, gla_norm, w_mem_kv, dsa_q_norm, dsa_k_norm, idx_k_norm, mem_q_norm, mem_k_norm, b_gate, w_o_gla, w_o_dsa, w_o_mem, w_out, g_ffn, w_r1, b_r1, w_r2, b_r2, w_gate, w_up, w_down):
    batch, seq, d = x.shape
    mem_len = mem.shape[1]
    x2d = x.reshape(batch * seq, d)
    mem2d = mem.reshape(batch * mem_len, d)
    params = (g_mix, g_mem, w_in, w_gla_a2, b_gla_a2, gla_norm, w_mem_kv, dsa_q_norm, dsa_k_norm,
              idx_k_norm, mem_q_norm, mem_k_norm, b_gate, w_o_gla, w_o_dsa, w_o_mem, w_out, g_ffn,
              w_r1, b_r1, w_r2, b_r2, w_gate, w_up, w_down)
    for layer in range(g_mix.shape[0]):
        x2d = _layer(x2d, mem2d, batch, seq, mem_len, *(p[layer] for p in params))
    return x2d.reshape(batch, seq, d)
```

```python
import functools

import jax
import jax.numpy as jnp
from jax import lax
from jax.experimental import pallas as pl
from jax.experimental.pallas import tpu as pltpu

F32 = jnp.float32
BF16 = jnp.bfloat16
I32 = jnp.int32

D_MODEL = 1024
EPS = 1e-6

GLA_HEADS = 4
GLA_DK = 128
GLA_DV = 256
GLA_GATE_RANK = 16
GLA_GATE_TEMP = 16.0
GLA_CHUNK = 64
GLA_UNROLL = 8

DSA_HEADS = 8
DSA_KV_HEADS = 2
DSA_HEAD_DIM = 128
IDX_HEADS = 8
IDX_DIM = 64
IDX_TOPK_MAX = 256

MEM_HEADS = 4
MEM_HEAD_DIM = 256

N_GROUPS = 4
EXPERTS_PER_GROUP = 4
N_EXPERTS = 16
D_EXPERT = 256

LANE = 128
MASKED_LOGIT = -1e30
LOG2_E = 1.4426950408889634

_OFF = {}
_acc = 0
for _name, _w in (("gq", 512), ("gk", 512), ("gv", 1024), ("gg", 1024), ("ga", 16), ("dq", 1024),
                  ("dk", 256), ("dv", 256), ("iq", 512), ("ik", 64), ("iw", 8), ("mq", 1024),
                  ("gates", 3072)):
    _OFF[_name] = (_acc, _acc + _w)
    _acc += _w


def _cols(w_in, name):
    lo, hi = _OFF[name]
    return w_in[:, lo:hi]


def _pad_cols(w, width):
    return jnp.pad(w, ((0, 0), (0, width - w.shape[1])))


def _rms_rows(x):
    return x * lax.rsqrt(jnp.mean(x * x, axis=-1, keepdims=True) + EPS)


def _dot(a, b):
    return jnp.dot(a, b, preferred_element_type=F32)


def _dot_nt(a, b):
    return lax.dot_general(a, b, (((1,), (1,)), ((), ())), preferred_element_type=F32)


def _dot_tn(a, b):
    return lax.dot_general(a, b, (((0,), (0,)), ((), ())), preferred_element_type=F32)


def _sigmoid(x):
    return 1.0 / (1.0 + jnp.exp(-x))


def _params(sem, vmem_mb):
    return pltpu.CompilerParams(dimension_semantics=sem, vmem_limit_bytes=vmem_mb * 1024 * 1024)


def _proj_body(epilogue, n_extra, x_ref, g_ref, w_ref, *rest):
    extra = rest[:n_extra]
    outs = rest[n_extra:-1]
    h_scr = rest[-1]

    @pl.when(pl.program_id(1) == 0)
    def _():
        h_scr[...] = (_rms_rows(x_ref[...]) * g_ref[...]).astype(BF16)

    epilogue(_dot(h_scr[...], w_ref[...]), extra, outs)


def _proj_call(name, x2d, gain, w, epilogue, extras, out_shapes, out_widths, tm, tn, vmem_mb,
               out_follows_cols=True):
    n, d = x2d.shape
    c = w.shape[1]
    in_specs = [pl.BlockSpec((tm, d), lambda i, j: (i, 0)),
                pl.BlockSpec((1, d), lambda i, j: (0, 0)),
                pl.BlockSpec((d, tn), lambda i, j: (0, j))]
    single = c == tn
    for e in extras:
        if e.shape[1] == c and not single:
            in_specs.append(pl.BlockSpec((1, tn), lambda i, j: (0, j)))
        else:
            in_specs.append(pl.BlockSpec(e.shape, lambda i, j: (0, 0)))
    by_col = out_follows_cols and not single
    out_specs = [pl.BlockSpec((tm, ow), (lambda i, j: (i, j)) if by_col else (lambda i, j: (i, 0)))
                 for ow in out_widths]
    return pl.pallas_call(
        functools.partial(_proj_body, epilogue, len(extras)),
        grid=(n // tm, c // tn),
        in_specs=in_specs,
        out_specs=out_specs,
        out_shape=out_shapes,
        scratch_shapes=[pltpu.VMEM((tm, d), BF16)],
        compiler_params=_params(("parallel", "arbitrary"), vmem_mb),
        name=name,
    )(x2d, gain.reshape(1, d), w, *extras)


def _epi_plain(acc, extra, outs):
    outs[0][...] = acc.astype(outs[0].dtype)


def _epi_headnorm(head_dim, scale, acc, extra, outs):
    gain = extra[0][...]
    for h in range(acc.shape[1] // head_dim):
        sl = slice(h * head_dim, (h + 1) * head_dim)
        outs[0][:, sl] = (_rms_rows(acc[:, sl]) * gain * scale).astype(outs[0].dtype)


def _epi_queries(acc, extra, outs):
    dsa_gain, mem_gain = extra
    j = pl.program_id(1)
    pl.when(j == 0)(functools.partial(_epi_headnorm, DSA_HEAD_DIM, DSA_HEAD_DIM ** -0.5 * LOG2_E,
                                      acc, [dsa_gain], [outs[0]]))
    pl.when(j == 1)(functools.partial(_epi_headnorm, MEM_HEAD_DIM, MEM_HEAD_DIM ** -0.5,
                                      acc, [mem_gain], [outs[1]]))
    pl.when(j == 2)(functools.partial(_epi_plain, acc, [], [outs[2]]))


def _epi_dsa_kv(acc, extra, outs):
    k_gain, idx_gain = extra
    kn_ref, v_ref, ki_ref, wi_ref, ga_ref = outs
    for h in range(DSA_KV_HEADS):
        sl = slice(h * DSA_HEAD_DIM, (h + 1) * DSA_HEAD_DIM)
        kn_ref[:, sl] = (_rms_rows(acc[:, sl]) * k_gain[...]).astype(BF16)
    v_ref[...] = acc[:, 256:512].astype(BF16)
    a = acc[:, 512:640]
    ms = jnp.sum(a * a, axis=-1, keepdims=True) * (1.0 / IDX_DIM)
    ki_ref[...] = (a * lax.rsqrt(ms + EPS) * idx_gain[...]).astype(BF16)
    wi_ref[...] = acc[:, 640:768] * (IDX_HEADS ** -0.5 * IDX_DIM ** -0.5)
    ga_ref[...] = acc[:, 768:896].astype(BF16)


def _epi_mem_kv(acc, extra, outs):
    gain = extra[0][...]
    j = pl.program_id(1)

    @pl.when(j == 0)
    def _():
        for h in range(MEM_HEADS):
            sl = slice(h * MEM_HEAD_DIM, (h + 1) * MEM_HEAD_DIM)
            outs[0][:, sl] = (_rms_rows(acc[:, sl]) * gain).astype(BF16)

    @pl.when(j == 1)
    def _():
        outs[0][...] = acc.astype(BF16)


def _split2(x):
    hi = x.astype(BF16)
    return hi, (x - hi.astype(F32)).astype(BF16)


def _gla_body(q_ref, k_ref, v_ref, g_ref, a_ref, wa_ref, ba_ref, ng_ref, o_ref, st_scr, la_scr):
    tb = q_ref.shape[0]
    c = GLA_CHUNK

    @pl.when(pl.program_id(1) == 0)
    def _():
        st_scr[...] = jnp.zeros_like(st_scr)

    pre = _dot(a_ref[...].astype(BF16), wa_ref[...]) + ba_ref[...]
    la_scr[...] = (jnp.minimum(pre, 0.0) - jnp.log1p(jnp.exp(-jnp.abs(pre)))) * (1.0 / GLA_GATE_TEMP)

    causal = lax.broadcasted_iota(I32, (c, c), 1) <= lax.broadcasted_iota(I32, (c, c), 0)
    tri = jnp.where(causal, 1.0, 0.0).astype(BF16)

    def chunk(ci):
        r0 = pl.multiple_of(ci * c, c)
        rows = pl.ds(r0, c)
        hi, lo = _split2(la_scr[rows, :])
        cum = _dot(tri, hi) + _dot(tri, lo)
        tot = cum[c - 1:c, :]
        kf = k_ref[rows, :]
        q_dec = (q_ref[rows, :] * (GLA_DK ** -0.5) * jnp.exp(cum)).astype(BF16)
        k_inv = (kf * jnp.exp(-cum)).astype(BF16)
        k_end = (kf * jnp.exp(tot - cum)).astype(BF16)
        decay = jnp.exp(tot)
        for h in range(GLA_HEADS):
            ks = slice(h * GLA_DK, (h + 1) * GLA_DK)
            vs = slice(h * GLA_DV, (h + 1) * GLA_DV)
            qd = q_dec[:, ks]
            vv = v_ref[rows, vs].astype(BF16)
            att = jnp.where(causal, _dot_nt(qd, k_inv[:, ks]), 0.0).astype(BF16)
            st = st_scr[h]
            o = _dot(att, vv) + _dot_nt(qd, st.astype(BF16))
            st_scr[h] = st * decay[:, ks] + _dot_tn(vv, k_end[:, ks])
            gg = g_ref[rows, vs].astype(F32)
            o_ref[rows, vs] = (_rms_rows(o) * ng_ref[...] * (gg * _sigmoid(gg))).astype(o_ref.dtype)

    def chunk_group(ji, carry):
        for u in range(GLA_UNROLL):
            chunk(GLA_UNROLL * ji + u)
        return carry

    lax.fori_loop(0, tb // (GLA_UNROLL * c), chunk_group, 0)


def _gla_call(zg, ga, wa, ba, ng, batch, seq, tb):
    n = zg.shape[0]
    nb = seq // tb
    row = lambda b, i: b * nb + i
    in_specs = [
        pl.BlockSpec((tb, 512), lambda b, i: (row(b, i), 0)),
        pl.BlockSpec((tb, 512), lambda b, i: (row(b, i), 1)),
        pl.BlockSpec((tb, 1024), lambda b, i: (row(b, i), 1)),
        pl.BlockSpec((tb, 1024), lambda b, i: (row(b, i), 2)),
        pl.BlockSpec((tb, LANE), lambda b, i: (row(b, i), 0)),
        pl.BlockSpec(wa.shape, lambda b, i: (0, 0)),
        pl.BlockSpec(ba.shape, lambda b, i: (0, 0)),
        pl.BlockSpec(ng.shape, lambda b, i: (0, 0)),
    ]
    hk = GLA_HEADS * GLA_DK
    return pl.pallas_call(
        _gla_body,
        grid=(batch, nb),
        in_specs=in_specs,
        out_specs=pl.BlockSpec((tb, GLA_HEADS * GLA_DV), lambda b, i: (row(b, i), 0)),
        out_shape=jax.ShapeDtypeStruct((n, GLA_HEADS * GLA_DV), BF16),
        scratch_shapes=[pltpu.VMEM((GLA_HEADS, GLA_DV, GLA_DK), F32),
                        pltpu.VMEM((tb, hk), F32)],
        compiler_params=_params(("parallel", "arbitrary"), 48),
        name="gla",
    )(zg, zg, zg, zg, ga, wa, ba, ng)


DSA_TQ = 256
DSA_TK = 512
DSA_COUNT_ROWS = 128
DSA_MAX_BISECT = 320
DSA_MIN_DENOM = 2.0 ** -100


def _dsa_body(topk, qn_ref, qi_ref, wi_ref, kn_ref, v_ref, ki_ref, o_ref,
              sc_scr, stat_scr, qa_scr, kmax_scr, s_scr, acc_scr):
    tq, tk, hd = DSA_TQ, DSA_TK, DSA_HEAD_DIM
    grp = DSA_HEADS // DSA_KV_HEADS
    kf = float(topk)
    q0 = pl.program_id(1) * tq
    nk = (q0 + tq + tk - 1) // tk
    qpos = q0 + lax.broadcasted_iota(I32, (tq, tk), 0)
    kofs = lax.broadcasted_iota(I32, (tq, tk), 1)
    inf = jnp.inf

    def chunk_start(ci):
        return pl.multiple_of(ci * tk, tk)

    @pl.when(pl.program_id(1) == 0)
    def _():
        def body(ci, mx):
            kc = kn_ref[pl.ds(chunk_start(ci), tk), :].astype(F32)
            return tuple(jnp.maximum(mx[g], jnp.sum(kc[:, g * hd:(g + 1) * hd] ** 2, axis=1, keepdims=True))
                         for g in range(DSA_KV_HEADS))

        mx = lax.fori_loop(0, kn_ref.shape[0] // tk, body,
                           tuple(jnp.zeros((tk, 1), F32) for _ in range(DSA_KV_HEADS)))
        for g in range(DSA_KV_HEADS):
            kmax_scr[g] = jnp.broadcast_to(jnp.max(mx[g], axis=0, keepdims=True), kmax_scr.shape[1:])

    wi = wi_ref[...]

    stat_scr[0] = jnp.full((tq, LANE), -inf, F32)
    stat_scr[1] = jnp.full((tq, LANE), inf, F32)
    stat_scr[2] = jnp.zeros((tq, LANE), F32)
    stat_scr[3] = jnp.zeros((tq, LANE), F32)

    def score_chunk(ci):
        k0 = chunk_start(ci)
        kc = ki_ref[pl.ds(k0, tk), :]
        score = jnp.zeros((tq, tk), F32)
        for h in range(IDX_HEADS):
            qh = qi_ref[:, h * LANE:(h + 1) * LANE]
            score = score + wi[:, h:h + 1] * jnp.maximum(_dot_nt(qh, kc), 0.0)
        valid = k0 + kofs <= qpos
        masked = jnp.where(valid, score, -inf)
        sc_scr[:, pl.ds(k0, tk)] = masked
        rmax, rmin, n_pos, n_nonneg = stat_scr[0], stat_scr[1], stat_scr[2], stat_scr[3]
        for t in range(tk // LANE):
            ts = slice(t * LANE, (t + 1) * LANE)
            rmax = jnp.maximum(rmax, masked[:, ts])
            rmin = jnp.minimum(rmin, jnp.where(valid[:, ts], score[:, ts], inf))
            n_pos = n_pos + jnp.where(masked[:, ts] > 0.0, 1.0, 0.0)
            n_nonneg = n_nonneg + jnp.where(masked[:, ts] >= 0.0, 1.0, 0.0)
        stat_scr[0], stat_scr[1], stat_scr[2], stat_scr[3] = rmax, rmin, n_pos, n_nonneg

    n_pairs = nk // 2
    odd_tail = nk % 2 == 1

    def score_pair(ji, carry):
        score_chunk(2 * ji)
        score_chunk(2 * ji + 1)
        return carry

    lax.fori_loop(0, n_pairs, score_pair, 0)

    @pl.when(odd_tail)
    def _():
        score_chunk(nk - 1)
        sc_scr[:, pl.ds(chunk_start(nk), tk)] = jnp.full((tq, tk), -inf, F32)
    hi0 = jnp.broadcast_to(jnp.max(stat_scr[0], axis=1, keepdims=True), (tq, LANE))
    lo0 = jnp.broadcast_to(jnp.min(stat_scr[1], axis=1, keepdims=True), (tq, LANE))
    ones_sum = jnp.ones((LANE, LANE), BF16)
    c_pos = _dot(stat_scr[2].astype(BF16), ones_sum)
    c_nn = _dot(stat_scr[3].astype(BF16), ones_sum)

    def count(*tests):
        partial = []
        for r0 in range(0, tq, DSA_COUNT_ROWS):
            rows = slice(r0, r0 + DSA_COUNT_ROWS)
            thr_b = [thr[rows] for _, thr in tests]

            def body(ji, cnts):
                sc = sc_scr[rows, pl.ds(pl.multiple_of(ji * 2 * tk, 2 * tk), 2 * tk)]
                out = []
                for (pred, _), tb, cnt in zip(tests, thr_b, cnts):
                    for t in range(2 * tk // LANE):
                        cnt = cnt + jnp.where(pred(sc[:, t * LANE:(t + 1) * LANE], tb), 1.0, 0.0)
                    out.append(cnt)
                return tuple(out)

            partial.append(lax.fori_loop(0, (nk + 1) // 2, body,
                                         tuple(jnp.zeros((DSA_COUNT_ROWS, LANE), F32) for _ in tests)))
        return [jnp.concatenate([_dot(p[k].astype(BF16), ones_sum) for p in partial], axis=0)
                for k in range(len(tests))]

    ge_pred = lambda a, b: a >= b
    n_causal = (q0 + 1 + lax.broadcasted_iota(I32, (tq, LANE), 0)).astype(F32)
    few = n_causal <= kf
    zero_tied = (c_pos < kf) & (c_nn >= kf)
    lo_init = jnp.where(few, lo0, jnp.where(c_nn >= kf, 0.0, lo0))
    hi_init = jnp.where(few | zero_tied, lo_init, jnp.where(c_nn < kf, 0.0, hi0))

    def bisect_cond(carry):
        it, _, _, pending = carry
        return (it < DSA_MAX_BISECT) & (pending > 0.0)

    def bisect(carry):
        it, lo, hi, _ = carry
        mid = 0.5 * lo + 0.5 * hi
        (cnt,) = count((ge_pred, mid))
        pending = jnp.max(jnp.where(mid > lo, jnp.where(mid < hi, 1.0, 0.0), 0.0))
        lo = jnp.where(cnt >= kf, mid, lo)
        hi = jnp.where(cnt > kf, hi, mid)
        return it + 1, lo, hi, pending

    _, thr, _, _ = lax.while_loop(bisect_cond, bisect, (jnp.int32(0), lo_init, hi_init, jnp.float32(1.0)))
    thr_w = jnp.tile(thr, (1, tk // LANE))
    (c_ge,) = count((ge_pred, thr))

    def break_ties(rows):
        nr = DSA_COUNT_ROWS
        thr_g, c_ge_g = thr_w[rows], c_ge[rows]

        def low_body(ci, low):
            sc = sc_scr[rows, pl.ds(chunk_start(ci), tk)]
            cand = jnp.where(sc >= thr_g, sc, inf)
            for t in range(tk // LANE):
                low = jnp.minimum(low, cand[:, t * LANE:(t + 1) * LANE])
            return low

        low = lax.fori_loop(0, nk, low_body, jnp.full((nr, LANE), inf, F32))
        tied_w = jnp.broadcast_to(jnp.min(low, axis=1, keepdims=True), (nr, tk))

        def eq_body(ci, cnt):
            sc = sc_scr[rows, pl.ds(chunk_start(ci), tk)]
            for t in range(tk // LANE):
                cnt = cnt + jnp.where(sc[:, t * LANE:(t + 1) * LANE] == tied_w[:, :LANE], 1.0, 0.0)
            return cnt

        c_eq = _dot(lax.fori_loop(0, nk, eq_body, jnp.zeros((nr, LANE), F32)).astype(BF16), ones_sum)
        allowed = jnp.tile(kf - (c_ge_g - c_eq), (1, tk // LANE))
        upper = jnp.where(lax.broadcasted_iota(I32, (tk, tk), 0) <= lax.broadcasted_iota(I32, (tk, tk), 1),
                          1.0, 0.0).astype(BF16)

        def body(ci, seen):
            k0 = chunk_start(ci)
            sc = sc_scr[rows, pl.ds(k0, tk)]
            eq = jnp.where(sc == tied_w, 1.0, 0.0).astype(BF16)
            rank = seen + _dot(eq, upper)
            drop = jnp.where(sc == tied_w, jnp.where(rank > allowed, 1.0, 0.0), 0.0) > 0.0
            sc_scr[rows, pl.ds(k0, tk)] = jnp.where(drop, -inf, sc)
            return jnp.broadcast_to(rank[:, tk - 1:tk], (nr, tk))

        lax.fori_loop(0, nk, body, jnp.zeros((nr, tk), F32))

    for r0 in range(0, tq, DSA_COUNT_ROWS):
        rows = slice(r0, r0 + DSA_COUNT_ROWS)
        pl.when(jnp.max(c_ge[rows]) > kf)(functools.partial(break_ties, rows))

    lane_q = lax.broadcasted_iota(I32, (tq, hd), 1)
    for g in range(DSA_KV_HEADS):
        kmax2 = kmax_scr[g][0:1, :]
        for r in range(grp):
            hcol = (g * grp + r) * hd
            rows = slice(r * tq, (r + 1) * tq)
            qh = qn_ref[:, hcol:hcol + hd]
            qf = qh.astype(F32)
            bound = jnp.sqrt(jnp.sum(qf * qf, axis=1, keepdims=True) * kmax2)
            qa_scr[g, rows, 0:hd] = qh
            qa_scr[g, rows, hd:2 * hd] = jnp.where(lane_q == 0, bound, 0.0).astype(BF16)
    acc_scr[...] = jnp.zeros_like(acc_scr)
    lane_k = lax.broadcasted_iota(I32, (tk, hd), 1)
    k_tail = jnp.where(lane_k == 0, -1.0, 0.0).astype(BF16)
    v_tail = jnp.where(lane_k == 0, 1.0, 0.0).astype(BF16)

    def sel_bias(k0):
        return jnp.where(sc_scr[:, pl.ds(k0, tk)] >= thr_w, 0.0, -inf)

    def logits(ci, g):
        ka = jnp.concatenate([kn_ref[pl.ds(chunk_start(ci), tk), g * hd:(g + 1) * hd], k_tail], axis=1)
        return _dot_nt(qa_scr[g], ka)

    def weighted_values(ci, g, s, bias):
        va = jnp.concatenate([v_ref[pl.ds(chunk_start(ci), tk), g * hd:(g + 1) * hd], v_tail], axis=1)
        p = jnp.exp2((s.reshape(grp, tq, tk) + bias[None]).reshape(grp * tq, tk))
        return _dot(p.astype(BF16), va)

    last_chunk = kn_ref.shape[0] // tk - 1
    for g in range(DSA_KV_HEADS):
        s_scr[g] = logits(0, g)

    def attend(ji, carry):
        ca = 2 * ji
        cb = ca + 1
        bias_a = sel_bias(chunk_start(ca))
        bias_b = sel_bias(chunk_start(cb))
        for g in range(DSA_KV_HEADS):
            s_a = s_scr[g]
            s_b = logits(cb, g)
            pv_a = weighted_values(ca, g, s_a, bias_a)
            s_scr[g] = logits(jnp.minimum(ca + 2, last_chunk), g)
            acc_scr[g] += pv_a + weighted_values(cb, g, s_b, bias_b)
        return carry

    lax.fori_loop(0, n_pairs, attend, 0)

    @pl.when(odd_tail)
    def _():
        bias = sel_bias(chunk_start(nk - 1))
        for g in range(DSA_KV_HEADS):
            acc_scr[g] += weighted_values(nk - 1, g, s_scr[g], bias)

    def write_out(g, out):
        for r in range(grp):
            hcol = (g * grp + r) * hd
            o_ref[:, hcol:hcol + hd] = out[r * tq:(r + 1) * tq, :].astype(o_ref.dtype)

    l_min = inf
    for g in range(DSA_KV_HEADS):
        acc = acc_scr[g]
        denom = acc[:, hd:hd + 1]
        l_min = jnp.minimum(l_min, jnp.min(denom))
        write_out(g, acc[:, 0:hd] / denom)

    @pl.when(jnp.logical_not(l_min >= DSA_MIN_DENOM))
    def _():
        for g in range(DSA_KV_HEADS):
            s_scr[g, :, 0:LANE] = jnp.full((grp * tq, LANE), MASKED_LOGIT, F32)
        acc_scr[...] = jnp.zeros_like(acc_scr)

        def attend_online(ci, carry):
            k0 = chunk_start(ci)
            bias = sel_bias(k0)
            for g in range(DSA_KV_HEADS):
                hs = slice(g * hd, (g + 1) * hd)
                s = _dot_nt(qa_scr[g, :, 0:hd], kn_ref[pl.ds(k0, tk), hs])
                s = (s.reshape(grp, tq, tk) + bias[None]).reshape(grp * tq, tk)
                m_prev = s_scr[g, :, 0:LANE]
                m_new = jnp.maximum(m_prev, jnp.max(s, axis=1, keepdims=True))
                p = jnp.exp2(s - jnp.tile(m_new, (1, tk // LANE)))
                alpha = jnp.exp2(m_prev - m_new)
                acc_scr[g, :, hd:2 * hd] = alpha * acc_scr[g, :, hd:2 * hd] + jnp.sum(p, axis=1, keepdims=True)
                acc_scr[g, :, 0:hd] = alpha * acc_scr[g, :, 0:hd] + _dot(p.astype(BF16), v_ref[pl.ds(k0, tk), hs])
                s_scr[g, :, 0:LANE] = m_new
            return carry

        lax.fori_loop(0, nk, attend_online, 0)
        for g in range(DSA_KV_HEADS):
            write_out(g, acc_scr[g, :, 0:hd] / acc_scr[g, :, hd:2 * hd])


def _dsa_call(qn, qi, wi, kn, vv, ki, batch, seq):
    n = qn.shape[0]
    tq, hd = DSA_TQ, DSA_HEAD_DIM
    nb = seq // tq
    topk = min(IDX_TOPK_MAX, seq // 4)
    grp = DSA_HEADS // DSA_KV_HEADS
    assert seq // LANE <= 256, "per-lane partial counts must stay exact in bf16"
    assert seq % (2 * DSA_TK) == 0 and seq % tq == 0 and tq % DSA_COUNT_ROWS == 0
    row = lambda b, i: (b * nb + i, 0)
    per_batch = lambda b, i: (b, 0)
    return pl.pallas_call(
        functools.partial(_dsa_body, topk),
        grid=(batch, nb),
        in_specs=[pl.BlockSpec((tq, qn.shape[1]), row),
                  pl.BlockSpec((tq, qi.shape[1]), row),
                  pl.BlockSpec((tq, LANE), row),
                  pl.BlockSpec((seq, kn.shape[1]), per_batch, pipeline_mode=pl.Buffered(1)),
                  pl.BlockSpec((seq, vv.shape[1]), per_batch, pipeline_mode=pl.Buffered(1)),
                  pl.BlockSpec((seq, LANE), per_batch, pipeline_mode=pl.Buffered(1))],
        out_specs=pl.BlockSpec((tq, DSA_HEADS * hd), row),
        out_shape=jax.ShapeDtypeStruct((n, DSA_HEADS * hd), BF16),
        scratch_shapes=[pltpu.VMEM((tq, seq), F32),
                        pltpu.VMEM((4, tq, LANE), F32),
                        pltpu.VMEM((DSA_KV_HEADS, grp * tq, 2 * hd), BF16),
                        pltpu.VMEM((DSA_KV_HEADS, 8, LANE), F32),
                        pltpu.VMEM((DSA_KV_HEADS, grp * tq, DSA_TK), F32),
                        pltpu.VMEM((DSA_KV_HEADS, grp * tq, 2 * hd), F32)],
        compiler_params=_params(("arbitrary", "arbitrary"), 48),
        name="dsa",
    )(qn, qi, wi, kn, vv, ki)


def _mem_attention(q_ref, k_ref, v_ref):
    heads = []
    for h in range(MEM_HEADS):
        sl = slice(h * MEM_HEAD_DIM, (h + 1) * MEM_HEAD_DIM)
        s = _dot_nt(q_ref[:, sl], k_ref[:, sl])
        p = jnp.exp(s - jnp.max(s, axis=1, keepdims=True))
        o = _dot(p.astype(BF16), v_ref[:, sl]) / jnp.sum(p, axis=1, keepdims=True)
        heads.append(o.astype(BF16))
    return jnp.concatenate(heads, axis=1)


def _merge_body(x_ref, gmix_ref, og_ref, od_ref, qm_ref, km_ref, vm_ref, wgate_ref, bgate_ref,
                wg_ref, wd_ref, wm_ref, wo_ref, o_ref):
    d = D_MODEL
    x = x_ref[...]
    h = (_rms_rows(x) * gmix_ref[...]).astype(BF16)
    o_mem = _mem_attention(qm_ref, km_ref, vm_ref)
    branches = (_dot(og_ref[...], wg_ref[...]), _dot(od_ref[...], wd_ref[...]), _dot(o_mem, wm_ref[...]))
    merged = None
    for j, br in enumerate(branches):
        cols = slice(j * d, (j + 1) * d)
        gate = _sigmoid(_dot(h, wgate_ref[:, cols]) + bgate_ref[:, cols])
        merged = gate * br if merged is None else merged + gate * br
    o_ref[...] = x + _dot(merged.astype(BF16), wo_ref[...])


def _merge_call(x2d, g_mix, o_gla, o_dsa, qm, kv, seq, mem_len, w_gate, b_gate, wg, wd, wm, wo, tm):
    n, d = x2d.shape
    row = lambda i: (i, 0)
    fixed = lambda i: (0, 0)
    per_batch = seq // tm
    return pl.pallas_call(
        _merge_body,
        grid=(n // tm,),
        in_specs=[pl.BlockSpec((tm, d), row), pl.BlockSpec((1, d), fixed),
                  pl.BlockSpec((tm, d), row), pl.BlockSpec((tm, d), row), pl.BlockSpec((tm, d), row),
                  pl.BlockSpec((mem_len, d), lambda i: (i // per_batch, 0)),
                  pl.BlockSpec((mem_len, d), lambda i: (i // per_batch, 1)),
                  pl.BlockSpec((d, 3 * d), fixed), pl.BlockSpec((1, 3 * d), fixed),
                  pl.BlockSpec((d, d), fixed), pl.BlockSpec((d, d), fixed),
                  pl.BlockSpec((d, d), fixed), pl.BlockSpec((d, d), fixed)],
        out_specs=pl.BlockSpec((tm, d), row),
        out_shape=jax.ShapeDtypeStruct((n, d), F32),
        compiler_params=_params(("parallel",), 56),
        name="merge",
    )(x2d, g_mix.reshape(1, d), o_gla, o_dsa, qm, kv, kv, w_gate, b_gate.reshape(1, -1), wg, wd, wm, wo)


def _route(z):
    lane = lax.broadcasted_iota(I32, z.shape, 1)
    lanef = lane.astype(F32)
    far = 1e9
    neg = -jnp.inf
    gmask = lane < N_GROUPS
    gmax = jnp.max(jnp.where(gmask, z, neg), axis=1, keepdims=True)
    gsum = jnp.sum(jnp.where(gmask, jnp.exp(z - gmax), 0.0), axis=1, keepdims=True)
    g_w = 1.0 / gsum
    g_sel = jnp.min(jnp.where(gmask & (z == gmax), lanef, far), axis=1, keepdims=True)
    lane_grp = ((lane - N_GROUPS) >> 2).astype(F32)
    emask = (lane >= N_GROUPS) & (lane < N_GROUPS + N_EXPERTS) & (lane_grp == g_sel)
    e1 = jnp.max(jnp.where(emask, z, neg), axis=1, keepdims=True)
    i1 = jnp.min(jnp.where(emask & (z == e1), lanef, far), axis=1, keepdims=True)
    emask2 = emask & (lanef != i1)
    e2 = jnp.max(jnp.where(emask2, z, neg), axis=1, keepdims=True)
    i2 = jnp.min(jnp.where(emask2 & (z == e2), lanef, far), axis=1, keepdims=True)
    t = jnp.exp(e2 - e1)
    p1 = 1.0 / (1.0 + t)
    p2 = t / (1.0 + t)
    return jnp.where(lanef == i1, g_w * p1, jnp.where(lanef == i2, g_w * p2, 0.0))


def _moe_body(x_ref, g_ref, wrh_ref, wrl_ref, br_ref, wg_ref, wu_ref, wd_ref, o_ref, h_scr, comb_scr):
    grp = pl.program_id(1)

    @pl.when(grp == 0)
    def _():
        x = x_ref[...]
        h = _rms_rows(x) * g_ref[...]
        hh = h.astype(BF16)
        hl = (h - hh.astype(F32)).astype(BF16)
        h_scr[...] = hh
        z = _dot(hh, wrh_ref[...]) + _dot(hl, wrh_ref[...]) + _dot(hh, wrl_ref[...]) + br_ref[...]
        comb_scr[...] = _route(z)
        o_ref[...] = x

    lane = lax.broadcasted_iota(I32, comb_scr.shape, 1)
    comb = comb_scr[...]
    h = h_scr[...]
    hidden = []
    for j in range(EXPERTS_PER_GROUP):
        first_lane = N_GROUPS + grp * EXPERTS_PER_GROUP + j
        ce = jnp.sum(jnp.where(lane == first_lane, comb, 0.0), axis=1, keepdims=True)
        gt = _dot(h, wg_ref[j])
        up = _dot(h, wu_ref[j])
        hidden.append((gt * _sigmoid(gt) * up * ce).astype(BF16))
    o_ref[...] += _dot(jnp.concatenate(hidden, axis=1), wd_ref[...])


def _moe_call(x1, g_ffn, wr_hi, wr_lo, br, wg, wu, wd, tm):
    n, d = x1.shape
    row = lambda i, e: (i, 0)
    fixed = lambda i, e: (0, 0)
    per_group = lambda i, e: (e, 0, 0)
    epg = EXPERTS_PER_GROUP
    wd = wd.reshape(N_EXPERTS * D_EXPERT, d)
    return pl.pallas_call(
        _moe_body,
        grid=(n // tm, N_GROUPS),
        in_specs=[pl.BlockSpec((tm, d), row), pl.BlockSpec((1, d), fixed),
                  pl.BlockSpec((d, LANE), fixed), pl.BlockSpec((d, LANE), fixed),
                  pl.BlockSpec((1, LANE), fixed),
                  pl.BlockSpec((epg, d, D_EXPERT), per_group), pl.BlockSpec((epg, d, D_EXPERT), per_group),
                  pl.BlockSpec((epg * D_EXPERT, d), lambda i, e: (e, 0))],
        out_specs=pl.BlockSpec((tm, d), row),
        out_shape=jax.ShapeDtypeStruct((n, d), F32),
        scratch_shapes=[pltpu.VMEM((tm, d), BF16), pltpu.VMEM((tm, LANE), F32)],
        compiler_params=_params(("parallel", "arbitrary"), 48),
        name="moe",
    )(x1, g_ffn.reshape(1, d), wr_hi, wr_lo, br, wg, wu, wd)


def _layer(x2d, mem2d, batch, seq, mem_len, g_mix, g_mem, w_in, w_gla_a2, b_gla_a2, gla_norm,
           w_mem_kv, dsa_q_norm, dsa_k_norm, idx_k_norm, mem_q_norm, mem_k_norm, b_gate,
           w_o_gla, w_o_dsa, w_o_mem, w_out, g_ffn, w_r1, b_r1, w_r2, b_r2, w_gate, w_up, w_down):
    n, d = x2d.shape
    sds = jax.ShapeDtypeStruct

    w_gla = w_in[:, _OFF["gq"][0]:_OFF["gg"][1]].astype(BF16)
    (zg,) = _proj_call("proj_gla", x2d, g_mix, w_gla, _epi_plain, [],
                       [sds((n, w_gla.shape[1]), BF16)], [1024], 1024, 1024, 40)

    w_iq = _cols(w_in, "iq").reshape(d, IDX_HEADS, IDX_DIM)
    w_iq = jnp.pad(w_iq, ((0, 0), (0, 0), (0, LANE - IDX_DIM))).reshape(d, IDX_HEADS * LANE)
    w_q = jnp.concatenate([_cols(w_in, "dq"), _cols(w_in, "mq"), w_iq], axis=1).astype(BF16)
    qn, qm, qi = _proj_call("proj_q", x2d, g_mix, w_q, _epi_queries,
                            [dsa_q_norm.reshape(1, -1), mem_q_norm.reshape(1, -1)],
                            [sds((n, 1024), BF16)] * 3, [1024] * 3, 1024, 1024, 40, out_follows_cols=False)

    w_kv = jnp.concatenate([_cols(w_in, "dk"), _cols(w_in, "dv"), _pad_cols(_cols(w_in, "ik"), LANE),
                            _pad_cols(_cols(w_in, "iw"), LANE), _pad_cols(_cols(w_in, "ga"), LANE)],
                           axis=1).astype(BF16)
    kn, vv, ki, wi, ga = _proj_call(
        "proj_dsa_kv", x2d, g_mix, w_kv, _epi_dsa_kv,
        [dsa_k_norm.reshape(1, -1), _pad_cols(idx_k_norm.reshape(1, -1), LANE)],
        [sds((n, 256), BF16), sds((n, 256), BF16), sds((n, LANE), BF16), sds((n, LANE), F32),
         sds((n, LANE), BF16)],
        [256, 256, LANE, LANE, LANE], 1024, w_kv.shape[1], 40)

    wa = jnp.pad(w_gla_a2, ((0, LANE - GLA_GATE_RANK), (0, 0))).astype(BF16)
    o_gla = _gla_call(zg, ga, wa, b_gla_a2.reshape(1, -1), gla_norm.reshape(1, -1), batch, seq, 512)

    o_dsa = _dsa_call(qn, qi, wi, kn, vv, ki, batch, seq)

    (kv,) = _proj_call("proj_mem_kv", mem2d, g_mem, w_mem_kv.astype(BF16), _epi_mem_kv,
                       [mem_k_norm.reshape(1, -1)], [sds((mem2d.shape[0], 2 * d), BF16)], [1024],
                       mem2d.shape[0], 1024, 40)

    x1 = _merge_call(x2d, g_mix, o_gla, o_dsa, qm, kv, seq, mem_len, _cols(w_in, "gates").astype(BF16),
                     b_gate, w_o_gla.astype(BF16), w_o_dsa.astype(BF16), w_o_mem.astype(BF16),
                     w_out.astype(BF16), 512)

    w_r = _pad_cols(jnp.concatenate([w_r1, w_r2], axis=1), LANE)
    wr_hi = w_r.astype(BF16)
    wr_lo = (w_r - wr_hi.astype(F32)).astype(BF16)
    b_r = _pad_cols(jnp.concatenate([b_r1, b_r2]).reshape(1, -1), LANE)
    return _moe_call(x1, g_ffn, wr_hi, wr_lo, b_r, w_gate.astype(BF16), w_up.astype(BF16),
                     w_down.astype(BF16), 1024)


def kernel(x, mem, g_mix, g_mem, w_in, w_gla_a2, b_gla_a2, gla_norm, w_mem_kv, dsa_q_norm, dsa_k_norm, idx_k_norm, mem_q_norm, mem_k_norm, b_gate, w_o_gla, w_o_dsa, w_o_mem, w_out, g_ffn, w_r1, b_r1, w_r2, b_r2, w_gate, w_up, w_down):
    batch, seq, d = x.shape
    mem_len = mem.shape[1]
    x2d = x.reshape(batch * seq, d)
    mem2d = mem.reshape(batch * mem_len, d)
    params = (g_mix, g_mem, w_in, w_gla_a2, b_gla_a2, gla_norm, w_mem_kv, dsa_q_norm, dsa_k_norm,
              idx_k_norm, mem_q_norm, mem_k_norm, b_gate, w_o_gla, w_o_dsa, w_o_mem, w_out, g_ffn,
              w_r1, b_r1, w_r2, b_r2, w_gate, w_up, w_down)
    for layer in range(g_mix.shape[0]):
        x2d = _layer(x2d, mem2d, batch, seq, mem_len, *(p[layer] for p in params))
    return x2d.reshape(batch, seq, d)
```

```python
import functools

import jax
import jax.numpy as jnp
from jax import lax
from jax.experimental import pallas as pl
from jax.experimental.pallas import tpu as pltpu

F32 = jnp.float32
BF16 = jnp.bfloat16
I32 = jnp.int32

D_MODEL = 1024
EPS = 1e-6

GLA_HEADS = 4
GLA_DK = 128
GLA_DV = 256
GLA_GATE_RANK = 16
GLA_GATE_TEMP = 16.0
GLA_CHUNK = 64
GLA_UNROLL = 8

DSA_HEADS = 8
DSA_KV_HEADS = 2
DSA_HEAD_DIM = 128
IDX_HEADS = 8
IDX_DIM = 64
IDX_TOPK_MAX = 256

MEM_HEADS = 4
MEM_HEAD_DIM = 256

N_GROUPS = 4
EXPERTS_PER_GROUP = 4
N_EXPERTS = 16
D_EXPERT = 256

LANE = 128
MASKED_LOGIT = -1e30
LOG2_E = 1.4426950408889634

_OFF = {}
_acc = 0
for _name, _w in (("gq", 512), ("gk", 512), ("gv", 1024), ("gg", 1024), ("ga", 16), ("dq", 1024),
                  ("dk", 256), ("dv", 256), ("iq", 512), ("ik", 64), ("iw", 8), ("mq", 1024),
                  ("gates", 3072)):
    _OFF[_name] = (_acc, _acc + _w)
    _acc += _w


def _cols(w_in, name):
    lo, hi = _OFF[name]
    return w_in[:, lo:hi]


def _pad_cols(w, width):
    return jnp.pad(w, ((0, 0), (0, width - w.shape[1])))


def _rms_rows(x):
    return x * lax.rsqrt(jnp.mean(x * x, axis=-1, keepdims=True) + EPS)


def _dot(a, b):
    return jnp.dot(a, b, preferred_element_type=F32)


def _dot_nt(a, b):
    return lax.dot_general(a, b, (((1,), (1,)), ((), ())), preferred_element_type=F32)


def _dot_tn(a, b):
    return lax.dot_general(a, b, (((0,), (0,)), ((), ())), preferred_element_type=F32)


def _sigmoid(x):
    return 1.0 / (1.0 + jnp.exp(-x))


def _params(sem, vmem_mb):
    return pltpu.CompilerParams(dimension_semantics=sem, vmem_limit_bytes=vmem_mb * 1024 * 1024)


def _proj_body(epilogue, n_extra, x_ref, g_ref, w_ref, *rest):
    extra = rest[:n_extra]
    outs = rest[n_extra:-1]
    h_scr = rest[-1]

    @pl.when(pl.program_id(1) == 0)
    def _():
        h_scr[...] = (_rms_rows(x_ref[...]) * g_ref[...]).astype(BF16)

    epilogue(_dot(h_scr[...], w_ref[...]), extra, outs)


def _proj_call(name, x2d, gain, w, epilogue, extras, out_shapes, out_widths, tm, tn, vmem_mb):
    n, d = x2d.shape
    c = w.shape[1]
    in_specs = [pl.BlockSpec((tm, d), lambda i, j: (i, 0)),
                pl.BlockSpec((1, d), lambda i, j: (0, 0)),
                pl.BlockSpec((d, tn), lambda i, j: (0, j))]
    single = c == tn
    for e in extras:
        if e.shape[1] == c and not single:
            in_specs.append(pl.BlockSpec((1, tn), lambda i, j: (0, j)))
        else:
            in_specs.append(pl.BlockSpec(e.shape, lambda i, j: (0, 0)))
    out_specs = [pl.BlockSpec((tm, ow), (lambda i, j: (i, 0)) if single else (lambda i, j: (i, j)))
                 for ow in out_widths]
    return pl.pallas_call(
        functools.partial(_proj_body, epilogue, len(extras)),
        grid=(n // tm, c // tn),
        in_specs=in_specs,
        out_specs=out_specs,
        out_shape=out_shapes,
        scratch_shapes=[pltpu.VMEM((tm, d), BF16)],
        compiler_params=_params(("parallel", "arbitrary"), vmem_mb),
        name=name,
    )(x2d, gain.reshape(1, d), w, *extras)


def _epi_plain(acc, extra, outs):
    outs[0][...] = acc.astype(outs[0].dtype)


def _epi_headnorm(head_dim, scale, acc, extra, outs):
    gain = extra[0][...]
    for h in range(acc.shape[1] // head_dim):
        sl = slice(h * head_dim, (h + 1) * head_dim)
        outs[0][:, sl] = (_rms_rows(acc[:, sl]) * gain * scale).astype(outs[0].dtype)


def _epi_dsa_kv(acc, extra, outs):
    k_gain, idx_gain = extra
    kn_ref, v_ref, ki_ref, wi_ref, ga_ref = outs
    for h in range(DSA_KV_HEADS):
        sl = slice(h * DSA_HEAD_DIM, (h + 1) * DSA_HEAD_DIM)
        kn_ref[:, sl] = (_rms_rows(acc[:, sl]) * k_gain[...]).astype(BF16)
    v_ref[...] = acc[:, 256:512].astype(BF16)
    a = acc[:, 512:640]
    ms = jnp.sum(a * a, axis=-1, keepdims=True) * (1.0 / IDX_DIM)
    ki_ref[...] = (a * lax.rsqrt(ms + EPS) * idx_gain[...]).astype(BF16)
    wi_ref[...] = acc[:, 640:768] * (IDX_HEADS ** -0.5 * IDX_DIM ** -0.5)
    ga_ref[...] = acc[:, 768:896].astype(BF16)


def _epi_mem_kv(acc, extra, outs):
    gain = extra[0][...]
    j = pl.program_id(1)

    @pl.when(j == 0)
    def _():
        for h in range(MEM_HEADS):
            sl = slice(h * MEM_HEAD_DIM, (h + 1) * MEM_HEAD_DIM)
            outs[0][:, sl] = (_rms_rows(acc[:, sl]) * gain).astype(BF16)

    @pl.when(j == 1)
    def _():
        outs[0][...] = acc.astype(BF16)


def _split2(x):
    hi = x.astype(BF16)
    return hi, (x - hi.astype(F32)).astype(BF16)


def _gla_body(q_ref, k_ref, v_ref, g_ref, a_ref, wa_ref, ba_ref, ng_ref, o_ref, st_scr, la_scr):
    tb = q_ref.shape[0]
    c = GLA_CHUNK

    @pl.when(pl.program_id(1) == 0)
    def _():
        st_scr[...] = jnp.zeros_like(st_scr)

    pre = _dot(a_ref[...].astype(BF16), wa_ref[...]) + ba_ref[...]
    la_scr[...] = (jnp.minimum(pre, 0.0) - jnp.log1p(jnp.exp(-jnp.abs(pre)))) * (1.0 / GLA_GATE_TEMP)

    causal = lax.broadcasted_iota(I32, (c, c), 1) <= lax.broadcasted_iota(I32, (c, c), 0)
    tri = jnp.where(causal, 1.0, 0.0).astype(BF16)

    def chunk(ci):
        r0 = pl.multiple_of(ci * c, c)
        rows = pl.ds(r0, c)
        hi, lo = _split2(la_scr[rows, :])
        cum = _dot(tri, hi) + _dot(tri, lo)
        tot = cum[c - 1:c, :]
        kf = k_ref[rows, :]
        q_dec = (q_ref[rows, :] * (GLA_DK ** -0.5) * jnp.exp(cum)).astype(BF16)
        k_inv = (kf * jnp.exp(-cum)).astype(BF16)
        k_end = (kf * jnp.exp(tot - cum)).astype(BF16)
        decay = jnp.exp(tot)
        for h in range(GLA_HEADS):
            ks = slice(h * GLA_DK, (h + 1) * GLA_DK)
            vs = slice(h * GLA_DV, (h + 1) * GLA_DV)
            qd = q_dec[:, ks]
            vv = v_ref[rows, vs].astype(BF16)
            att = jnp.where(causal, _dot_nt(qd, k_inv[:, ks]), 0.0).astype(BF16)
            st = st_scr[h]
            o = _dot(att, vv) + _dot_nt(qd, st.astype(BF16))
            st_scr[h] = st * decay[:, ks] + _dot_tn(vv, k_end[:, ks])
            gg = g_ref[rows, vs].astype(F32)
            o_ref[rows, vs] = (_rms_rows(o) * ng_ref[...] * (gg * _sigmoid(gg))).astype(o_ref.dtype)

    def chunk_group(ji, carry):
        for u in range(GLA_UNROLL):
            chunk(GLA_UNROLL * ji + u)
        return carry

    lax.fori_loop(0, tb // (GLA_UNROLL * c), chunk_group, 0)


def _gla_call(zg, ga, wa, ba, ng, batch, seq, tb):
    n = zg.shape[0]
    nb = seq // tb
    row = lambda b, i: b * nb + i
    in_specs = [
        pl.BlockSpec((tb, 512), lambda b, i: (row(b, i), 0)),
        pl.BlockSpec((tb, 512), lambda b, i: (row(b, i), 1)),
        pl.BlockSpec((tb, 1024), lambda b, i: (row(b, i), 1)),
        pl.BlockSpec((tb, 1024), lambda b, i: (row(b, i), 2)),
        pl.BlockSpec((tb, LANE), lambda b, i: (row(b, i), 0)),
        pl.BlockSpec(wa.shape, lambda b, i: (0, 0)),
        pl.BlockSpec(ba.shape, lambda b, i: (0, 0)),
        pl.BlockSpec(ng.shape, lambda b, i: (0, 0)),
    ]
    hk = GLA_HEADS * GLA_DK
    return pl.pallas_call(
        _gla_body,
        grid=(batch, nb),
        in_specs=in_specs,
        out_specs=pl.BlockSpec((tb, GLA_HEADS * GLA_DV), lambda b, i: (row(b, i), 0)),
        out_shape=jax.ShapeDtypeStruct((n, GLA_HEADS * GLA_DV), BF16),
        scratch_shapes=[pltpu.VMEM((GLA_HEADS, GLA_DV, GLA_DK), F32),
                        pltpu.VMEM((tb, hk), F32)],
        compiler_params=_params(("parallel", "arbitrary"), 48),
        name="gla",
    )(zg, zg, zg, zg, ga, wa, ba, ng)


DSA_TQ = 256
DSA_TK = 512
DSA_COUNT_ROWS = 128
DSA_MAX_BISECT = 320
DSA_MIN_DENOM = 2.0 ** -100


def _dsa_body(topk, qn_ref, qi_ref, wi_ref, kn_ref, v_ref, ki_ref, o_ref,
              sc_scr, stat_scr, qa_scr, kmax_scr, s_scr, acc_scr):
    tq, tk, hd = DSA_TQ, DSA_TK, DSA_HEAD_DIM
    grp = DSA_HEADS // DSA_KV_HEADS
    kf = float(topk)
    q0 = pl.program_id(1) * tq
    nk = (q0 + tq + tk - 1) // tk
    qpos = q0 + lax.broadcasted_iota(I32, (tq, tk), 0)
    kofs = lax.broadcasted_iota(I32, (tq, tk), 1)
    inf = jnp.inf

    def chunk_start(ci):
        return pl.multiple_of(ci * tk, tk)

    @pl.when(pl.program_id(1) == 0)
    def _():
        def body(ci, mx):
            kc = kn_ref[pl.ds(chunk_start(ci), tk), :].astype(F32)
            return tuple(jnp.maximum(mx[g], jnp.sum(kc[:, g * hd:(g + 1) * hd] ** 2, axis=1, keepdims=True))
                         for g in range(DSA_KV_HEADS))

        mx = lax.fori_loop(0, kn_ref.shape[0] // tk, body,
                           tuple(jnp.zeros((tk, 1), F32) for _ in range(DSA_KV_HEADS)))
        for g in range(DSA_KV_HEADS):
            kmax_scr[g] = jnp.broadcast_to(jnp.max(mx[g], axis=0, keepdims=True), kmax_scr.shape[1:])

    wi = wi_ref[...]

    stat_scr[0] = jnp.full((tq, LANE), -inf, F32)
    stat_scr[1] = jnp.full((tq, LANE), inf, F32)
    stat_scr[2] = jnp.zeros((tq, LANE), F32)
    stat_scr[3] = jnp.zeros((tq, LANE), F32)

    def score_chunk(ci):
        k0 = chunk_start(ci)
        kc = ki_ref[pl.ds(k0, tk), :]
        score = jnp.zeros((tq, tk), F32)
        for h in range(IDX_HEADS):
            qh = qi_ref[:, h * LANE:(h + 1) * LANE]
            score = score + wi[:, h:h + 1] * jnp.maximum(_dot_nt(qh, kc), 0.0)
        valid = k0 + kofs <= qpos
        masked = jnp.where(valid, score, -inf)
        sc_scr[:, pl.ds(k0, tk)] = masked
        rmax, rmin, n_pos, n_nonneg = stat_scr[0], stat_scr[1], stat_scr[2], stat_scr[3]
        for t in range(tk // LANE):
            ts = slice(t * LANE, (t + 1) * LANE)
            rmax = jnp.maximum(rmax, masked[:, ts])
            rmin = jnp.minimum(rmin, jnp.where(valid[:, ts], score[:, ts], inf))
            n_pos = n_pos + jnp.where(masked[:, ts] > 0.0, 1.0, 0.0)
            n_nonneg = n_nonneg + jnp.where(masked[:, ts] >= 0.0, 1.0, 0.0)
        stat_scr[0], stat_scr[1], stat_scr[2], stat_scr[3] = rmax, rmin, n_pos, n_nonneg

    n_pairs = nk // 2
    odd_tail = nk % 2 == 1

    def score_pair(ji, carry):
        score_chunk(2 * ji)
        score_chunk(2 * ji + 1)
        return carry

    lax.fori_loop(0, n_pairs, score_pair, 0)

    @pl.when(odd_tail)
    def _():
        score_chunk(nk - 1)
        sc_scr[:, pl.ds(chunk_start(nk), tk)] = jnp.full((tq, tk), -inf, F32)
    hi0 = jnp.broadcast_to(jnp.max(stat_scr[0], axis=1, keepdims=True), (tq, LANE))
    lo0 = jnp.broadcast_to(jnp.min(stat_scr[1], axis=1, keepdims=True), (tq, LANE))
    ones_sum = jnp.ones((LANE, LANE), BF16)
    c_pos = _dot(stat_scr[2].astype(BF16), ones_sum)
    c_nn = _dot(stat_scr[3].astype(BF16), ones_sum)

    def count(*tests):
        partial = []
        for r0 in range(0, tq, DSA_COUNT_ROWS):
            rows = slice(r0, r0 + DSA_COUNT_ROWS)
            thr_b = [thr[rows] for _, thr in tests]

            def body(ji, cnts):
                sc = sc_scr[rows, pl.ds(pl.multiple_of(ji * 2 * tk, 2 * tk), 2 * tk)]
                out = []
                for (pred, _), tb, cnt in zip(tests, thr_b, cnts):
                    for t in range(2 * tk // LANE):
                        cnt = cnt + jnp.where(pred(sc[:, t * LANE:(t + 1) * LANE], tb), 1.0, 0.0)
                    out.append(cnt)
                return tuple(out)

            partial.append(lax.fori_loop(0, (nk + 1) // 2, body,
                                         tuple(jnp.zeros((DSA_COUNT_ROWS, LANE), F32) for _ in tests)))
        return [jnp.concatenate([_dot(p[k].astype(BF16), ones_sum) for p in partial], axis=0)
                for k in range(len(tests))]

    ge_pred = lambda a, b: a >= b
    n_causal = (q0 + 1 + lax.broadcasted_iota(I32, (tq, LANE), 0)).astype(F32)
    few = n_causal <= kf
    zero_tied = (c_pos < kf) & (c_nn >= kf)
    lo_init = jnp.where(few, lo0, jnp.where(c_nn >= kf, 0.0, lo0))
    hi_init = jnp.where(few | zero_tied, lo_init, jnp.where(c_nn < kf, 0.0, hi0))

    def bisect_cond(carry):
        it, _, _, pending = carry
        return (it < DSA_MAX_BISECT) & (pending > 0.0)

    def bisect(carry):
        it, lo, hi, _ = carry
        mid = 0.5 * lo + 0.5 * hi
        (cnt,) = count((ge_pred, mid))
        pending = jnp.max(jnp.where(mid > lo, jnp.where(mid < hi, 1.0, 0.0), 0.0))
        lo = jnp.where(cnt >= kf, mid, lo)
        hi = jnp.where(cnt > kf, hi, mid)
        return it + 1, lo, hi, pending

    _, thr, _, _ = lax.while_loop(bisect_cond, bisect, (jnp.int32(0), lo_init, hi_init, jnp.float32(1.0)))
    thr_w = jnp.tile(thr, (1, tk // LANE))
    (c_ge,) = count((ge_pred, thr))

    def break_ties(rows):
        nr = DSA_COUNT_ROWS
        thr_g, c_ge_g = thr_w[rows], c_ge[rows]

        def low_body(ci, low):
            sc = sc_scr[rows, pl.ds(chunk_start(ci), tk)]
            cand = jnp.where(sc >= thr_g, sc, inf)
            for t in range(tk // LANE):
                low = jnp.minimum(low, cand[:, t * LANE:(t + 1) * LANE])
            return low

        low = lax.fori_loop(0, nk, low_body, jnp.full((nr, LANE), inf, F32))
        tied_w = jnp.broadcast_to(jnp.min(low, axis=1, keepdims=True), (nr, tk))

        def eq_body(ci, cnt):
            sc = sc_scr[rows, pl.ds(chunk_start(ci), tk)]
            for t in range(tk // LANE):
                cnt = cnt + jnp.where(sc[:, t * LANE:(t + 1) * LANE] == tied_w[:, :LANE], 1.0, 0.0)
            return cnt

        c_eq = _dot(lax.fori_loop(0, nk, eq_body, jnp.zeros((nr, LANE), F32)).astype(BF16), ones_sum)
        allowed = jnp.tile(kf - (c_ge_g - c_eq), (1, tk // LANE))
        upper = jnp.where(lax.broadcasted_iota(I32, (tk, tk), 0) <= lax.broadcasted_iota(I32, (tk, tk), 1),
                          1.0, 0.0).astype(BF16)

        def body(ci, seen):
            k0 = chunk_start(ci)
            sc = sc_scr[rows, pl.ds(k0, tk)]
            eq = jnp.where(sc == tied_w, 1.0, 0.0).astype(BF16)
            rank = seen + _dot(eq, upper)
            drop = jnp.where(sc == tied_w, jnp.where(rank > allowed, 1.0, 0.0), 0.0) > 0.0
            sc_scr[rows, pl.ds(k0, tk)] = jnp.where(drop, -inf, sc)
            return jnp.broadcast_to(rank[:, tk - 1:tk], (nr, tk))

        lax.fori_loop(0, nk, body, jnp.zeros((nr, tk), F32))

    for r0 in range(0, tq, DSA_COUNT_ROWS):
        rows = slice(r0, r0 + DSA_COUNT_ROWS)
        pl.when(jnp.max(c_ge[rows]) > kf)(functools.partial(break_ties, rows))

    lane_q = lax.broadcasted_iota(I32, (tq, hd), 1)
    for g in range(DSA_KV_HEADS):
        kmax2 = kmax_scr[g][0:1, :]
        for r in range(grp):
            hcol = (g * grp + r) * hd
            rows = slice(r * tq, (r + 1) * tq)
            qh = qn_ref[:, hcol:hcol + hd]
            qf = qh.astype(F32)
            bound = jnp.sqrt(jnp.sum(qf * qf, axis=1, keepdims=True) * kmax2)
            qa_scr[g, rows, 0:hd] = qh
            qa_scr[g, rows, hd:2 * hd] = jnp.where(lane_q == 0, bound, 0.0).astype(BF16)
    acc_scr[...] = jnp.zeros_like(acc_scr)
    lane_k = lax.broadcasted_iota(I32, (tk, hd), 1)
    k_tail = jnp.where(lane_k == 0, -1.0, 0.0).astype(BF16)
    v_tail = jnp.where(lane_k == 0, 1.0, 0.0).astype(BF16)

    def sel_bias(k0):
        return jnp.where(sc_scr[:, pl.ds(k0, tk)] >= thr_w, 0.0, -inf)

    def logits(ci, g):
        ka = jnp.concatenate([kn_ref[pl.ds(chunk_start(ci), tk), g * hd:(g + 1) * hd], k_tail], axis=1)
        return _dot_nt(qa_scr[g], ka)

    def weighted_values(ci, g, s, bias):
        va = jnp.concatenate([v_ref[pl.ds(chunk_start(ci), tk), g * hd:(g + 1) * hd], v_tail], axis=1)
        p = jnp.exp2((s.reshape(grp, tq, tk) + bias[None]).reshape(grp * tq, tk))
        return _dot(p.astype(BF16), va)

    last_chunk = kn_ref.shape[0] // tk - 1
    for g in range(DSA_KV_HEADS):
        s_scr[g] = logits(0, g)

    def attend(ji, carry):
        ca = 2 * ji
        cb = ca + 1
        bias_a = sel_bias(chunk_start(ca))
        bias_b = sel_bias(chunk_start(cb))
        for g in range(DSA_KV_HEADS):
            s_a = s_scr[g]
            s_b = logits(cb, g)
            pv_a = weighted_values(ca, g, s_a, bias_a)
            s_scr[g] = logits(jnp.minimum(ca + 2, last_chunk), g)
            acc_scr[g] += pv_a + weighted_values(cb, g, s_b, bias_b)
        return carry

    lax.fori_loop(0, n_pairs, attend, 0)

    @pl.when(odd_tail)
    def _():
        bias = sel_bias(chunk_start(nk - 1))
        for g in range(DSA_KV_HEADS):
            acc_scr[g] += weighted_values(nk - 1, g, s_scr[g], bias)

    def write_out(g, out):
        for r in range(grp):
            hcol = (g * grp + r) * hd
            o_ref[:, hcol:hcol + hd] = out[r * tq:(r + 1) * tq, :].astype(o_ref.dtype)

    l_min = inf
    for g in range(DSA_KV_HEADS):
        acc = acc_scr[g]
        denom = acc[:, hd:hd + 1]
        l_min = jnp.minimum(l_min, jnp.min(denom))
        write_out(g, acc[:, 0:hd] / denom)

    @pl.when(jnp.logical_not(l_min >= DSA_MIN_DENOM))
    def _():
        for g in range(DSA_KV_HEADS):
            s_scr[g, :, 0:LANE] = jnp.full((grp * tq, LANE), MASKED_LOGIT, F32)
        acc_scr[...] = jnp.zeros_like(acc_scr)

        def attend_online(ci, carry):
            k0 = chunk_start(ci)
            bias = sel_bias(k0)
            for g in range(DSA_KV_HEADS):
                hs = slice(g * hd, (g + 1) * hd)
                s = _dot_nt(qa_scr[g, :, 0:hd], kn_ref[pl.ds(k0, tk), hs])
                s = (s.reshape(grp, tq, tk) + bias[None]).reshape(grp * tq, tk)
                m_prev = s_scr[g, :, 0:LANE]
                m_new = jnp.maximum(m_prev, jnp.max(s, axis=1, keepdims=True))
                p = jnp.exp2(s - jnp.tile(m_new, (1, tk // LANE)))
                alpha = jnp.exp2(m_prev - m_new)
                acc_scr[g, :, hd:2 * hd] = alpha * acc_scr[g, :, hd:2 * hd] + jnp.sum(p, axis=1, keepdims=True)
                acc_scr[g, :, 0:hd] = alpha * acc_scr[g, :, 0:hd] + _dot(p.astype(BF16), v_ref[pl.ds(k0, tk), hs])
                s_scr[g, :, 0:LANE] = m_new
            return carry

        lax.fori_loop(0, nk, attend_online, 0)
        for g in range(DSA_KV_HEADS):
            write_out(g, acc_scr[g, :, 0:hd] / acc_scr[g, :, hd:2 * hd])


def _dsa_call(qn, qi, wi, kn, vv, ki, batch, seq):
    n = qn.shape[0]
    tq, hd = DSA_TQ, DSA_HEAD_DIM
    nb = seq // tq
    topk = min(IDX_TOPK_MAX, seq // 4)
    grp = DSA_HEADS // DSA_KV_HEADS
    assert seq // LANE <= 256, "per-lane partial counts must stay exact in bf16"
    assert seq % (2 * DSA_TK) == 0 and seq % tq == 0 and tq % DSA_COUNT_ROWS == 0
    row = lambda b, i: (b * nb + i, 0)
    per_batch = lambda b, i: (b, 0)
    return pl.pallas_call(
        functools.partial(_dsa_body, topk),
        grid=(batch, nb),
        in_specs=[pl.BlockSpec((tq, qn.shape[1]), row),
                  pl.BlockSpec((tq, qi.shape[1]), row),
                  pl.BlockSpec((tq, LANE), row),
                  pl.BlockSpec((seq, kn.shape[1]), per_batch, pipeline_mode=pl.Buffered(1)),
                  pl.BlockSpec((seq, vv.shape[1]), per_batch, pipeline_mode=pl.Buffered(1)),
                  pl.BlockSpec((seq, LANE), per_batch, pipeline_mode=pl.Buffered(1))],
        out_specs=pl.BlockSpec((tq, DSA_HEADS * hd), row),
        out_shape=jax.ShapeDtypeStruct((n, DSA_HEADS * hd), BF16),
        scratch_shapes=[pltpu.VMEM((tq, seq), F32),
                        pltpu.VMEM((4, tq, LANE), F32),
                        pltpu.VMEM((DSA_KV_HEADS, grp * tq, 2 * hd), BF16),
                        pltpu.VMEM((DSA_KV_HEADS, 8, LANE), F32),
                        pltpu.VMEM((DSA_KV_HEADS, grp * tq, DSA_TK), F32),
                        pltpu.VMEM((DSA_KV_HEADS, grp * tq, 2 * hd), F32)],
        compiler_params=_params(("arbitrary", "arbitrary"), 48),
        name="dsa",
    )(qn, qi, wi, kn, vv, ki)


def _mem_attention(q_ref, k_ref, v_ref):
    heads = []
    for h in range(MEM_HEADS):
        sl = slice(h * MEM_HEAD_DIM, (h + 1) * MEM_HEAD_DIM)
        s = _dot_nt(q_ref[:, sl], k_ref[:, sl])
        p = jnp.exp(s - jnp.max(s, axis=1, keepdims=True))
        o = _dot(p.astype(BF16), v_ref[:, sl]) / jnp.sum(p, axis=1, keepdims=True)
        heads.append(o.astype(BF16))
    return jnp.concatenate(heads, axis=1)


def _merge_body(x_ref, gmix_ref, og_ref, od_ref, qm_ref, km_ref, vm_ref, wgate_ref, bgate_ref,
                wg_ref, wd_ref, wm_ref, wo_ref, o_ref):
    d = D_MODEL
    x = x_ref[...]
    h = (_rms_rows(x) * gmix_ref[...]).astype(BF16)
    o_mem = _mem_attention(qm_ref, km_ref, vm_ref)
    branches = (_dot(og_ref[...], wg_ref[...]), _dot(od_ref[...], wd_ref[...]), _dot(o_mem, wm_ref[...]))
    merged = None
    for j, br in enumerate(branches):
        cols = slice(j * d, (j + 1) * d)
        gate = _sigmoid(_dot(h, wgate_ref[:, cols]) + bgate_ref[:, cols])
        merged = gate * br if merged is None else merged + gate * br
    o_ref[...] = x + _dot(merged.astype(BF16), wo_ref[...])


def _merge_call(x2d, g_mix, o_gla, o_dsa, qm, kv, seq, mem_len, w_gate, b_gate, wg, wd, wm, wo, tm):
    n, d = x2d.shape
    row = lambda i: (i, 0)
    fixed = lambda i: (0, 0)
    per_batch = seq // tm
    return pl.pallas_call(
        _merge_body,
        grid=(n // tm,),
        in_specs=[pl.BlockSpec((tm, d), row), pl.BlockSpec((1, d), fixed),
                  pl.BlockSpec((tm, d), row), pl.BlockSpec((tm, d), row), pl.BlockSpec((tm, d), row),
                  pl.BlockSpec((mem_len, d), lambda i: (i // per_batch, 0)),
                  pl.BlockSpec((mem_len, d), lambda i: (i // per_batch, 1)),
                  pl.BlockSpec((d, 3 * d), fixed), pl.BlockSpec((1, 3 * d), fixed),
                  pl.BlockSpec((d, d), fixed), pl.BlockSpec((d, d), fixed),
                  pl.BlockSpec((d, d), fixed), pl.BlockSpec((d, d), fixed)],
        out_specs=pl.BlockSpec((tm, d), row),
        out_shape=jax.ShapeDtypeStruct((n, d), F32),
        compiler_params=_params(("parallel",), 56),
        name="merge",
    )(x2d, g_mix.reshape(1, d), o_gla, o_dsa, qm, kv, kv, w_gate, b_gate.reshape(1, -1), wg, wd, wm, wo)


def _route(z):
    lane = lax.broadcasted_iota(I32, z.shape, 1)
    lanef = lane.astype(F32)
    far = 1e9
    neg = -jnp.inf
    gmask = lane < N_GROUPS
    gmax = jnp.max(jnp.where(gmask, z, neg), axis=1, keepdims=True)
    gsum = jnp.sum(jnp.where(gmask, jnp.exp(z - gmax), 0.0), axis=1, keepdims=True)
    g_w = 1.0 / gsum
    g_sel = jnp.min(jnp.where(gmask & (z == gmax), lanef, far), axis=1, keepdims=True)
    lane_grp = ((lane - N_GROUPS) >> 2).astype(F32)
    emask = (lane >= N_GROUPS) & (lane < N_GROUPS + N_EXPERTS) & (lane_grp == g_sel)
    e1 = jnp.max(jnp.where(emask, z, neg), axis=1, keepdims=True)
    i1 = jnp.min(jnp.where(emask & (z == e1), lanef, far), axis=1, keepdims=True)
    emask2 = emask & (lanef != i1)
    e2 = jnp.max(jnp.where(emask2, z, neg), axis=1, keepdims=True)
    i2 = jnp.min(jnp.where(emask2 & (z == e2), lanef, far), axis=1, keepdims=True)
    t = jnp.exp(e2 - e1)
    p1 = 1.0 / (1.0 + t)
    p2 = t / (1.0 + t)
    return jnp.where(lanef == i1, g_w * p1, jnp.where(lanef == i2, g_w * p2, 0.0))


def _moe_body(x_ref, g_ref, wrh_ref, wrl_ref, br_ref, wg_ref, wu_ref, wd_ref, o_ref, h_scr, comb_scr):
    grp = pl.program_id(1)

    @pl.when(grp == 0)
    def _():
        x = x_ref[...]
        h = _rms_rows(x) * g_ref[...]
        hh = h.astype(BF16)
        hl = (h - hh.astype(F32)).astype(BF16)
        h_scr[...] = hh
        z = _dot(hh, wrh_ref[...]) + _dot(hl, wrh_ref[...]) + _dot(hh, wrl_ref[...]) + br_ref[...]
        comb_scr[...] = _route(z)
        o_ref[...] = x

    lane = lax.broadcasted_iota(I32, comb_scr.shape, 1)
    comb = comb_scr[...]
    h = h_scr[...]
    hidden = []
    for j in range(EXPERTS_PER_GROUP):
        first_lane = N_GROUPS + grp * EXPERTS_PER_GROUP + j
        ce = jnp.sum(jnp.where(lane == first_lane, comb, 0.0), axis=1, keepdims=True)
        gt = _dot(h, wg_ref[j])
        up = _dot(h, wu_ref[j])
        hidden.append((gt * _sigmoid(gt) * up * ce).astype(BF16))
    o_ref[...] += _dot(jnp.concatenate(hidden, axis=1), wd_ref[...])


def _moe_call(x1, g_ffn, wr_hi, wr_lo, br, wg, wu, wd, tm):
    n, d = x1.shape
    row = lambda i, e: (i, 0)
    fixed = lambda i, e: (0, 0)
    per_group = lambda i, e: (e, 0, 0)
    epg = EXPERTS_PER_GROUP
    wd = wd.reshape(N_EXPERTS * D_EXPERT, d)
    return pl.pallas_call(
        _moe_body,
        grid=(n // tm, N_GROUPS),
        in_specs=[pl.BlockSpec((tm, d), row), pl.BlockSpec((1, d), fixed),
                  pl.BlockSpec((d, LANE), fixed), pl.BlockSpec((d, LANE), fixed),
                  pl.BlockSpec((1, LANE), fixed),
                  pl.BlockSpec((epg, d, D_EXPERT), per_group), pl.BlockSpec((epg, d, D_EXPERT), per_group),
                  pl.BlockSpec((epg * D_EXPERT, d), lambda i, e: (e, 0))],
        out_specs=pl.BlockSpec((tm, d), row),
        out_shape=jax.ShapeDtypeStruct((n, d), F32),
        scratch_shapes=[pltpu.VMEM((tm, d), BF16), pltpu.VMEM((tm, LANE), F32)],
        compiler_params=_params(("parallel", "arbitrary"), 48),
        name="moe",
    )(x1, g_ffn.reshape(1, d), wr_hi, wr_lo, br, wg, wu, wd)


def _layer(x2d, mem2d, batch, seq, mem_len, g_mix, g_mem, w_in, w_gla_a2, b_gla_a2, gla_norm,
           w_mem_kv, dsa_q_norm, dsa_k_norm, idx_k_norm, mem_q_norm, mem_k_norm, b_gate,
           w_o_gla, w_o_dsa, w_o_mem, w_out, g_ffn, w_r1, b_r1, w_r2, b_r2, w_gate, w_up, w_down):
    n, d = x2d.shape
    sds = jax.ShapeDtypeStruct

    w_gla = w_in[:, _OFF["gq"][0]:_OFF["gg"][1]].astype(BF16)
    (zg,) = _proj_call("proj_gla", x2d, g_mix, w_gla, _epi_plain, [],
                       [sds((n, w_gla.shape[1]), BF16)], [1024], 1024, 1024, 40)

    (qn,) = _proj_call("proj_dsa_q", x2d, g_mix, _cols(w_in, "dq").astype(BF16),
                       functools.partial(_epi_headnorm, DSA_HEAD_DIM, DSA_HEAD_DIM ** -0.5 * LOG2_E),
                       [dsa_q_norm.reshape(1, -1)], [sds((n, 1024), BF16)], [1024], 1024, 1024, 40)
    (qm,) = _proj_call("proj_mem_q", x2d, g_mix, _cols(w_in, "mq").astype(BF16),
                       functools.partial(_epi_headnorm, MEM_HEAD_DIM, MEM_HEAD_DIM ** -0.5),
                       [mem_q_norm.reshape(1, -1)], [sds((n, 1024), BF16)], [1024], 1024, 1024, 40)
    w_iq = _cols(w_in, "iq").reshape(d, IDX_HEADS, IDX_DIM)
    w_iq = jnp.pad(w_iq, ((0, 0), (0, 0), (0, LANE - IDX_DIM))).reshape(d, IDX_HEADS * LANE).astype(BF16)
    (qi,) = _proj_call("proj_idx_q", x2d, g_mix, w_iq, _epi_plain, [],
                       [sds((n, IDX_HEADS * LANE), BF16)], [1024], 1024, 1024, 40)

    w_kv = jnp.concatenate([_cols(w_in, "dk"), _cols(w_in, "dv"), _pad_cols(_cols(w_in, "ik"), LANE),
                            _pad_cols(_cols(w_in, "iw"), LANE), _pad_cols(_cols(w_in, "ga"), LANE)],
                           axis=1).astype(BF16)
    kn, vv, ki, wi, ga = _proj_call(
        "proj_dsa_kv", x2d, g_mix, w_kv, _epi_dsa_kv,
        [dsa_k_norm.reshape(1, -1), _pad_cols(idx_k_norm.reshape(1, -1), LANE)],
        [sds((n, 256), BF16), sds((n, 256), BF16), sds((n, LANE), BF16), sds((n, LANE), F32),
         sds((n, LANE), BF16)],
        [256, 256, LANE, LANE, LANE], 1024, w_kv.shape[1], 40)

    wa = jnp.pad(w_gla_a2, ((0, LANE - GLA_GATE_RANK), (0, 0))).astype(BF16)
    o_gla = _gla_call(zg, ga, wa, b_gla_a2.reshape(1, -1), gla_norm.reshape(1, -1), batch, seq, 512)

    o_dsa = _dsa_call(qn, qi, wi, kn, vv, ki, batch, seq)

    (kv,) = _proj_call("proj_mem_kv", mem2d, g_mem, w_mem_kv.astype(BF16), _epi_mem_kv,
                       [mem_k_norm.reshape(1, -1)], [sds((mem2d.shape[0], 2 * d), BF16)], [1024],
                       mem2d.shape[0], 1024, 40)

    x1 = _merge_call(x2d, g_mix, o_gla, o_dsa, qm, kv, seq, mem_len, _cols(w_in, "gates").astype(BF16),
                     b_gate, w_o_gla.astype(BF16), w_o_dsa.astype(BF16), w_o_mem.astype(BF16),
                     w_out.astype(BF16), 512)

    w_r = _pad_cols(jnp.concatenate([w_r1, w_r2], axis=1), LANE)
    wr_hi = w_r.astype(BF16)
    wr_lo = (w_r - wr_hi.astype(F32)).astype(BF16)
    b_r = _pad_cols(jnp.concatenate([b_r1, b_r2]).reshape(1, -1), LANE)
    return _moe_call(x1, g_ffn, wr_hi, wr_lo, b_r, w_gate.astype(BF16), w_up.astype(BF16),
                     w_down.astype(BF16), 1024)


def kernel(x, mem, g_mix, g_mem, w_in, w_gla_a2, b_gla_a2, gla_norm, w_mem_kv, dsa_q_norm, dsa_k_norm, idx_k_norm, mem_q_norm, mem_k_norm, b_gate, w_o_gla, w_o_dsa, w_o_mem, w_out, g_ffn, w_r1, b_r1, w_r2, b_r2, w_gate, w_up, w_down):
    batch, seq, d = x.shape
    mem_len = mem.shape[1]
    x2d = x.reshape(batch * seq, d)
    mem2d = mem.reshape(batch * mem_len, d)
    params = (g_mix, g_mem, w_in, w_gla_a2, b_gla_a2, gla_norm, w_mem_kv, dsa_q_norm, dsa_k_norm,
              idx_k_norm, mem_q_norm, mem_k_norm, b_gate, w_o_gla, w_o_dsa, w_o_mem, w_out, g_ffn,
              w_r1, b_r1, w_r2, b_r2, w_gate, w_up, w_down)
    for layer in range(g_mix.shape[0]):
        x2d = _layer(x2d, mem2d, batch, seq, mem_len, *(p[layer] for p in params))
    return x2d.reshape(batch, seq, d)
```

```python
import functools

import jax
import jax.numpy as jnp
from jax import lax
from jax.experimental import pallas as pl
from jax.experimental.pallas import tpu as pltpu

F32 = jnp.float32
BF16 = jnp.bfloat16
I32 = jnp.int32

D_MODEL = 1024
EPS = 1e-6

GLA_HEADS = 4
GLA_DK = 128
GLA_DV = 256
GLA_GATE_RANK = 16
GLA_GATE_TEMP = 16.0
GLA_CHUNK = 64
GLA_UNROLL = 8

DSA_HEADS = 8
DSA_KV_HEADS = 2
DSA_HEAD_DIM = 128
IDX_HEADS = 8
IDX_DIM = 64
IDX_TOPK_MAX = 256

MEM_HEADS = 4
MEM_HEAD_DIM = 256

N_GROUPS = 4
EXPERTS_PER_GROUP = 4
N_EXPERTS = 16
D_EXPERT = 256

LANE = 128
PROJ_TM = 2048
PROJ_VMEM_MB = 48
MASKED_LOGIT = -1e30
LOG2_E = 1.4426950408889634

_OFF = {}
_acc = 0
for _name, _w in (("gq", 512), ("gk", 512), ("gv", 1024), ("gg", 1024), ("ga", 16), ("dq", 1024),
                  ("dk", 256), ("dv", 256), ("iq", 512), ("ik", 64), ("iw", 8), ("mq", 1024),
                  ("gates", 3072)):
    _OFF[_name] = (_acc, _acc + _w)
    _acc += _w


def _cols(w_in, name):
    lo, hi = _OFF[name]
    return w_in[:, lo:hi]


def _pad_cols(w, width):
    return jnp.pad(w, ((0, 0), (0, width - w.shape[1])))


def _rms_rows(x):
    return x * lax.rsqrt(jnp.mean(x * x, axis=-1, keepdims=True) + EPS)


def _dot(a, b):
    return jnp.dot(a, b, preferred_element_type=F32)


def _dot_nt(a, b):
    return lax.dot_general(a, b, (((1,), (1,)), ((), ())), preferred_element_type=F32)


def _dot_tn(a, b):
    return lax.dot_general(a, b, (((0,), (0,)), ((), ())), preferred_element_type=F32)


def _sigmoid(x):
    return 1.0 / (1.0 + jnp.exp(-x))


def _params(sem, vmem_mb):
    return pltpu.CompilerParams(dimension_semantics=sem, vmem_limit_bytes=vmem_mb * 1024 * 1024)


def _proj_body(epilogue, n_extra, x_ref, g_ref, w_ref, *rest):
    extra = rest[:n_extra]
    outs = rest[n_extra:-1]
    h_scr = rest[-1]

    @pl.when(pl.program_id(1) == 0)
    def _():
        h_scr[...] = (_rms_rows(x_ref[...]) * g_ref[...]).astype(BF16)

    epilogue(_dot(h_scr[...], w_ref[...]), extra, outs)


def _proj_call(name, x2d, gain, w, epilogue, extras, out_shapes, out_widths, tm, tn, vmem_mb):
    n, d = x2d.shape
    c = w.shape[1]
    in_specs = [pl.BlockSpec((tm, d), lambda i, j: (i, 0)),
                pl.BlockSpec((1, d), lambda i, j: (0, 0)),
                pl.BlockSpec((d, tn), lambda i, j: (0, j))]
    single = c == tn
    for e in extras:
        if e.shape[1] == c and not single:
            in_specs.append(pl.BlockSpec((1, tn), lambda i, j: (0, j)))
        else:
            in_specs.append(pl.BlockSpec(e.shape, lambda i, j: (0, 0)))
    out_specs = [pl.BlockSpec((tm, ow), (lambda i, j: (i, 0)) if single else (lambda i, j: (i, j)))
                 for ow in out_widths]
    return pl.pallas_call(
        functools.partial(_proj_body, epilogue, len(extras)),
        grid=(n // tm, c // tn),
        in_specs=in_specs,
        out_specs=out_specs,
        out_shape=out_shapes,
        scratch_shapes=[pltpu.VMEM((tm, d), BF16)],
        compiler_params=_params(("parallel", "arbitrary"), vmem_mb),
        name=name,
    )(x2d, gain.reshape(1, d), w, *extras)


def _epi_plain(acc, extra, outs):
    outs[0][...] = acc.astype(outs[0].dtype)


def _epi_headnorm(head_dim, scale, acc, extra, outs):
    gain = extra[0][...]
    for h in range(acc.shape[1] // head_dim):
        sl = slice(h * head_dim, (h + 1) * head_dim)
        outs[0][:, sl] = (_rms_rows(acc[:, sl]) * gain * scale).astype(outs[0].dtype)


def _epi_dsa_kv(acc, extra, outs):
    k_gain, idx_gain = extra
    kn_ref, v_ref, ki_ref, wi_ref, ga_ref = outs
    for h in range(DSA_KV_HEADS):
        sl = slice(h * DSA_HEAD_DIM, (h + 1) * DSA_HEAD_DIM)
        kn_ref[:, sl] = (_rms_rows(acc[:, sl]) * k_gain[...]).astype(BF16)
    v_ref[...] = acc[:, 256:512].astype(BF16)
    a = acc[:, 512:640]
    ms = jnp.sum(a * a, axis=-1, keepdims=True) * (1.0 / IDX_DIM)
    ki_ref[...] = (a * lax.rsqrt(ms + EPS) * idx_gain[...]).astype(BF16)
    wi_ref[...] = acc[:, 640:768] * (IDX_HEADS ** -0.5 * IDX_DIM ** -0.5)
    ga_ref[...] = acc[:, 768:896].astype(BF16)


def _epi_mem_kv(acc, extra, outs):
    gain = extra[0][...]
    j = pl.program_id(1)

    @pl.when(j == 0)
    def _():
        for h in range(MEM_HEADS):
            sl = slice(h * MEM_HEAD_DIM, (h + 1) * MEM_HEAD_DIM)
            outs[0][:, sl] = (_rms_rows(acc[:, sl]) * gain).astype(BF16)

    @pl.when(j == 1)
    def _():
        outs[0][...] = acc.astype(BF16)


def _split2(x):
    hi = x.astype(BF16)
    return hi, (x - hi.astype(F32)).astype(BF16)


def _gla_body(q_ref, k_ref, v_ref, g_ref, a_ref, wa_ref, ba_ref, ng_ref, o_ref, st_scr, la_scr):
    tb = q_ref.shape[0]
    c = GLA_CHUNK

    @pl.when(pl.program_id(1) == 0)
    def _():
        st_scr[...] = jnp.zeros_like(st_scr)

    pre = _dot(a_ref[...].astype(BF16), wa_ref[...]) + ba_ref[...]
    la_scr[...] = (jnp.minimum(pre, 0.0) - jnp.log1p(jnp.exp(-jnp.abs(pre)))) * (1.0 / GLA_GATE_TEMP)

    causal = lax.broadcasted_iota(I32, (c, c), 1) <= lax.broadcasted_iota(I32, (c, c), 0)
    tri = jnp.where(causal, 1.0, 0.0).astype(BF16)

    def chunk(ci):
        r0 = pl.multiple_of(ci * c, c)
        rows = pl.ds(r0, c)
        hi, lo = _split2(la_scr[rows, :])
        cum = _dot(tri, hi) + _dot(tri, lo)
        tot = cum[c - 1:c, :]
        kf = k_ref[rows, :]
        q_dec = (q_ref[rows, :] * (GLA_DK ** -0.5) * jnp.exp(cum)).astype(BF16)
        k_inv = (kf * jnp.exp(-cum)).astype(BF16)
        k_end = (kf * jnp.exp(tot - cum)).astype(BF16)
        decay = jnp.exp(tot)
        for h in range(GLA_HEADS):
            ks = slice(h * GLA_DK, (h + 1) * GLA_DK)
            vs = slice(h * GLA_DV, (h + 1) * GLA_DV)
            qd = q_dec[:, ks]
            vv = v_ref[rows, vs].astype(BF16)
            att = jnp.where(causal, _dot_nt(qd, k_inv[:, ks]), 0.0).astype(BF16)
            st = st_scr[h]
            o = _dot(att, vv) + _dot_nt(qd, st.astype(BF16))
            st_scr[h] = st * decay[:, ks] + _dot_tn(vv, k_end[:, ks])
            gg = g_ref[rows, vs].astype(F32)
            o_ref[rows, vs] = (_rms_rows(o) * ng_ref[...] * (gg * _sigmoid(gg))).astype(o_ref.dtype)

    def chunk_group(ji, carry):
        for u in range(GLA_UNROLL):
            chunk(GLA_UNROLL * ji + u)
        return carry

    lax.fori_loop(0, tb // (GLA_UNROLL * c), chunk_group, 0)


def _gla_call(zg, ga, wa, ba, ng, batch, seq, tb):
    n = zg.shape[0]
    nb = seq // tb
    row = lambda b, i: b * nb + i
    in_specs = [
        pl.BlockSpec((tb, 512), lambda b, i: (row(b, i), 0)),
        pl.BlockSpec((tb, 512), lambda b, i: (row(b, i), 1)),
        pl.BlockSpec((tb, 1024), lambda b, i: (row(b, i), 1)),
        pl.BlockSpec((tb, 1024), lambda b, i: (row(b, i), 2)),
        pl.BlockSpec((tb, LANE), lambda b, i: (row(b, i), 0)),
        pl.BlockSpec(wa.shape, lambda b, i: (0, 0)),
        pl.BlockSpec(ba.shape, lambda b, i: (0, 0)),
        pl.BlockSpec(ng.shape, lambda b, i: (0, 0)),
    ]
    hk = GLA_HEADS * GLA_DK
    return pl.pallas_call(
        _gla_body,
        grid=(batch, nb),
        in_specs=in_specs,
        out_specs=pl.BlockSpec((tb, GLA_HEADS * GLA_DV), lambda b, i: (row(b, i), 0)),
        out_shape=jax.ShapeDtypeStruct((n, GLA_HEADS * GLA_DV), BF16),
        scratch_shapes=[pltpu.VMEM((GLA_HEADS, GLA_DV, GLA_DK), F32),
                        pltpu.VMEM((tb, hk), F32)],
        compiler_params=_params(("parallel", "arbitrary"), 48),
        name="gla",
    )(zg, zg, zg, zg, ga, wa, ba, ng)


DSA_TQ = 256
DSA_TK = 512
DSA_COUNT_ROWS = 128
DSA_MAX_BISECT = 320
DSA_MIN_DENOM = 2.0 ** -100


def _dsa_body(topk, qn_ref, qi_ref, wi_ref, kn_ref, v_ref, ki_ref, o_ref,
              sc_scr, stat_scr, qa_scr, kmax_scr, s_scr, acc_scr):
    tq, tk, hd = DSA_TQ, DSA_TK, DSA_HEAD_DIM
    grp = DSA_HEADS // DSA_KV_HEADS
    kf = float(topk)
    q0 = pl.program_id(1) * tq
    nk = (q0 + tq + tk - 1) // tk
    qpos = q0 + lax.broadcasted_iota(I32, (tq, tk), 0)
    kofs = lax.broadcasted_iota(I32, (tq, tk), 1)
    inf = jnp.inf

    def chunk_start(ci):
        return pl.multiple_of(ci * tk, tk)

    @pl.when(pl.program_id(1) == 0)
    def _():
        def body(ci, mx):
            kc = kn_ref[pl.ds(chunk_start(ci), tk), :].astype(F32)
            return tuple(jnp.maximum(mx[g], jnp.sum(kc[:, g * hd:(g + 1) * hd] ** 2, axis=1, keepdims=True))
                         for g in range(DSA_KV_HEADS))

        mx = lax.fori_loop(0, kn_ref.shape[0] // tk, body,
                           tuple(jnp.zeros((tk, 1), F32) for _ in range(DSA_KV_HEADS)))
        for g in range(DSA_KV_HEADS):
            kmax_scr[g] = jnp.broadcast_to(jnp.max(mx[g], axis=0, keepdims=True), kmax_scr.shape[1:])

    wi = wi_ref[...]

    stat_scr[0] = jnp.full((tq, LANE), -inf, F32)
    stat_scr[1] = jnp.full((tq, LANE), inf, F32)
    stat_scr[2] = jnp.zeros((tq, LANE), F32)
    stat_scr[3] = jnp.zeros((tq, LANE), F32)

    def score_chunk(ci):
        k0 = chunk_start(ci)
        kc = ki_ref[pl.ds(k0, tk), :]
        score = jnp.zeros((tq, tk), F32)
        for h in range(IDX_HEADS):
            qh = qi_ref[:, h * LANE:(h + 1) * LANE]
            score = score + wi[:, h:h + 1] * jnp.maximum(_dot_nt(qh, kc), 0.0)
        valid = k0 + kofs <= qpos
        masked = jnp.where(valid, score, -inf)
        sc_scr[:, pl.ds(k0, tk)] = masked
        rmax, rmin, n_pos, n_nonneg = stat_scr[0], stat_scr[1], stat_scr[2], stat_scr[3]
        for t in range(tk // LANE):
            ts = slice(t * LANE, (t + 1) * LANE)
            rmax = jnp.maximum(rmax, masked[:, ts])
            rmin = jnp.minimum(rmin, jnp.where(valid[:, ts], score[:, ts], inf))
            n_pos = n_pos + jnp.where(masked[:, ts] > 0.0, 1.0, 0.0)
            n_nonneg = n_nonneg + jnp.where(masked[:, ts] >= 0.0, 1.0, 0.0)
        stat_scr[0], stat_scr[1], stat_scr[2], stat_scr[3] = rmax, rmin, n_pos, n_nonneg

    n_pairs = nk // 2
    odd_tail = nk % 2 == 1

    def score_pair(ji, carry):
        score_chunk(2 * ji)
        score_chunk(2 * ji + 1)
        return carry

    lax.fori_loop(0, n_pairs, score_pair, 0)

    @pl.when(odd_tail)
    def _():
        score_chunk(nk - 1)
        sc_scr[:, pl.ds(chunk_start(nk), tk)] = jnp.full((tq, tk), -inf, F32)
    hi0 = jnp.broadcast_to(jnp.max(stat_scr[0], axis=1, keepdims=True), (tq, LANE))
    lo0 = jnp.broadcast_to(jnp.min(stat_scr[1], axis=1, keepdims=True), (tq, LANE))
    ones_sum = jnp.ones((LANE, LANE), BF16)
    c_pos = _dot(stat_scr[2].astype(BF16), ones_sum)
    c_nn = _dot(stat_scr[3].astype(BF16), ones_sum)

    def count(*tests):
        partial = []
        for r0 in range(0, tq, DSA_COUNT_ROWS):
            rows = slice(r0, r0 + DSA_COUNT_ROWS)
            thr_b = [thr[rows] for _, thr in tests]

            def body(ji, cnts):
                sc = sc_scr[rows, pl.ds(pl.multiple_of(ji * 2 * tk, 2 * tk), 2 * tk)]
                out = []
                for (pred, _), tb, cnt in zip(tests, thr_b, cnts):
                    for t in range(2 * tk // LANE):
                        cnt = cnt + jnp.where(pred(sc[:, t * LANE:(t + 1) * LANE], tb), 1.0, 0.0)
                    out.append(cnt)
                return tuple(out)

            partial.append(lax.fori_loop(0, (nk + 1) // 2, body,
                                         tuple(jnp.zeros((DSA_COUNT_ROWS, LANE), F32) for _ in tests)))
        return [jnp.concatenate([_dot(p[k].astype(BF16), ones_sum) for p in partial], axis=0)
                for k in range(len(tests))]

    ge_pred = lambda a, b: a >= b
    n_causal = (q0 + 1 + lax.broadcasted_iota(I32, (tq, LANE), 0)).astype(F32)
    few = n_causal <= kf
    zero_tied = (c_pos < kf) & (c_nn >= kf)
    lo_init = jnp.where(few, lo0, jnp.where(c_nn >= kf, 0.0, lo0))
    hi_init = jnp.where(few | zero_tied, lo_init, jnp.where(c_nn < kf, 0.0, hi0))

    def bisect_cond(carry):
        it, _, _, pending = carry
        return (it < DSA_MAX_BISECT) & (pending > 0.0)

    def bisect(carry):
        it, lo, hi, _ = carry
        mid = 0.5 * lo + 0.5 * hi
        (cnt,) = count((ge_pred, mid))
        pending = jnp.max(jnp.where(mid > lo, jnp.where(mid < hi, 1.0, 0.0), 0.0))
        lo = jnp.where(cnt >= kf, mid, lo)
        hi = jnp.where(cnt > kf, hi, mid)
        return it + 1, lo, hi, pending

    _, thr, _, _ = lax.while_loop(bisect_cond, bisect, (jnp.int32(0), lo_init, hi_init, jnp.float32(1.0)))
    thr_w = jnp.tile(thr, (1, tk // LANE))
    (c_ge,) = count((ge_pred, thr))

    def break_ties(rows):
        nr = DSA_COUNT_ROWS
        thr_g, c_ge_g = thr_w[rows], c_ge[rows]

        def low_body(ci, low):
            sc = sc_scr[rows, pl.ds(chunk_start(ci), tk)]
            cand = jnp.where(sc >= thr_g, sc, inf)
            for t in range(tk // LANE):
                low = jnp.minimum(low, cand[:, t * LANE:(t + 1) * LANE])
            return low

        low = lax.fori_loop(0, nk, low_body, jnp.full((nr, LANE), inf, F32))
        tied_w = jnp.broadcast_to(jnp.min(low, axis=1, keepdims=True), (nr, tk))

        def eq_body(ci, cnt):
            sc = sc_scr[rows, pl.ds(chunk_start(ci), tk)]
            for t in range(tk // LANE):
                cnt = cnt + jnp.where(sc[:, t * LANE:(t + 1) * LANE] == tied_w[:, :LANE], 1.0, 0.0)
            return cnt

        c_eq = _dot(lax.fori_loop(0, nk, eq_body, jnp.zeros((nr, LANE), F32)).astype(BF16), ones_sum)
        allowed = jnp.tile(kf - (c_ge_g - c_eq), (1, tk // LANE))
        upper = jnp.where(lax.broadcasted_iota(I32, (tk, tk), 0) <= lax.broadcasted_iota(I32, (tk, tk), 1),
                          1.0, 0.0).astype(BF16)

        def body(ci, seen):
            k0 = chunk_start(ci)
            sc = sc_scr[rows, pl.ds(k0, tk)]
            eq = jnp.where(sc == tied_w, 1.0, 0.0).astype(BF16)
            rank = seen + _dot(eq, upper)
            drop = jnp.where(sc == tied_w, jnp.where(rank > allowed, 1.0, 0.0), 0.0) > 0.0
            sc_scr[rows, pl.ds(k0, tk)] = jnp.where(drop, -inf, sc)
            return jnp.broadcast_to(rank[:, tk - 1:tk], (nr, tk))

        lax.fori_loop(0, nk, body, jnp.zeros((nr, tk), F32))

    for r0 in range(0, tq, DSA_COUNT_ROWS):
        rows = slice(r0, r0 + DSA_COUNT_ROWS)
        pl.when(jnp.max(c_ge[rows]) > kf)(functools.partial(break_ties, rows))

    lane_q = lax.broadcasted_iota(I32, (tq, hd), 1)
    for g in range(DSA_KV_HEADS):
        kmax2 = kmax_scr[g][0:1, :]
        for r in range(grp):
            hcol = (g * grp + r) * hd
            rows = slice(r * tq, (r + 1) * tq)
            qh = qn_ref[:, hcol:hcol + hd]
            qf = qh.astype(F32)
            bound = jnp.sqrt(jnp.sum(qf * qf, axis=1, keepdims=True) * kmax2)
            qa_scr[g, rows, 0:hd] = qh
            qa_scr[g, rows, hd:2 * hd] = jnp.where(lane_q == 0, bound, 0.0).astype(BF16)
    acc_scr[...] = jnp.zeros_like(acc_scr)
    lane_k = lax.broadcasted_iota(I32, (tk, hd), 1)
    k_tail = jnp.where(lane_k == 0, -1.0, 0.0).astype(BF16)
    v_tail = jnp.where(lane_k == 0, 1.0, 0.0).astype(BF16)

    def sel_bias(k0):
        return jnp.where(sc_scr[:, pl.ds(k0, tk)] >= thr_w, 0.0, -inf)

    def logits(ci, g):
        ka = jnp.concatenate([kn_ref[pl.ds(chunk_start(ci), tk), g * hd:(g + 1) * hd], k_tail], axis=1)
        return _dot_nt(qa_scr[g], ka)

    def weighted_values(ci, g, s, bias):
        va = jnp.concatenate([v_ref[pl.ds(chunk_start(ci), tk), g * hd:(g + 1) * hd], v_tail], axis=1)
        p = jnp.exp2((s.reshape(grp, tq, tk) + bias[None]).reshape(grp * tq, tk))
        return _dot(p.astype(BF16), va)

    last_chunk = kn_ref.shape[0] // tk - 1
    for g in range(DSA_KV_HEADS):
        s_scr[g] = logits(0, g)

    def attend(ji, carry):
        ca = 2 * ji
        cb = ca + 1
        bias_a = sel_bias(chunk_start(ca))
        bias_b = sel_bias(chunk_start(cb))
        for g in range(DSA_KV_HEADS):
            s_a = s_scr[g]
            s_b = logits(cb, g)
            pv_a = weighted_values(ca, g, s_a, bias_a)
            s_scr[g] = logits(jnp.minimum(ca + 2, last_chunk), g)
            acc_scr[g] += pv_a + weighted_values(cb, g, s_b, bias_b)
        return carry

    lax.fori_loop(0, n_pairs, attend, 0)

    @pl.when(odd_tail)
    def _():
        bias = sel_bias(chunk_start(nk - 1))
        for g in range(DSA_KV_HEADS):
            acc_scr[g] += weighted_values(nk - 1, g, s_scr[g], bias)

    def write_out(g, out):
        for r in range(grp):
            hcol = (g * grp + r) * hd
            o_ref[:, hcol:hcol + hd] = out[r * tq:(r + 1) * tq, :].astype(o_ref.dtype)

    l_min = inf
    for g in range(DSA_KV_HEADS):
        acc = acc_scr[g]
        denom = acc[:, hd:hd + 1]
        l_min = jnp.minimum(l_min, jnp.min(denom))
        write_out(g, acc[:, 0:hd] / denom)

    @pl.when(jnp.logical_not(l_min >= DSA_MIN_DENOM))
    def _():
        for g in range(DSA_KV_HEADS):
            s_scr[g, :, 0:LANE] = jnp.full((grp * tq, LANE), MASKED_LOGIT, F32)
        acc_scr[...] = jnp.zeros_like(acc_scr)

        def attend_online(ci, carry):
            k0 = chunk_start(ci)
            bias = sel_bias(k0)
            for g in range(DSA_KV_HEADS):
                hs = slice(g * hd, (g + 1) * hd)
                s = _dot_nt(qa_scr[g, :, 0:hd], kn_ref[pl.ds(k0, tk), hs])
                s = (s.reshape(grp, tq, tk) + bias[None]).reshape(grp * tq, tk)
                m_prev = s_scr[g, :, 0:LANE]
                m_new = jnp.maximum(m_prev, jnp.max(s, axis=1, keepdims=True))
                p = jnp.exp2(s - jnp.tile(m_new, (1, tk // LANE)))
                alpha = jnp.exp2(m_prev - m_new)
                acc_scr[g, :, hd:2 * hd] = alpha * acc_scr[g, :, hd:2 * hd] + jnp.sum(p, axis=1, keepdims=True)
                acc_scr[g, :, 0:hd] = alpha * acc_scr[g, :, 0:hd] + _dot(p.astype(BF16), v_ref[pl.ds(k0, tk), hs])
                s_scr[g, :, 0:LANE] = m_new
            return carry

        lax.fori_loop(0, nk, attend_online, 0)
        for g in range(DSA_KV_HEADS):
            write_out(g, acc_scr[g, :, 0:hd] / acc_scr[g, :, hd:2 * hd])


def _dsa_call(qn, qi, wi, kn, vv, ki, batch, seq):
    n = qn.shape[0]
    tq, hd = DSA_TQ, DSA_HEAD_DIM
    nb = seq // tq
    topk = min(IDX_TOPK_MAX, seq // 4)
    grp = DSA_HEADS // DSA_KV_HEADS
    assert seq // LANE <= 256, "per-lane partial counts must stay exact in bf16"
    assert seq % (2 * DSA_TK) == 0 and seq % tq == 0 and tq % DSA_COUNT_ROWS == 0
    row = lambda b, i: (b * nb + i, 0)
    per_batch = lambda b, i: (b, 0)
    return pl.pallas_call(
        functools.partial(_dsa_body, topk),
        grid=(batch, nb),
        in_specs=[pl.BlockSpec((tq, qn.shape[1]), row),
                  pl.BlockSpec((tq, qi.shape[1]), row),
                  pl.BlockSpec((tq, LANE), row),
                  pl.BlockSpec((seq, kn.shape[1]), per_batch, pipeline_mode=pl.Buffered(1)),
                  pl.BlockSpec((seq, vv.shape[1]), per_batch, pipeline_mode=pl.Buffered(1)),
                  pl.BlockSpec((seq, LANE), per_batch, pipeline_mode=pl.Buffered(1))],
        out_specs=pl.BlockSpec((tq, DSA_HEADS * hd), row),
        out_shape=jax.ShapeDtypeStruct((n, DSA_HEADS * hd), BF16),
        scratch_shapes=[pltpu.VMEM((tq, seq), F32),
                        pltpu.VMEM((4, tq, LANE), F32),
                        pltpu.VMEM((DSA_KV_HEADS, grp * tq, 2 * hd), BF16),
                        pltpu.VMEM((DSA_KV_HEADS, 8, LANE), F32),
                        pltpu.VMEM((DSA_KV_HEADS, grp * tq, DSA_TK), F32),
                        pltpu.VMEM((DSA_KV_HEADS, grp * tq, 2 * hd), F32)],
        compiler_params=_params(("arbitrary", "arbitrary"), 48),
        name="dsa",
    )(qn, qi, wi, kn, vv, ki)


def _mem_attention(q_ref, k_ref, v_ref):
    heads = []
    for h in range(MEM_HEADS):
        sl = slice(h * MEM_HEAD_DIM, (h + 1) * MEM_HEAD_DIM)
        s = _dot_nt(q_ref[:, sl], k_ref[:, sl])
        p = jnp.exp(s - jnp.max(s, axis=1, keepdims=True))
        o = _dot(p.astype(BF16), v_ref[:, sl]) / jnp.sum(p, axis=1, keepdims=True)
        heads.append(o.astype(BF16))
    return jnp.concatenate(heads, axis=1)


def _merge_body(x_ref, gmix_ref, og_ref, od_ref, qm_ref, km_ref, vm_ref, wgate_ref, bgate_ref,
                wg_ref, wd_ref, wm_ref, wo_ref, o_ref):
    d = D_MODEL
    x = x_ref[...]
    h = (_rms_rows(x) * gmix_ref[...]).astype(BF16)
    o_mem = _mem_attention(qm_ref, km_ref, vm_ref)
    branches = (_dot(og_ref[...], wg_ref[...]), _dot(od_ref[...], wd_ref[...]), _dot(o_mem, wm_ref[...]))
    merged = None
    for j, br in enumerate(branches):
        cols = slice(j * d, (j + 1) * d)
        gate = _sigmoid(_dot(h, wgate_ref[:, cols]) + bgate_ref[:, cols])
        merged = gate * br if merged is None else merged + gate * br
    o_ref[...] = x + _dot(merged.astype(BF16), wo_ref[...])


def _merge_call(x2d, g_mix, o_gla, o_dsa, qm, kv, seq, mem_len, w_gate, b_gate, wg, wd, wm, wo, tm):
    n, d = x2d.shape
    row = lambda i: (i, 0)
    fixed = lambda i: (0, 0)
    per_batch = seq // tm
    return pl.pallas_call(
        _merge_body,
        grid=(n // tm,),
        in_specs=[pl.BlockSpec((tm, d), row), pl.BlockSpec((1, d), fixed),
                  pl.BlockSpec((tm, d), row), pl.BlockSpec((tm, d), row), pl.BlockSpec((tm, d), row),
                  pl.BlockSpec((mem_len, d), lambda i: (i // per_batch, 0)),
                  pl.BlockSpec((mem_len, d), lambda i: (i // per_batch, 1)),
                  pl.BlockSpec((d, 3 * d), fixed), pl.BlockSpec((1, 3 * d), fixed),
                  pl.BlockSpec((d, d), fixed), pl.BlockSpec((d, d), fixed),
                  pl.BlockSpec((d, d), fixed), pl.BlockSpec((d, d), fixed)],
        out_specs=pl.BlockSpec((tm, d), row),
        out_shape=jax.ShapeDtypeStruct((n, d), F32),
        compiler_params=_params(("parallel",), 56),
        name="merge",
    )(x2d, g_mix.reshape(1, d), o_gla, o_dsa, qm, kv, kv, w_gate, b_gate.reshape(1, -1), wg, wd, wm, wo)


def _route(z):
    lane = lax.broadcasted_iota(I32, z.shape, 1)
    lanef = lane.astype(F32)
    far = 1e9
    neg = -jnp.inf
    gmask = lane < N_GROUPS
    gmax = jnp.max(jnp.where(gmask, z, neg), axis=1, keepdims=True)
    gsum = jnp.sum(jnp.where(gmask, jnp.exp(z - gmax), 0.0), axis=1, keepdims=True)
    g_w = 1.0 / gsum
    g_sel = jnp.min(jnp.where(gmask & (z == gmax), lanef, far), axis=1, keepdims=True)
    lane_grp = ((lane - N_GROUPS) >> 2).astype(F32)
    emask = (lane >= N_GROUPS) & (lane < N_GROUPS + N_EXPERTS) & (lane_grp == g_sel)
    e1 = jnp.max(jnp.where(emask, z, neg), axis=1, keepdims=True)
    i1 = jnp.min(jnp.where(emask & (z == e1), lanef, far), axis=1, keepdims=True)
    emask2 = emask & (lanef != i1)
    e2 = jnp.max(jnp.where(emask2, z, neg), axis=1, keepdims=True)
    i2 = jnp.min(jnp.where(emask2 & (z == e2), lanef, far), axis=1, keepdims=True)
    t = jnp.exp(e2 - e1)
    p1 = 1.0 / (1.0 + t)
    p2 = t / (1.0 + t)
    return jnp.where(lanef == i1, g_w * p1, jnp.where(lanef == i2, g_w * p2, 0.0))


def _moe_body(x_ref, g_ref, wrh_ref, wrl_ref, br_ref, wg_ref, wu_ref, wd_ref, o_ref, h_scr, comb_scr):
    grp = pl.program_id(1)

    @pl.when(grp == 0)
    def _():
        x = x_ref[...]
        h = _rms_rows(x) * g_ref[...]
        hh = h.astype(BF16)
        hl = (h - hh.astype(F32)).astype(BF16)
        h_scr[...] = hh
        z = _dot(hh, wrh_ref[...]) + _dot(hl, wrh_ref[...]) + _dot(hh, wrl_ref[...]) + br_ref[...]
        comb_scr[...] = _route(z)
        o_ref[...] = x

    lane = lax.broadcasted_iota(I32, comb_scr.shape, 1)
    comb = comb_scr[...]
    h = h_scr[...]
    hidden = []
    for j in range(EXPERTS_PER_GROUP):
        first_lane = N_GROUPS + grp * EXPERTS_PER_GROUP + j
        ce = jnp.sum(jnp.where(lane == first_lane, comb, 0.0), axis=1, keepdims=True)
        gt = _dot(h, wg_ref[j])
        up = _dot(h, wu_ref[j])
        hidden.append((gt * _sigmoid(gt) * up * ce).astype(BF16))
    o_ref[...] += _dot(jnp.concatenate(hidden, axis=1), wd_ref[...])


def _moe_call(x1, g_ffn, wr_hi, wr_lo, br, wg, wu, wd, tm):
    n, d = x1.shape
    row = lambda i, e: (i, 0)
    fixed = lambda i, e: (0, 0)
    per_group = lambda i, e: (e, 0, 0)
    epg = EXPERTS_PER_GROUP
    wd = wd.reshape(N_EXPERTS * D_EXPERT, d)
    return pl.pallas_call(
        _moe_body,
        grid=(n // tm, N_GROUPS),
        in_specs=[pl.BlockSpec((tm, d), row), pl.BlockSpec((1, d), fixed),
                  pl.BlockSpec((d, LANE), fixed), pl.BlockSpec((d, LANE), fixed),
                  pl.BlockSpec((1, LANE), fixed),
                  pl.BlockSpec((epg, d, D_EXPERT), per_group), pl.BlockSpec((epg, d, D_EXPERT), per_group),
                  pl.BlockSpec((epg * D_EXPERT, d), lambda i, e: (e, 0))],
        out_specs=pl.BlockSpec((tm, d), row),
        out_shape=jax.ShapeDtypeStruct((n, d), F32),
        scratch_shapes=[pltpu.VMEM((tm, d), BF16), pltpu.VMEM((tm, LANE), F32)],
        compiler_params=_params(("parallel", "arbitrary"), 48),
        name="moe",
    )(x1, g_ffn.reshape(1, d), wr_hi, wr_lo, br, wg, wu, wd)


def _layer(x2d, mem2d, batch, seq, mem_len, g_mix, g_mem, w_in, w_gla_a2, b_gla_a2, gla_norm,
           w_mem_kv, dsa_q_norm, dsa_k_norm, idx_k_norm, mem_q_norm, mem_k_norm, b_gate,
           w_o_gla, w_o_dsa, w_o_mem, w_out, g_ffn, w_r1, b_r1, w_r2, b_r2, w_gate, w_up, w_down):
    n, d = x2d.shape
    sds = jax.ShapeDtypeStruct

    w_gla = w_in[:, _OFF["gq"][0]:_OFF["gg"][1]].astype(BF16)
    (zg,) = _proj_call("proj_gla", x2d, g_mix, w_gla, _epi_plain, [],
                       [sds((n, w_gla.shape[1]), BF16)], [1024], PROJ_TM, 1024, PROJ_VMEM_MB)

    (qn,) = _proj_call("proj_dsa_q", x2d, g_mix, _cols(w_in, "dq").astype(BF16),
                       functools.partial(_epi_headnorm, DSA_HEAD_DIM, DSA_HEAD_DIM ** -0.5 * LOG2_E),
                       [dsa_q_norm.reshape(1, -1)], [sds((n, 1024), BF16)], [1024], PROJ_TM, 1024, PROJ_VMEM_MB)
    (qm,) = _proj_call("proj_mem_q", x2d, g_mix, _cols(w_in, "mq").astype(BF16),
                       functools.partial(_epi_headnorm, MEM_HEAD_DIM, MEM_HEAD_DIM ** -0.5),
                       [mem_q_norm.reshape(1, -1)], [sds((n, 1024), BF16)], [1024], PROJ_TM, 1024, PROJ_VMEM_MB)
    w_iq = _cols(w_in, "iq").reshape(d, IDX_HEADS, IDX_DIM)
    w_iq = jnp.pad(w_iq, ((0, 0), (0, 0), (0, LANE - IDX_DIM))).reshape(d, IDX_HEADS * LANE).astype(BF16)
    (qi,) = _proj_call("proj_idx_q", x2d, g_mix, w_iq, _epi_plain, [],
                       [sds((n, IDX_HEADS * LANE), BF16)], [1024], PROJ_TM, 1024, PROJ_VMEM_MB)

    w_kv = jnp.concatenate([_cols(w_in, "dk"), _cols(w_in, "dv"), _pad_cols(_cols(w_in, "ik"), LANE),
                            _pad_cols(_cols(w_in, "iw"), LANE), _pad_cols(_cols(w_in, "ga"), LANE)],
                           axis=1).astype(BF16)
    kn, vv, ki, wi, ga = _proj_call(
        "proj_dsa_kv", x2d, g_mix, w_kv, _epi_dsa_kv,
        [dsa_k_norm.reshape(1, -1), _pad_cols(idx_k_norm.reshape(1, -1), LANE)],
        [sds((n, 256), BF16), sds((n, 256), BF16), sds((n, LANE), BF16), sds((n, LANE), F32),
         sds((n, LANE), BF16)],
        [256, 256, LANE, LANE, LANE], PROJ_TM, w_kv.shape[1], PROJ_VMEM_MB)

    wa = jnp.pad(w_gla_a2, ((0, LANE - GLA_GATE_RANK), (0, 0))).astype(BF16)
    o_gla = _gla_call(zg, ga, wa, b_gla_a2.reshape(1, -1), gla_norm.reshape(1, -1), batch, seq, 512)

    o_dsa = _dsa_call(qn, qi, wi, kn, vv, ki, batch, seq)

    (kv,) = _proj_call("proj_mem_kv", mem2d, g_mem, w_mem_kv.astype(BF16), _epi_mem_kv,
                       [mem_k_norm.reshape(1, -1)], [sds((mem2d.shape[0], 2 * d), BF16)], [1024],
                       mem2d.shape[0], 1024, 40)

    x1 = _merge_call(x2d, g_mix, o_gla, o_dsa, qm, kv, seq, mem_len, _cols(w_in, "gates").astype(BF16),
                     b_gate, w_o_gla.astype(BF16), w_o_dsa.astype(BF16), w_o_mem.astype(BF16),
                     w_out.astype(BF16), 512)

    w_r = _pad_cols(jnp.concatenate([w_r1, w_r2], axis=1), LANE)
    wr_hi = w_r.astype(BF16)
    wr_lo = (w_r - wr_hi.astype(F32)).astype(BF16)
    b_r = _pad_cols(jnp.concatenate([b_r1, b_r2]).reshape(1, -1), LANE)
    return _moe_call(x1, g_ffn, wr_hi, wr_lo, b_r, w_gate.astype(BF16), w_up.astype(BF16),
                     w_down.astype(BF16), 1024)


def kernel(x, mem, g_mix, g_mem, w_in, w_gla_a2, b_gla_a2, gla_norm, w_mem_kv, dsa_q_norm, dsa_k_norm, idx_k_norm, mem_q_norm, mem_k_norm, b_gate, w_o_gla, w_o_dsa, w_o_mem, w_out, g_ffn, w_r1, b_r1, w_r2, b_r2, w_gate, w_up, w_down):
    batch, seq, d = x.shape
    mem_len = mem.shape[1]
    x2d = x.reshape(batch * seq, d)
    mem2d = mem.reshape(batch * mem_len, d)
    params = (g_mix, g_mem, w_in, w_gla_a2, b_gla_a2, gla_norm, w_mem_kv, dsa_q_norm, dsa_k_norm,
              idx_k_norm, mem_q_norm, mem_k_norm, b_gate, w_o_gla, w_o_dsa, w_o_mem, w_out, g_ffn,
              w_r1, b_r1, w_r2, b_r2, w_gate, w_up, w_down)
    for layer in range(g_mix.shape[0]):
        x2d = _layer(x2d, mem2d, batch, seq, mem_len, *(p[layer] for p in params))
    return x2d.reshape(batch, seq, d)
```
